```python
import functools
import jax, jax.numpy as jnp
from jax import lax
import numpy as np

D_MODEL = 1024
BATCH = 4
SEQ = 4096
DEPTH = 1
DEC_BATCH = 32
DEC_SEQ = 64
PAST_LEN = 1024

CHUNK = 64
N_HEADS = 16
N_KV_HEADS = 4
HEAD_DIM = 64
GROUP = N_HEADS // N_KV_HEADS
WINDOW = 128
N_BACK = WINDOW // CHUNK
Q_W = N_HEADS * HEAD_DIM
KV_W = N_KV_HEADS * HEAD_DIM
D_RNN = 1280
N_RNN_BLOCKS = 10
RNN_BLOCK = D_RNN // N_RNN_BLOCKS
CONV_W = 4
LRU_C = 8.0
D_FF = ((8 * D_MODEL // 3 + 255) // 256) * 256
IN_SIZES = (Q_W, KV_W, KV_W, D_RNN, D_RNN, D_MODEL, D_MODEL)
IN_COLS = Q_W + 2 * KV_W + 2 * D_RNN + 2 * D_MODEL
EPS = 1e-6
NEG_INF = -1e30

kernel_name = 'hybrid_swa_sink_rglru_stream_step'


def _rmsnorm(x, g):
    x32 = x.astype(jnp.float32)
    y = x32 * lax.rsqrt(jnp.mean(x32 * x32, axis=-1, keepdims=True) + EPS) * g.astype(jnp.float32)
    return y.astype(x.dtype)


def _alibi_slopes():
    h = jnp.arange(1, N_HEADS + 1, dtype=jnp.float32)
    return jnp.exp2(-8.0 * h / N_HEADS).reshape(N_KV_HEADS, GROUP)


def _sink_attention(q, k, v, qpos, kpos, sinks):
    s = jnp.einsum('bnqhgd,bnshd->bnhgqs', q, k,
                   preferred_element_type=jnp.float32) * (HEAD_DIM ** -0.5)
    dist = jnp.abs(qpos[:, :, None] - kpos[:, None, :]).astype(jnp.float32)
    s = s - _alibi_slopes()[None, None, :, :, None, None] * dist[None, :, None, None]
    qc = (qpos // CHUNK)[:, :, None]
    kc = (kpos // CHUNK)[:, None, :]
    valid = (kpos[:, None, :] >= 0) & (kc <= qc) & (kc >= qc - N_BACK)
    s = jnp.where(valid[None, :, None, None], s, NEG_INF)
    sink = jnp.broadcast_to(sinks.astype(jnp.float32).reshape(1, 1, N_KV_HEADS, GROUP, 1, 1),
                            s.shape[:-1] + (1,))
    p = jax.nn.softmax(jnp.concatenate([s, sink], axis=-1), axis=-1)[..., :-1]
    return jnp.einsum('bnhgqs,bnshd->bnqhgd', p.astype(v.dtype), v)


def _attn_prompt(q, k, v, sinks):
    B, S = q.shape[:2]
    nc = S // CHUNK
    kp = jnp.pad(k, ((0, 0), (WINDOW, 0), (0, 0), (0, 0)))
    vp = jnp.pad(v, ((0, 0), (WINDOW, 0), (0, 0), (0, 0)))
    kch = kp.reshape(B, nc + N_BACK, CHUNK, N_KV_HEADS, HEAD_DIM)
    vch = vp.reshape(B, nc + N_BACK, CHUNK, N_KV_HEADS, HEAD_DIM)
    kb = jnp.concatenate([kch[:, j:j + nc] for j in range(N_BACK + 1)], axis=2)
    vb = jnp.concatenate([vch[:, j:j + nc] for j in range(N_BACK + 1)], axis=2)
    qb = q.reshape(B, nc, CHUNK, N_KV_HEADS, GROUP, HEAD_DIM)
    qpos = jnp.arange(nc)[:, None] * CHUNK + jnp.arange(CHUNK)[None, :]
    kpos = (jnp.arange(nc)[:, None] - N_BACK) * CHUNK + jnp.arange(WINDOW + CHUNK)[None, :]
    o = _sink_attention(qb, kb, vb, qpos, kpos, sinks).reshape(B, S, Q_W)
    return o, kp[:, -WINDOW:], vp[:, -WINDOW:]


def _attn_sample(q, k, v, sinks, k_cache, v_cache):
    B, L = q.shape[:2]
    kk = jnp.concatenate([k_cache.astype(k.dtype), k], axis=1)
    vv = jnp.concatenate([v_cache.astype(v.dtype), v], axis=1)
    qpos = (PAST_LEN + jnp.arange(L))[None, :]
    kpos = (PAST_LEN - WINDOW + jnp.arange(WINDOW + L))[None, :]
    o = _sink_attention(q[:, None], kk[:, None], vv[:, None], qpos, kpos, sinks).reshape(B, L, Q_W)
    return o, kk[:, -WINDOW:], vv[:, -WINDOW:]


def _causal_conv(xr, buf, w, b):
    L = xr.shape[1]
    xp = jnp.concatenate([buf.astype(xr.dtype), xr], axis=1)
    out = b + sum(xp[:, j:j + L] * w[j] for j in range(CONV_W))
    return out, xp[:, -(CONV_W - 1):]


def _rglru(xr, h0, w_a, b_a, w_x, b_x, lam):
    B, L, _ = xr.shape
    x32 = xr.astype(jnp.float32)
    xb = x32.reshape(B, L, N_RNN_BLOCKS, RNN_BLOCK)
    r = jax.nn.sigmoid(jnp.einsum('blnc,ncd->blnd', xb, w_a.astype(jnp.float32)).reshape(B, L, D_RNN)
                       + b_a.astype(jnp.float32))
    i = jax.nn.sigmoid(jnp.einsum('blnc,ncd->blnd', xb, w_x.astype(jnp.float32)).reshape(B, L, D_RNN)
                       + b_x.astype(jnp.float32))
    log_a = -LRU_C * r * jax.nn.softplus(-lam.astype(jnp.float32))
    a = jnp.exp(log_a)
    bterm = jnp.sqrt(-jnp.expm1(2.0 * log_a)) * (i * x32)
    bterm = bterm.at[:, 0].add(a[:, 0] * h0.astype(jnp.float32))

    def comb(left, right):
        a1, b1 = left
        a2, b2 = right
        return a1 * a2, a2 * b1 + b2

    _, h = lax.associative_scan(comb, (a, bterm), axis=1)
    return h, h[:, -1]


def _layer(x, c, attn_fn, conv_buf, h0, w_ada, b_ada, g_pre_mix, g_post_mix, w_in, attn_sinks,
           w_conv, b_conv, w_rg_a, b_rg_a, w_rg_x, b_rg_x, rg_lambda, w_attn_o, w_rnn_o, w_out,
           g_pre_ffn, g_post_ffn, w_ffn_gate, w_ffn_up, w_ffn_down):
    B, L, _ = x.shape
    mod = (jax.nn.silu(c) @ w_ada + b_ada).reshape(B, 6, D_MODEL)[:, :, None, :]
    sh_m, sc_m, gt_m, sh_f, sc_f, gt_f = (mod[:, j] for j in range(6))

    u = _rmsnorm(x, g_pre_mix) * (1.0 + sc_m) + sh_m
    z = u @ w_in
    parts = []
    off = 0
    for n in IN_SIZES:
        parts.append(z[..., off:off + n])
        off += n
    q, k, v, xr, yr, ga, gr = parts
    q = q.reshape(B, L, N_KV_HEADS, GROUP, HEAD_DIM)
    k = k.reshape(B, L, N_KV_HEADS, HEAD_DIM)
    v = v.reshape(B, L, N_KV_HEADS, HEAD_DIM)
    attn_o, k_state, v_state = attn_fn(q, k, v, attn_sinks)
    xc, conv_state = _causal_conv(xr, conv_buf, w_conv, b_conv)
    h, h_last = _rglru(xc, h0, w_rg_a, b_rg_a, w_rg_x, b_rg_x, rg_lambda)
    rnn_o = h.astype(x.dtype) * jax.nn.gelu(yr)
    merged = jax.nn.sigmoid(ga) * (attn_o @ w_attn_o) + jax.nn.sigmoid(gr) * (rnn_o @ w_rnn_o)
    x = x + gt_m * _rmsnorm(merged @ w_out, g_post_mix)

    u = _rmsnorm(x, g_pre_ffn) * (1.0 + sc_f) + sh_f
    f = (jax.nn.silu(u @ w_ffn_gate) * (u @ w_ffn_up)) @ w_ffn_down
    x = x + gt_f * _rmsnorm(f, g_post_ffn)
    return x, k_state, v_state, conv_state, h_last


def setup_inputs(seed: int = 0) -> dict:
    key = jax.random.key(seed)
    ks = iter(jax.random.split(key, 40))

    def nrm(shape, scale):
        return jax.random.normal(next(ks), shape, jnp.float32) * scale

    def gain(shape):
        return 1.0 + 0.01 * jax.random.normal(next(ks), shape, jnp.float32)

    u = jax.random.uniform(next(ks), (DEPTH, D_RNN), jnp.float32, minval=0.9, maxval=0.999)
    return {
        'x_prompt': nrm((BATCH, SEQ, D_MODEL), 1.0),
        'x_sample': nrm((DEC_BATCH, DEC_SEQ, D_MODEL), 1.0),
        'c_prompt': nrm((BATCH, D_MODEL), 1.0),
        'c_sample': nrm((DEC_BATCH, D_MODEL), 1.0),
        'cache_k': nrm((DEPTH, DEC_BATCH, WINDOW, N_KV_HEADS, HEAD_DIM), 1.0),
        'cache_v': nrm((DEPTH, DEC_BATCH, WINDOW, N_KV_HEADS, HEAD_DIM), 1.0),
        'state_conv': nrm((DEPTH, DEC_BATCH, CONV_W - 1, D_RNN), 1.0),
        'state_h': nrm((DEPTH, DEC_BATCH, D_RNN), 0.5),
        'w_ada': nrm((DEPTH, D_MODEL, 6 * D_MODEL), 0.5 * D_MODEL ** -0.5),
        'b_ada': nrm((DEPTH, 6 * D_MODEL), 0.01),
        'g_pre_mix': gain((DEPTH, D_MODEL)),
        'g_post_mix': gain((DEPTH, D_MODEL)),
        'w_in': nrm((DEPTH, D_MODEL, IN_COLS), D_MODEL ** -0.5),
        'attn_sinks': nrm((DEPTH, N_HEADS), 1.0),
        'w_conv': nrm((DEPTH, CONV_W, D_RNN), CONV_W ** -0.5),
        'b_conv': nrm((DEPTH, D_RNN), 0.01),
        'w_rg_a': nrm((DEPTH, N_RNN_BLOCKS, RNN_BLOCK, RNN_BLOCK), RNN_BLOCK ** -0.5),
        'b_rg_a': nrm((DEPTH, D_RNN), 0.01),
        'w_rg_x': nrm((DEPTH, N_RNN_BLOCKS, RNN_BLOCK, RNN_BLOCK), RNN_BLOCK ** -0.5),
        'b_rg_x': nrm((DEPTH, D_RNN), 0.01),
        'rg_lambda': jnp.log(u) - jnp.log1p(-u),
        'w_attn_o': nrm((DEPTH, Q_W, D_MODEL), Q_W ** -0.5),
        'w_rnn_o': nrm((DEPTH, D_RNN, D_MODEL), D_RNN ** -0.5),
        'w_out': nrm((DEPTH, D_MODEL, D_MODEL), D_MODEL ** -0.5),
        'g_pre_ffn': gain((DEPTH, D_MODEL)),
        'g_post_ffn': gain((DEPTH, D_MODEL)),
        'w_ffn_gate': nrm((DEPTH, D_MODEL, D_FF), D_MODEL ** -0.5),
        'w_ffn_up': nrm((DEPTH, D_MODEL, D_FF), D_MODEL ** -0.5),
        'w_ffn_down': nrm((DEPTH, D_FF, D_MODEL), D_FF ** -0.5),
    }


def reference(x_prompt, x_sample, c_prompt, c_sample, cache_k, cache_v, state_conv, state_h,
              w_ada, b_ada, g_pre_mix, g_post_mix, w_in, attn_sinks, w_conv, b_conv,
              w_rg_a, b_rg_a, w_rg_x, b_rg_x, rg_lambda, w_attn_o, w_rnn_o, w_out,
              g_pre_ffn, g_post_ffn, w_ffn_gate, w_ffn_up, w_ffn_down):
    layer_params = (w_ada, b_ada, g_pre_mix, g_post_mix, w_in, attn_sinks, w_conv, b_conv,
                    w_rg_a, b_rg_a, w_rg_x, b_rg_x, rg_lambda, w_attn_o, w_rnn_o, w_out,
                    g_pre_ffn, g_post_ffn, w_ffn_gate, w_ffn_up, w_ffn_down)
    xp, xs = x_prompt, x_sample
    kps, vps, cps, hps, kss, vss, css, hss = [], [], [], [], [], [], [], []
    for l in range(DEPTH):
        lp = [w[l] for w in layer_params]
        conv0 = jnp.zeros((xp.shape[0], CONV_W - 1, D_RNN), xp.dtype)
        h0 = jnp.zeros((xp.shape[0], D_RNN), jnp.float32)
        xp, kp_, vp_, cp_, hp_ = _layer(xp, c_prompt, _attn_prompt, conv0, h0, *lp)
        attn_s = functools.partial(_attn_sample, k_cache=cache_k[l], v_cache=cache_v[l])
        xs, ks_, vs_, cs_, hs_ = _layer(xs, c_sample, attn_s, state_conv[l], state_h[l], *lp)
        kps.append(kp_); vps.append(vp_); cps.append(cp_); hps.append(hp_)
        kss.append(ks_); vss.append(vs_); css.append(cs_); hss.append(hs_)
    return (xp, xs, jnp.stack(kps), jnp.stack(vps), jnp.stack(cps), jnp.stack(hps),
            jnp.stack(kss), jnp.stack(vss), jnp.stack(css), jnp.stack(hss))
```

```python
import functools
import math

import jax
import jax.numpy as jnp
from jax import lax
from jax.experimental import pallas as pl
from jax.experimental.pallas import tpu as pltpu

CHUNK = 64
N_HEADS = 16
N_KV_HEADS = 4
HEAD_DIM = 64
GROUP = N_HEADS // N_KV_HEADS
WINDOW = 128
N_BACK = WINDOW // CHUNK
KV_W = N_KV_HEADS * HEAD_DIM
RNN_BLOCK = 128
CONV_W = 4
LRU_C = 8.0
EPS = 1e-6
NEG_INF = -1e30

LANES = 128
SUBLANES = 8
KEY_WIN = WINDOW + 2 * CHUNK
VMEM_LIMIT_BYTES = 56 * 1024 * 1024

_F32 = jnp.float32
_BF16 = jnp.bfloat16


def _dot(a, b):
    return lax.dot_general(a, b, (((1,), (0,)), ((), ())), preferred_element_type=_F32)


def _dot_nt(a, b):
    return lax.dot_general(a, b, (((1,), (1,)), ((), ())), preferred_element_type=_F32)


def _rms(x):
    return x * lax.rsqrt(jnp.mean(x * x, axis=-1, keepdims=True) + EPS)


def _softplus(x):
    return jnp.maximum(x, 0.0) + jnp.log1p(jnp.exp(-jnp.abs(x)))


def _neg_expm1(y):
    series = y * (1.0 + y * (0.5 + y * (1.0 / 6.0 + y * (1.0 / 24.0 + y * (1.0 / 120.0)))))
    return jnp.where(y > -0.0625, -series, 1.0 - jnp.exp(y))


def _gelu_tanh(x):
    return 0.5 * x * (1.0 + jnp.tanh(math.sqrt(2.0 / math.pi) * (x + 0.044715 * (x * x * x))))


def _dup_heads(kv):
    n = kv.shape[0]
    lo = lax.broadcasted_iota(jnp.int32, (n, LANES), 1) < HEAD_DIM
    outs = []
    for blk in range(KV_W // LANES):
        b = kv[:, blk * LANES:(blk + 1) * LANES]
        r = pltpu.roll(b, HEAD_DIM, axis=1)
        outs.append(jnp.where(lo, b, r).astype(_BF16))
        outs.append(jnp.where(lo, r, b).astype(_BF16))
    return outs


def _ada_kernel(c_ref, w_ref, b_ref, o_ref):
    c = c_ref[...]
    a = (c * jax.nn.sigmoid(c)).astype(_BF16)
    o_ref[...] = _dot(a, w_ref[...].astype(_BF16)) + b_ref[...]


def _ada_call(c, w_ada, b_ada):
    rows, d = c.shape
    n = w_ada.shape[1]
    bn = d
    return pl.pallas_call(
        _ada_kernel,
        grid=(n // bn,),
        in_specs=[
            pl.BlockSpec((rows, d), lambda j: (0, 0)),
            pl.BlockSpec((d, bn), lambda j: (0, j)),
            pl.BlockSpec((1, bn), lambda j: (0, j)),
        ],
        out_specs=pl.BlockSpec((rows, bn), lambda j: (0, j)),
        out_shape=jax.ShapeDtypeStruct((rows, n), _F32),
        compiler_params=pltpu.CompilerParams(dimension_semantics=("arbitrary",)),
        name="ada",
    )(c, w_ada, b_ada.reshape(1, n))


def _mixer_kernel(carry, ns, ls, tiles_per_seq, d_model, d_rnn, *refs):
    if carry:
        (x_ref, mod_ref, sinks_ref, v1024_ref, v1280_ref, w_in_ref, w_rg_ref,
         w_ao_ref, w_ro_ref, w_out_ref,
         y_ref, ks_ref, vs_ref, cs_ref, hs_ref,
         u_ref, kdup_ref, vdup_ref, bias_ref, xpad_ref, hcar_ref, attn_ref, rnn_ref) = refs
    else:
        (x_ref, mod_ref, ck_ref, cv_ref, c0_ref, h0_ref, sinks_ref, v1024_ref, v1280_ref,
         w_in_ref, w_rg_ref, w_ao_ref, w_ro_ref, w_out_ref,
         y_ref, ks_ref, vs_ref, cs_ref, hs_ref,
         u_ref, kdup_ref, vdup_ref, bias_ref, xpad_ref, hcar_ref, attn_ref, rnn_ref) = refs

    i = pl.program_id(0)
    rows = ns * ls
    lq = min(ls, 2 * CHUNK)
    n_win = ls // lq
    q_w = N_HEADS * HEAD_DIM
    n_blk = d_rnn // RNN_BLOCK
    seq_start = (i % tiles_per_seq) == 0 if carry else None

    @pl.when(i == 0)
    def _():
        t = lax.broadcasted_iota(jnp.int32, (lq, KEY_WIN), 0)
        j = lax.broadcasted_iota(jnp.int32, (lq, KEY_WIN), 1)
        dist = jnp.abs(t + WINDOW - j).astype(_F32)
        qc = t // CHUNK
        kc = j // CHUNK
        base = jnp.where(kc >= qc, jnp.where(kc <= qc + N_BACK, 0.0, NEG_INF), NEG_INF)
        for h in range(N_KV_HEADS):
            for g in range(GROUP):
                slope = 2.0 ** (-8.0 * (h * GROUP + g + 1) / N_HEADS)
                bias_ref[h, g * lq:(g + 1) * lq, :] = base - slope * dist

    if carry:
        @pl.when(seq_start)
        def _():
            kdup_ref[0, :, 0:WINDOW, :] = jnp.zeros((N_KV_HEADS, WINDOW, LANES), _BF16)
            vdup_ref[0, :, 0:WINDOW, :] = jnp.zeros((N_KV_HEADS, WINDOW, LANES), _BF16)
            xpad_ref[0, 0:SUBLANES, :] = jnp.zeros((SUBLANES, d_rnn), _F32)
            hcar_ref[...] = jnp.zeros((1, d_rnn), _F32)

    x = x_ref[...]
    xn = _rms(x) * v1024_ref[0:1, :]
    for s in range(ns):
        sl = slice(s * ls, (s + 1) * ls)
        u_ref[sl, :] = (xn[sl] * (1.0 + mod_ref[s, 1:2, :]) + mod_ref[s, 0:1, :]).astype(_BF16)
    u = u_ref[...]

    off = 0

    def proj(width):
        nonlocal off
        out = _dot(u, w_in_ref[:, off:off + width])
        off += width
        return out

    q = proj(q_w)
    k = proj(KV_W)
    v = proj(KV_W)
    xr = proj(d_rnn)
    yr = proj(d_rnn)
    ga = proj(d_model)
    gr = proj(d_model)

    for s in range(ns):
        sl = slice(s * ls, (s + 1) * ls)
        kd = _dup_heads(k[sl])
        vd = _dup_heads(v[sl])
        for h in range(N_KV_HEADS):
            kdup_ref[s, h, WINDOW:WINDOW + ls, :] = kd[h]
            vdup_ref[s, h, WINDOW:WINDOW + ls, :] = vd[h]
        if carry:
            ks_ref[0] = k[rows - WINDOW:rows]
            vs_ref[0] = v[rows - WINDOW:rows]
        else:
            hk = _dup_heads(ck_ref[s])
            hv = _dup_heads(cv_ref[s])
            pad = KEY_WIN - WINDOW - ls
            for h in range(N_KV_HEADS):
                kdup_ref[s, h, 0:WINDOW, :] = hk[h]
                vdup_ref[s, h, 0:WINDOW, :] = hv[h]
                kdup_ref[s, h, WINDOW + ls:KEY_WIN, :] = jnp.zeros((pad, LANES), _BF16)
                vdup_ref[s, h, WINDOW + ls:KEY_WIN, :] = jnp.zeros((pad, LANES), _BF16)
            ks_ref[s, 0:WINDOW - ls, :] = ck_ref[s, ls:WINDOW, :]
            ks_ref[s, WINDOW - ls:WINDOW, :] = k[sl]
            vs_ref[s, 0:WINDOW - ls, :] = cv_ref[s, ls:WINDOW, :]
            vs_ref[s, WINDOW - ls:WINDOW, :] = v[sl]

    lane = lax.broadcasted_iota(jnp.int32, (1, LANES), 1)
    scale = HEAD_DIM ** -0.5
    m_lo = jnp.where(lane < HEAD_DIM, scale, 0.0).astype(_F32)
    m_hi = scale - m_lo
    lo_sel = lax.broadcasted_iota(jnp.int32, (lq, LANES), 1) < HEAD_DIM
    if carry:
        kcol = lax.broadcasted_iota(jnp.int32, (1, KEY_WIN), 1)
        hist_bias = jnp.where(jnp.logical_and(seq_start, kcol < WINDOW), NEG_INF, 0.0).astype(_F32)
    for s in range(ns):
        for w in range(n_win):
            r0 = s * ls + w * lq
            for h in range(N_KV_HEADS):
                c0 = h * GROUP * HEAD_DIM
                qa = q[r0:r0 + lq, c0:c0 + LANES]
                qb = q[r0:r0 + lq, c0 + LANES:c0 + 2 * LANES]
                qs = jnp.concatenate([qa * m_lo, qa * m_hi, qb * m_lo, qb * m_hi], axis=0).astype(_BF16)
                kd = kdup_ref[s, h, w * lq:w * lq + KEY_WIN, :]
                vd = vdup_ref[s, h, w * lq:w * lq + KEY_WIN, :]
                sc = _dot_nt(qs, kd) + bias_ref[h]
                if carry and w == 0:
                    sc = sc + hist_bias
                ps, inv = [], []
                for g in range(GROUP):
                    sg = sc[g * lq:(g + 1) * lq]
                    sink = sinks_ref[h * GROUP + g]
                    mg = jnp.maximum(jnp.max(sg, axis=-1, keepdims=True), sink)
                    pg = jnp.exp(sg - mg)
                    inv.append(1.0 / (jnp.sum(pg, axis=-1, keepdims=True) + jnp.exp(sink - mg)))
                    ps.append(pg.astype(_BF16))
                o = _dot(jnp.concatenate(ps, axis=0), vd)
                og = [o[g * lq:(g + 1) * lq] * inv[g] for g in range(GROUP)]
                attn_ref[r0:r0 + lq, c0:c0 + LANES] = jnp.where(lo_sel, og[0], og[1]).astype(_BF16)
                attn_ref[r0:r0 + lq, c0 + LANES:c0 + 2 * LANES] = jnp.where(lo_sel, og[2], og[3]).astype(_BF16)
    if carry:
        kdup_ref[0, :, 0:WINDOW, :] = kdup_ref[0, :, ls:ls + WINDOW, :]
        vdup_ref[0, :, 0:WINDOW, :] = vdup_ref[0, :, ls:ls + WINDOW, :]

    xcs = []
    for s in range(ns):
        sl = slice(s * ls, (s + 1) * ls)
        if not carry:
            xpad_ref[s, SUBLANES - (CONV_W - 1):SUBLANES, :] = c0_ref[s]
        xpad_ref[s, SUBLANES:SUBLANES + ls, :] = xr[sl]
        acc = v1280_ref[CONV_W:CONV_W + 1, :] + v1280_ref[CONV_W - 1:CONV_W, :] * xr[sl]
        for jj in range(CONV_W - 1):
            st = SUBLANES - (CONV_W - 1) + jj
            acc = acc + v1280_ref[jj:jj + 1, :] * xpad_ref[s, st:st + ls, :]
        xcs.append(acc)
        tail = xpad_ref[s, SUBLANES + ls - (CONV_W - 1):SUBLANES + ls, :]
        cs_ref[s] = tail
        if carry:
            xpad_ref[s, SUBLANES - (CONV_W - 1):SUBLANES, :] = tail
    xc = xcs[0] if ns == 1 else jnp.concatenate(xcs, axis=0)
    xcb = xc.astype(_BF16)

    sub = lax.broadcasted_iota(jnp.int32, (ls // SUBLANES, SUBLANES, RNN_BLOCK), 1)
    for n in range(n_blk):
        cs_ = slice(n * RNN_BLOCK, (n + 1) * RNN_BLOCK)
        gates = _dot(xcb[:, cs_], w_rg_ref[n])
        r = jax.nn.sigmoid(gates[:, :RNN_BLOCK] + v1280_ref[5:6, cs_])
        ig = jax.nn.sigmoid(gates[:, RNN_BLOCK:] + v1280_ref[6:7, cs_])
        log_a = (-LRU_C * _softplus(-v1280_ref[7:8, cs_])) * r
        a = jnp.exp(log_a)
        b = jnp.sqrt(_neg_expm1(2.0 * log_a)) * (ig * xc[:, cs_])
        hs = []
        for s in range(ns):
            sl = slice(s * ls, (s + 1) * ls)
            a3 = a[sl].reshape(ls // SUBLANES, SUBLANES, RNN_BLOCK)
            b3 = b[sl].reshape(ls // SUBLANES, SUBLANES, RNN_BLOCK)
            d = 1
            while d < SUBLANES:
                keep = sub >= d
                b3 = b3 + a3 * jnp.where(keep, pltpu.roll(b3, d, axis=1), 0.0)
                a3 = a3 * jnp.where(keep, pltpu.roll(a3, d, axis=1), 1.0)
                d *= 2
            hprev = hcar_ref[0:1, cs_] if carry else h0_ref[s, 0:1, cs_]
            for jb in range(ls // SUBLANES):
                hj = b3[jb] + a3[jb] * hprev
                hs.append(hj)
                hprev = hj[SUBLANES - 1:SUBLANES, :]
            hs_ref[s, 0:1, cs_] = hprev
            if carry:
                hcar_ref[0:1, cs_] = hprev
        hfull = jnp.concatenate(hs, axis=0)
        rnn_ref[:, cs_] = (hfull * _gelu_tanh(yr[:, cs_])).astype(_BF16)

    ao = _dot(attn_ref[...], w_ao_ref[...])
    ro = _dot(rnn_ref[...], w_ro_ref[...])
    merged = (jax.nn.sigmoid(ga) * ao + jax.nn.sigmoid(gr) * ro).astype(_BF16)
    yn = _rms(_dot(merged, w_out_ref[...])) * v1024_ref[1:2, :]
    for s in range(ns):
        sl = slice(s * ls, (s + 1) * ls)
        y_ref[sl, :] = x[sl] + mod_ref[s, 2:3, :] * yn[sl]


def _whole(memory_space=pltpu.VMEM):
    return pl.BlockSpec(memory_space=memory_space)


def _mixer_call(x, mod, hist, sinks, v1024, v1280, w_in, w_rg, w_ao, w_ro, w_out, *, ns, ls):
    n_seq, seq_len, d_model = x.shape
    d_rnn = v1280.shape[1]
    carry = hist is None
    rows = ns * ls
    if carry:
        assert ns == 1 and seq_len % ls == 0 and ls % (2 * CHUNK) == 0
        tiles_per_seq = seq_len // ls
        n_steps = n_seq * tiles_per_seq
        seq_of = lambda i: (i // tiles_per_seq, 0, 0)
    else:
        assert seq_len == ls == CHUNK and n_seq % ns == 0
        tiles_per_seq = 1
        n_steps = n_seq // ns
        seq_of = lambda i: (i, 0, 0)
    lq = min(ls, 2 * CHUNK)
    xf = x.reshape(n_seq * seq_len, d_model)

    in_specs = [pl.BlockSpec((rows, d_model), lambda i: (i, 0)),
                pl.BlockSpec((ns, 6, d_model), seq_of)]
    args = [xf, mod]
    if not carry:
        ck, cv, c0, h0 = hist
        in_specs += [pl.BlockSpec((ns, WINDOW, KV_W), seq_of),
                     pl.BlockSpec((ns, WINDOW, KV_W), seq_of),
                     pl.BlockSpec((ns, CONV_W - 1, d_rnn), seq_of),
                     pl.BlockSpec((ns, 1, d_rnn), seq_of)]
        args += [ck, cv, c0, h0]
    in_specs += [_whole(pltpu.SMEM)] + [_whole()] * 7
    args += [sinks, v1024, v1280, w_in, w_rg, w_ao, w_ro, w_out]

    out_shape = (jax.ShapeDtypeStruct((n_seq * seq_len, d_model), _F32),
                 jax.ShapeDtypeStruct((n_seq, WINDOW, KV_W), _F32),
                 jax.ShapeDtypeStruct((n_seq, WINDOW, KV_W), _F32),
                 jax.ShapeDtypeStruct((n_seq, CONV_W - 1, d_rnn), _F32),
                 jax.ShapeDtypeStruct((n_seq, 1, d_rnn), _F32))
    out_specs = (pl.BlockSpec((rows, d_model), lambda i: (i, 0)),
                 pl.BlockSpec((ns, WINDOW, KV_W), seq_of),
                 pl.BlockSpec((ns, WINDOW, KV_W), seq_of),
                 pl.BlockSpec((ns, CONV_W - 1, d_rnn), seq_of),
                 pl.BlockSpec((ns, 1, d_rnn), seq_of))
    win_rows = WINDOW + max(ls, 2 * CHUNK)
    scratch = [pltpu.VMEM((rows, d_model), _BF16),
               pltpu.VMEM((ns, N_KV_HEADS, win_rows, LANES), _BF16),
               pltpu.VMEM((ns, N_KV_HEADS, win_rows, LANES), _BF16),
               pltpu.VMEM((N_KV_HEADS, GROUP * lq, KEY_WIN), _F32),
               pltpu.VMEM((ns, SUBLANES + ls, d_rnn), _F32),
               pltpu.VMEM((1, d_rnn), _F32),
               pltpu.VMEM((rows, N_HEADS * HEAD_DIM), _BF16),
               pltpu.VMEM((rows, d_rnn), _BF16)]
    kern = functools.partial(_mixer_kernel, carry, ns, ls, tiles_per_seq, d_model, d_rnn)
    return pl.pallas_call(
        kern,
        grid=(n_steps,),
        in_specs=in_specs,
        out_specs=out_specs,
        out_shape=out_shape,
        scratch_shapes=scratch,
        compiler_params=pltpu.CompilerParams(dimension_semantics=("arbitrary",),
                                             vmem_limit_bytes=VMEM_LIMIT_BYTES),
        name="mixer_stream" if carry else "mixer_step",
    )(*args)


def _ffn_kernel(ns, ls, x_ref, mod_ref, v1024_ref, wg_ref, wu_ref, wd_ref, y_ref, u_ref):
    x = x_ref[...]
    xn = _rms(x) * v1024_ref[0:1, :]
    for s in range(ns):
        sl = slice(s * ls, (s + 1) * ls)
        u_ref[sl, :] = (xn[sl] * (1.0 + mod_ref[s, 4:5, :]) + mod_ref[s, 3:4, :]).astype(_BF16)
    u = u_ref[...]
    g = _dot(u, wg_ref[...])
    up = _dot(u, wu_ref[...])
    hmid = (g * jax.nn.sigmoid(g) * up).astype(_BF16)
    yn = _rms(_dot(hmid, wd_ref[...])) * v1024_ref[1:2, :]
    for s in range(ns):
        sl = slice(s * ls, (s + 1) * ls)
        y_ref[sl, :] = x[sl] + mod_ref[s, 5:6, :] * yn[sl]


def _ffn_call(xf, mod, v1024, wg, wu, wd, *, ns, ls, seq_len):
    n_rows, d_model = xf.shape
    rows = ns * ls
    if ns == 1:
        tiles_per_seq = seq_len // ls
        seq_of = lambda i: (i // tiles_per_seq, 0, 0)
    else:
        assert ls == seq_len
        seq_of = lambda i: (i, 0, 0)
    return pl.pallas_call(
        functools.partial(_ffn_kernel, ns, ls),
        grid=(n_rows // rows,),
        in_specs=[pl.BlockSpec((rows, d_model), lambda i: (i, 0)),
                  pl.BlockSpec((ns, 6, d_model), seq_of),
                  _whole(), _whole(), _whole(), _whole()],
        out_specs=pl.BlockSpec((rows, d_model), lambda i: (i, 0)),
        out_shape=jax.ShapeDtypeStruct((n_rows, d_model), _F32),
        scratch_shapes=[pltpu.VMEM((rows, d_model), _BF16)],
        compiler_params=pltpu.CompilerParams(dimension_semantics=("arbitrary",),
                                             vmem_limit_bytes=VMEM_LIMIT_BYTES),
        name="ffn",
    )(xf, mod, v1024, wg, wu, wd)


MIX_TILE = 256
MIX_SEQS = 4
FFN_TILE = 512
FFN_SEQS = 8


def kernel(x_prompt, x_sample, c_prompt, c_sample, cache_k, cache_v, state_conv, state_h, w_ada, b_ada, g_pre_mix, g_post_mix, w_in, attn_sinks, w_conv, b_conv, w_rg_a, b_rg_a, w_rg_x, b_rg_x, rg_lambda, w_attn_o, w_rnn_o, w_out, g_pre_ffn, g_post_ffn, w_ffn_gate, w_ffn_up, w_ffn_down):
    depth = w_in.shape[0]
    assert depth == 1
    bp, sp, d_model = x_prompt.shape
    bs, ss, _ = x_sample.shape
    d_rnn = w_conv.shape[-1]
    l = 0

    n_c = bp + bs
    pad = (-n_c) % 16
    c_all = jnp.concatenate([c_prompt, c_sample, jnp.zeros((pad, d_model), _F32)], axis=0)
    mod = _ada_call(c_all, w_ada[l], b_ada[l])[:n_c].reshape(n_c, 6, d_model)
    mod_p, mod_s = mod[:bp], mod[bp:]

    bf = lambda w: w.astype(_BF16)
    v1024_mix = jnp.stack([g_pre_mix[l], g_post_mix[l]])
    v1024_ffn = jnp.stack([g_pre_ffn[l], g_post_ffn[l]])
    v1280 = jnp.concatenate([w_conv[l], b_conv[l][None], b_rg_a[l][None], b_rg_x[l][None],
                             rg_lambda[l][None]], axis=0)
    w_rg = bf(jnp.concatenate([w_rg_a[l], w_rg_x[l]], axis=-1))
    mix_w = (attn_sinks[l], v1024_mix, v1280, bf(w_in[l]), w_rg, bf(w_attn_o[l]), bf(w_rnn_o[l]),
             bf(w_out[l]))
    ffn_w = (v1024_ffn, bf(w_ffn_gate[l]), bf(w_ffn_up[l]), bf(w_ffn_down[l]))

    xp1, kp, vp, cp, hp = _mixer_call(x_prompt, mod_p, None, *mix_w, ns=1, ls=MIX_TILE)
    hist = (cache_k[l].reshape(bs, WINDOW, KV_W), cache_v[l].reshape(bs, WINDOW, KV_W),
            state_conv[l], state_h[l].reshape(bs, 1, d_rnn))
    xs1, ks, vs, cs, hs = _mixer_call(x_sample, mod_s, hist, *mix_w, ns=MIX_SEQS, ls=ss)

    yp = _ffn_call(xp1, mod_p, *ffn_w, ns=1, ls=FFN_TILE, seq_len=sp)
    ys = _ffn_call(xs1, mod_s, *ffn_w, ns=FFN_SEQS, ls=ss, seq_len=ss)

    kv_shape = lambda n: (1, n, WINDOW, N_KV_HEADS, HEAD_DIM)
    return (yp.reshape(bp, sp, d_model), ys.reshape(bs, ss, d_model),
            kp.reshape(kv_shape(bp)), vp.reshape(kv_shape(bp)), cp[None], hp.reshape(1, bp, d_rnn),
            ks.reshape(kv_shape(bs)), vs.reshape(kv_shape(bs)), cs[None], hs.reshape(1, bs, d_rnn))
```

```python
import functools
import math

import jax
import jax.numpy as jnp
from jax import lax
from jax.experimental import pallas as pl
from jax.experimental.pallas import tpu as pltpu

CHUNK = 64
N_HEADS = 16
N_KV_HEADS = 4
HEAD_DIM = 64
GROUP = N_HEADS // N_KV_HEADS
WINDOW = 128
N_BACK = WINDOW // CHUNK
KV_W = N_KV_HEADS * HEAD_DIM
RNN_BLOCK = 128
CONV_W = 4
LRU_C = 8.0
EPS = 1e-6
NEG_INF = -1e30

LANES = 128
SUBLANES = 8
MXU_COLS = 256
KEY_WIN = WINDOW + 2 * CHUNK
VMEM_LIMIT_BYTES = 56 * 1024 * 1024
LOG2E = math.log2(math.e)
ROW_B_CONV, ROW_B_A, ROW_B_X, ROW_LAMBDA = CONV_W, CONV_W + 1, CONV_W + 2, CONV_W + 3
MOD_SHIFT_MIX, MOD_SCALE_MIX, MOD_GATE_MIX, MOD_SHIFT_FFN, MOD_SCALE_FFN, MOD_GATE_FFN = range(6)

_F32 = jnp.float32
_BF16 = jnp.bfloat16


def _dot(a, b):
    return lax.dot_general(a, b, (((1,), (0,)), ((), ())), preferred_element_type=_F32)


def _dot_nt(a, b):
    return lax.dot_general(a, b, (((1,), (1,)), ((), ())), preferred_element_type=_F32)


def _rms(x):
    return x * lax.rsqrt(jnp.mean(x * x, axis=-1, keepdims=True) + EPS)


def _softplus(x):
    return jnp.maximum(x, 0.0) + jnp.log1p(jnp.exp(-jnp.abs(x)))


def _sigmoid(x):
    return 0.5 * jnp.tanh(0.5 * x) + 0.5


def _silu(x):
    h = 0.5 * x
    return h * jnp.tanh(h) + h


def _sqrt_nonneg(x):
    return jnp.where(x > 0.0, x * lax.rsqrt(x), 0.0)


def _one_minus_sq(log_a, a):
    y = 2.0 * log_a
    series = -y * (1.0 + y * (0.5 + y * (1.0 / 6.0)))
    return jnp.where(y > -0.01, series, 1.0 - a * a)


def _gelu_tanh(x):
    c = math.sqrt(2.0 / math.pi)
    hx = 0.5 * x
    return hx + hx * jnp.tanh(x * (c + (c * 0.044715) * (x * x)))


def _split_heads(kv, scale=None):
    n = kv.shape[0]
    lo = lax.broadcasted_iota(jnp.int32, (n, LANES), 1) < HEAD_DIM
    outs = []
    for blk in range(KV_W // LANES):
        b = kv[:, blk * LANES:(blk + 1) * LANES]
        if scale is not None:
            b = b * scale
        r = pltpu.roll(b, HEAD_DIM, axis=1)
        outs.append((jnp.where(lo, b, 0.0).astype(_BF16), jnp.where(lo, 0.0, r).astype(_BF16)))
        outs.append((jnp.where(lo, r, 0.0).astype(_BF16), jnp.where(lo, 0.0, b).astype(_BF16)))
    return outs


def _ada_kernel(c_ref, w_ref, b_ref, o_ref):
    c = c_ref[...]
    a = _silu(c).astype(_BF16)
    o_ref[...] = _dot(a, w_ref[...].astype(_BF16)) + b_ref[...]


def _ada_call(c, w_ada, b_ada):
    rows, d = c.shape
    n = w_ada.shape[1]
    bn = d
    return pl.pallas_call(
        _ada_kernel,
        grid=(n // bn,),
        in_specs=[
            pl.BlockSpec((rows, d), lambda j: (0, 0)),
            pl.BlockSpec((d, bn), lambda j: (0, j)),
            pl.BlockSpec((1, bn), lambda j: (0, j)),
        ],
        out_specs=pl.BlockSpec((rows, bn), lambda j: (0, j)),
        out_shape=jax.ShapeDtypeStruct((rows, n), _F32),
        compiler_params=pltpu.CompilerParams(dimension_semantics=("arbitrary",)),
        name="ada",
    )(c, w_ada, b_ada.reshape(1, n))


def _mixer_kernel(carry, ns, ls, tiles_per_seq, d_model, d_rnn, *refs):
    n_in = 4 if carry else 8
    xa_ref, xb_ref, moda_ref, modb_ref = refs[:4]
    if not carry:
        ck_ref, cv_ref, c0_ref, h0_ref = refs[4:8]
    (sinks_ref, v1024_ref, v1280_ref, w_in_ref, w_rg_ref, w_ao_ref, w_ro_ref, w_out_ref,
     y_ref, ks_ref, vs_ref, cs_ref, hs_ref,
     u_ref, q_ref, k_ref, v_ref, xr_ref, yr_ref, ga_ref, gr_ref, xc_ref, sga_ref, sgr_ref,
     kwin_ref, vwin_ref, kst_ref, vst_ref, cst_ref, bias_ref, xpad_ref, hcar_ref,
     attn_ref, rnn_ref) = refs[n_in:]

    i = pl.program_id(0)
    rows = ns * ls
    lq = min(ls, 2 * CHUNK)
    n_win = ls // lq
    q_w = N_HEADS * HEAD_DIM
    n_blk = d_rnn // RNN_BLOCK
    if carry:
        keep_mix = jnp.where(lax.rem(i + tiles_per_seq - 1, tiles_per_seq) == 0, 0.0, 1.0)
        keep_prep = jnp.where(lax.rem(i, tiles_per_seq) == 0, 0.0, 1.0)

    @pl.when(i == 0)
    def _():
        for ref in (q_ref, yr_ref, xc_ref, sga_ref, sgr_ref, kwin_ref, vwin_ref, kst_ref,
                    vst_ref, cst_ref, xpad_ref, hcar_ref):
            ref[...] = jnp.zeros(ref.shape, ref.dtype)
        t = lax.broadcasted_iota(jnp.int32, (lq, KEY_WIN), 0)
        j = lax.broadcasted_iota(jnp.int32, (lq, KEY_WIN), 1)
        dist = jnp.abs(t + WINDOW - j).astype(_F32)
        qc = t // CHUNK
        kc = j // CHUNK
        base = jnp.where(kc >= qc, jnp.where(kc <= qc + N_BACK, 0.0, NEG_INF), NEG_INF)
        for h in range(N_KV_HEADS):
            for g in range(GROUP):
                slope = 2.0 ** (-8.0 * (h * GROUP + g + 1) / N_HEADS)
                bias_ref[h, g % 2, (g // 2) * lq:(g // 2 + 1) * lq, :] = LOG2E * (base - slope * dist)

    ks_ref[...] = kst_ref[...]
    vs_ref[...] = vst_ref[...]
    cs_ref[...] = cst_ref[...]

    xn = _rms(xa_ref[...]) * v1024_ref[0:1, :]
    for s in range(ns):
        sl = slice(s * ls, (s + 1) * ls)
        u_ref[sl, :] = (xn[sl] * (1.0 + moda_ref[s, MOD_SCALE_MIX:MOD_SCALE_MIX + 1, :])
                        + moda_ref[s, MOD_SHIFT_MIX:MOD_SHIFT_MIX + 1, :]).astype(_BF16)

    proj = []
    off = 0
    for ref, width, free_after in (
            (q_ref, q_w, lambda c: 2 * (c // MXU_COLS) + 1),
            (k_ref, KV_W, lambda c: -1), (v_ref, KV_W, lambda c: -1),
            (xr_ref, d_rnn, lambda c: -1),
            (yr_ref, d_rnn, lambda c: (c + MXU_COLS - 1) // RNN_BLOCK),
            (ga_ref, d_model, lambda c: -1), (gr_ref, d_model, lambda c: -1)):
        for c in range(0, width, MXU_COLS):
            proj.append((free_after(c), ref, c, off + c))
        off += width
    proj.sort(key=lambda p: p[0])

    def emit_proj(slot, slots_left):
        ready = [p for p in proj if p[0] <= slot]
        quota = -(-len(proj) // slots_left) if slots_left else len(proj)
        for p in ready[:quota]:
            proj.remove(p)
            _, ref, c, wc = p
            ref[:, c:c + MXU_COLS] = _dot(u_ref[...], w_in_ref[:, wc:wc + MXU_COLS]).astype(ref.dtype)

    lo_sel = lax.broadcasted_iota(jnp.int32, (lq, LANES), 1) < HEAD_DIM
    if carry:
        kcol = lax.broadcasted_iota(jnp.int32, (1, KEY_WIN), 1)
        hist_bias = jnp.where(kcol < WINDOW, NEG_INF, 0.0).astype(_F32) * (1.0 - keep_mix)

    def attend(s, w, h):
        r0 = s * ls + w * lq
        kw = slice(w * lq, w * lq + KEY_WIN)
        c0 = h * GROUP * HEAD_DIM
        qab = jnp.concatenate([q_ref[r0:r0 + lq, c0:c0 + LANES],
                               q_ref[r0:r0 + lq, c0 + LANES:c0 + 2 * LANES]], axis=0)
        ps, inv = [[], []], [None] * GROUP
        for half in range(2):
            sc = _dot_nt(qab, kwin_ref[s, h, half, kw, :]) + bias_ref[h, half]
            if carry and w == 0:
                sc = sc + hist_bias
            for pair in range(2):
                g = 2 * pair + half
                sg = sc[pair * lq:(pair + 1) * lq]
                sink = LOG2E * sinks_ref[h * GROUP + g]
                mg = jnp.maximum(jnp.max(sg, axis=-1, keepdims=True), sink)
                pg = jnp.exp2(sg - mg)
                inv[g] = 1.0 / (jnp.sum(pg, axis=-1, keepdims=True) + jnp.exp2(sink - mg))
                ps[half].append(pg.astype(_BF16))
        o = (_dot(jnp.concatenate(ps[0], axis=0), vwin_ref[s, h, 0, kw, :])
             + _dot(jnp.concatenate(ps[1], axis=0), vwin_ref[s, h, 1, kw, :]))
        for pair in range(2):
            norm = jnp.where(lo_sel, inv[2 * pair], inv[2 * pair + 1])
            attn_ref[r0:r0 + lq, c0 + pair * LANES:c0 + (pair + 1) * LANES] = (
                o[pair * lq:(pair + 1) * lq] * norm).astype(_BF16)

    units = [(s, w, h) for h in range(N_KV_HEADS) for s in range(ns) for w in range(n_win)]
    units_per_slot = len(units) // (2 * N_KV_HEADS)

    sub = lax.broadcasted_iota(jnp.int32, (ls // SUBLANES, SUBLANES, RNN_BLOCK), 1)

    def recur(n):
        cs_ = slice(n * RNN_BLOCK, (n + 1) * RNN_BLOCK)
        xc = xc_ref[:, cs_]
        gates = _dot(xc.astype(_BF16), w_rg_ref[n])
        r = _sigmoid(gates[:, :RNN_BLOCK] + v1280_ref[ROW_B_A:ROW_B_A + 1, cs_])
        ig = _sigmoid(gates[:, RNN_BLOCK:] + v1280_ref[ROW_B_X:ROW_B_X + 1, cs_])
        log_a = (-LRU_C * _softplus(-v1280_ref[ROW_LAMBDA:ROW_LAMBDA + 1, cs_])) * r
        a = jnp.exp(log_a)
        b = _sqrt_nonneg(_one_minus_sq(log_a, a)) * (ig * xc)
        hs = []
        for s in range(ns):
            sl = slice(s * ls, (s + 1) * ls)
            a3 = a[sl].reshape(ls // SUBLANES, SUBLANES, RNN_BLOCK)
            b3 = b[sl].reshape(ls // SUBLANES, SUBLANES, RNN_BLOCK)
            d = 1
            while d < SUBLANES:
                keep = sub >= d
                b3 = b3 + a3 * jnp.where(keep, pltpu.roll(b3, d, axis=1), 0.0)
                a3 = a3 * jnp.where(keep, pltpu.roll(a3, d, axis=1), 1.0)
                d *= 2
            hprev = hcar_ref[0:1, cs_] * keep_mix if carry else h0_ref[s, 0:1, cs_]
            for jb in range(ls // SUBLANES):
                hj = b3[jb] + a3[jb] * hprev
                hs.append(hj)
                hprev = hj[SUBLANES - 1:SUBLANES, :]
            hs_ref[s, 0:1, cs_] = hprev
            if carry:
                hcar_ref[0:1, cs_] = hprev
        hfull = jnp.concatenate(hs, axis=0)
        rnn_ref[:, cs_] = (hfull * _gelu_tanh(yr_ref[:, cs_])).astype(_BF16)

    for n in range(n_blk):
        recur(n)
        for _ in range(units_per_slot):
            if units:
                attend(*units.pop(0))
        emit_proj(n, n_blk - n)
    assert not units

    ao = _dot(attn_ref[...], w_ao_ref[...])
    ro = _dot(rnn_ref[...], w_ro_ref[...])

    qk_scale = LOG2E * HEAD_DIM ** -0.5
    if carry:
        kwin_ref[0, :, :, 0:WINDOW, :] = kwin_ref[0, :, :, ls:ls + WINDOW, :]
        vwin_ref[0, :, :, 0:WINDOW, :] = vwin_ref[0, :, :, ls:ls + WINDOW, :]
        tail_rows = slice(SUBLANES - (CONV_W - 1), SUBLANES)
        xpad_ref[0, tail_rows, :] = xpad_ref[0, tail_rows, :] * keep_prep
    for s in range(ns):
        sl = slice(s * ls, (s + 1) * ls)
        kd = _split_heads(k_ref[sl, :], qk_scale)
        vd = _split_heads(v_ref[sl, :])
        for h in range(N_KV_HEADS):
            for half in range(2):
                kwin_ref[s, h, half, WINDOW:WINDOW + ls, :] = kd[h][half]
                vwin_ref[s, h, half, WINDOW:WINDOW + ls, :] = vd[h][half]
        if carry:
            kst_ref[0] = k_ref[rows - WINDOW:rows, :]
            vst_ref[0] = v_ref[rows - WINDOW:rows, :]
        else:
            hk = _split_heads(ck_ref[s], qk_scale)
            hv = _split_heads(cv_ref[s])
            pad = KEY_WIN - WINDOW - ls
            for h in range(N_KV_HEADS):
                for half in range(2):
                    kwin_ref[s, h, half, 0:WINDOW, :] = hk[h][half]
                    vwin_ref[s, h, half, 0:WINDOW, :] = hv[h][half]
                    kwin_ref[s, h, half, WINDOW + ls:KEY_WIN, :] = jnp.zeros((pad, LANES), _BF16)
                    vwin_ref[s, h, half, WINDOW + ls:KEY_WIN, :] = jnp.zeros((pad, LANES), _BF16)
            kst_ref[s, 0:WINDOW - ls, :] = ck_ref[s, ls:WINDOW, :]
            kst_ref[s, WINDOW - ls:WINDOW, :] = k_ref[sl, :]
            vst_ref[s, 0:WINDOW - ls, :] = cv_ref[s, ls:WINDOW, :]
            vst_ref[s, WINDOW - ls:WINDOW, :] = v_ref[sl, :]

        xr = xr_ref[sl, :]
        if not carry:
            xpad_ref[s, SUBLANES - (CONV_W - 1):SUBLANES, :] = c0_ref[s]
        xpad_ref[s, SUBLANES:SUBLANES + ls, :] = xr
        acc = v1280_ref[ROW_B_CONV:ROW_B_CONV + 1, :] + v1280_ref[CONV_W - 1:CONV_W, :] * xr
        for jj in range(CONV_W - 1):
            st = SUBLANES - (CONV_W - 1) + jj
            acc = acc + v1280_ref[jj:jj + 1, :] * xpad_ref[s, st:st + ls, :]
        xc_ref[sl, :] = acc
        tail = xpad_ref[s, SUBLANES + ls - (CONV_W - 1):SUBLANES + ls, :]
        cst_ref[s] = tail
        if carry:
            xpad_ref[s, SUBLANES - (CONV_W - 1):SUBLANES, :] = tail

    merged = (sga_ref[...] * ao + sgr_ref[...] * ro).astype(_BF16)
    mo = _dot(merged, w_out_ref[...])
    sga_ref[...] = _sigmoid(ga_ref[...])
    sgr_ref[...] = _sigmoid(gr_ref[...])
    yn = _rms(mo) * v1024_ref[1:2, :]
    for s in range(ns):
        sl = slice(s * ls, (s + 1) * ls)
        y_ref[sl, :] = xb_ref[sl, :] + modb_ref[s, MOD_GATE_MIX:MOD_GATE_MIX + 1, :] * yn[sl]
    emit_proj(n_blk, 0)
    assert not proj


def _whole(memory_space=pltpu.VMEM):
    return pl.BlockSpec(memory_space=memory_space)


def _mixer_call(x, mod, hist, sinks, v1024, v1280, w_in, w_rg, w_ao, w_ro, w_out, *, ns, ls):
    n_seq, seq_len, d_model = x.shape
    d_rnn = v1280.shape[1]
    carry = hist is None
    rows = ns * ls
    if carry:
        assert ns == 1 and seq_len % ls == 0 and ls % (2 * CHUNK) == 0
        tiles_per_seq = seq_len // ls
    else:
        assert seq_len == ls == CHUNK and n_seq % ns == 0
        tiles_per_seq = 1
    n_tiles = n_seq * seq_len // rows
    lq = min(ls, 2 * CHUNK)
    xf = x.reshape(n_seq * seq_len, d_model)

    tile_a = lambda i: jnp.minimum(i, n_tiles - 1)
    tile_b = lambda i: jnp.maximum(i - 1, 0)
    seq_a = lambda i: (tile_a(i) // tiles_per_seq, 0, 0)
    seq_b = lambda i: (tile_b(i) // tiles_per_seq, 0, 0)

    in_specs = [pl.BlockSpec((rows, d_model), lambda i: (tile_a(i), 0)),
                pl.BlockSpec((rows, d_model), lambda i: (tile_b(i), 0)),
                pl.BlockSpec((ns, 6, d_model), seq_a),
                pl.BlockSpec((ns, 6, d_model), seq_b)]
    args = [xf, xf, mod, mod]
    if not carry:
        ck, cv, c0, h0 = hist
        in_specs += [pl.BlockSpec((ns, WINDOW, KV_W), seq_a),
                     pl.BlockSpec((ns, WINDOW, KV_W), seq_a),
                     pl.BlockSpec((ns, CONV_W - 1, d_rnn), seq_a),
                     pl.BlockSpec((ns, 1, d_rnn), seq_b)]
        args += [ck, cv, c0, h0]
    in_specs += [_whole(pltpu.SMEM)] + [_whole()] * 7
    args += [sinks, v1024, v1280, w_in, w_rg, w_ao, w_ro, w_out]

    out_shape = (jax.ShapeDtypeStruct((n_seq * seq_len, d_model), _F32),
                 jax.ShapeDtypeStruct((n_seq, WINDOW, KV_W), _F32),
                 jax.ShapeDtypeStruct((n_seq, WINDOW, KV_W), _F32),
                 jax.ShapeDtypeStruct((n_seq, CONV_W - 1, d_rnn), _F32),
                 jax.ShapeDtypeStruct((n_seq, 1, d_rnn), _F32))
    out_specs = (pl.BlockSpec((rows, d_model), lambda i: (tile_b(i), 0)),
                 pl.BlockSpec((ns, WINDOW, KV_W), seq_b),
                 pl.BlockSpec((ns, WINDOW, KV_W), seq_b),
                 pl.BlockSpec((ns, CONV_W - 1, d_rnn), seq_b),
                 pl.BlockSpec((ns, 1, d_rnn), seq_b))
    win_rows = WINDOW + max(ls, 2 * CHUNK)
    scratch = [pltpu.VMEM((rows, d_model), _BF16),
               pltpu.VMEM((rows, N_HEADS * HEAD_DIM), _BF16),
               pltpu.VMEM((rows, KV_W), _F32),
               pltpu.VMEM((rows, KV_W), _F32),
               pltpu.VMEM((rows, d_rnn), _F32),
               pltpu.VMEM((rows, d_rnn), _F32),
               pltpu.VMEM((rows, d_model), _F32),
               pltpu.VMEM((rows, d_model), _F32),
               pltpu.VMEM((rows, d_rnn), _F32),
               pltpu.VMEM((rows, d_model), _F32),
               pltpu.VMEM((rows, d_model), _F32),
               pltpu.VMEM((ns, N_KV_HEADS, 2, win_rows, LANES), _BF16),
               pltpu.VMEM((ns, N_KV_HEADS, 2, win_rows, LANES), _BF16),
               pltpu.VMEM((ns, WINDOW, KV_W), _F32),
               pltpu.VMEM((ns, WINDOW, KV_W), _F32),
               pltpu.VMEM((ns, CONV_W - 1, d_rnn), _F32),
               pltpu.VMEM((N_KV_HEADS, 2, 2 * lq, KEY_WIN), _F32),
               pltpu.VMEM((ns, SUBLANES + ls, d_rnn), _F32),
               pltpu.VMEM((1, d_rnn), _F32),
               pltpu.VMEM((rows, N_HEADS * HEAD_DIM), _BF16),
               pltpu.VMEM((rows, d_rnn), _BF16)]
    kern = functools.partial(_mixer_kernel, carry, ns, ls, tiles_per_seq, d_model, d_rnn)
    return pl.pallas_call(
        kern,
        grid=(n_tiles + 1,),
        in_specs=in_specs,
        out_specs=out_specs,
        out_shape=out_shape,
        scratch_shapes=scratch,
        compiler_params=pltpu.CompilerParams(dimension_semantics=("arbitrary",),
                                             vmem_limit_bytes=VMEM_LIMIT_BYTES),
        name="mixer_stream" if carry else "mixer_step",
    )(*args)


def _ffn_kernel(ns, ls, x_ref, mod_ref, v1024_ref, wg_ref, wu_ref, wd_ref, y_ref, u_ref):
    x = x_ref[...]
    xn = _rms(x) * v1024_ref[0:1, :]
    for s in range(ns):
        sl = slice(s * ls, (s + 1) * ls)
        u_ref[sl, :] = (xn[sl] * (1.0 + mod_ref[s, MOD_SCALE_FFN:MOD_SCALE_FFN + 1, :])
                        + mod_ref[s, MOD_SHIFT_FFN:MOD_SHIFT_FFN + 1, :]).astype(_BF16)
    u = u_ref[...]
    g = _dot(u, wg_ref[...])
    up = _dot(u, wu_ref[...])
    hmid = (_silu(g) * up).astype(_BF16)
    yn = _rms(_dot(hmid, wd_ref[...])) * v1024_ref[1:2, :]
    for s in range(ns):
        sl = slice(s * ls, (s + 1) * ls)
        y_ref[sl, :] = x[sl] + mod_ref[s, MOD_GATE_FFN:MOD_GATE_FFN + 1, :] * yn[sl]


def _ffn_call(xf, mod, v1024, wg, wu, wd, *, ns, ls, seq_len):
    n_rows, d_model = xf.shape
    rows = ns * ls
    if ns == 1:
        tiles_per_seq = seq_len // ls
        seq_of = lambda i: (i // tiles_per_seq, 0, 0)
    else:
        assert ls == seq_len
        seq_of = lambda i: (i, 0, 0)
    return pl.pallas_call(
        functools.partial(_ffn_kernel, ns, ls),
        grid=(n_rows // rows,),
        in_specs=[pl.BlockSpec((rows, d_model), lambda i: (i, 0)),
                  pl.BlockSpec((ns, 6, d_model), seq_of),
                  _whole(), _whole(), _whole(), _whole()],
        out_specs=pl.BlockSpec((rows, d_model), lambda i: (i, 0)),
        out_shape=jax.ShapeDtypeStruct((n_rows, d_model), _F32),
        scratch_shapes=[pltpu.VMEM((rows, d_model), _BF16)],
        compiler_params=pltpu.CompilerParams(dimension_semantics=("arbitrary",),
                                             vmem_limit_bytes=VMEM_LIMIT_BYTES),
        name="ffn",
    )(xf, mod, v1024, wg, wu, wd)


MIX_TILE = 256
MIX_SEQS = 4
FFN_TILE = 512
FFN_SEQS = 8


def kernel(x_prompt, x_sample, c_prompt, c_sample, cache_k, cache_v, state_conv, state_h, w_ada, b_ada, g_pre_mix, g_post_mix, w_in, attn_sinks, w_conv, b_conv, w_rg_a, b_rg_a, w_rg_x, b_rg_x, rg_lambda, w_attn_o, w_rnn_o, w_out, g_pre_ffn, g_post_ffn, w_ffn_gate, w_ffn_up, w_ffn_down):
    depth = w_in.shape[0]
    assert depth == 1
    bp, sp, d_model = x_prompt.shape
    bs, ss, _ = x_sample.shape
    d_rnn = w_conv.shape[-1]
    l = 0

    n_c = bp + bs
    pad = (-n_c) % 16
    c_all = jnp.concatenate([c_prompt, c_sample, jnp.zeros((pad, d_model), _F32)], axis=0)
    mod = _ada_call(c_all, w_ada[l], b_ada[l])[:n_c].reshape(n_c, 6, d_model)
    mod_p, mod_s = mod[:bp], mod[bp:]

    bf = lambda w: w.astype(_BF16)
    v1024_mix = jnp.stack([g_pre_mix[l], g_post_mix[l]])
    v1024_ffn = jnp.stack([g_pre_ffn[l], g_post_ffn[l]])
    v1280 = jnp.concatenate([w_conv[l], b_conv[l][None], b_rg_a[l][None], b_rg_x[l][None],
                             rg_lambda[l][None]], axis=0)
    w_rg = bf(jnp.concatenate([w_rg_a[l], w_rg_x[l]], axis=-1))
    mix_w = (attn_sinks[l], v1024_mix, v1280, bf(w_in[l]), w_rg, bf(w_attn_o[l]), bf(w_rnn_o[l]),
             bf(w_out[l]))
    ffn_w = (v1024_ffn, bf(w_ffn_gate[l]), bf(w_ffn_up[l]), bf(w_ffn_down[l]))

    xp1, kp, vp, cp, hp = _mixer_call(x_prompt, mod_p, None, *mix_w, ns=1, ls=MIX_TILE)
    hist = (cache_k[l].reshape(bs, WINDOW, KV_W), cache_v[l].reshape(bs, WINDOW, KV_W),
            state_conv[l], state_h[l].reshape(bs, 1, d_rnn))
    xs1, ks, vs, cs, hs = _mixer_call(x_sample, mod_s, hist, *mix_w, ns=MIX_SEQS, ls=ss)

    yp = _ffn_call(xp1, mod_p, *ffn_w, ns=1, ls=FFN_TILE, seq_len=sp)
    ys = _ffn_call(xs1, mod_s, *ffn_w, ns=FFN_SEQS, ls=ss, seq_len=ss)

    kv_shape = lambda n: (1, n, WINDOW, N_KV_HEADS, HEAD_DIM)
    return (yp.reshape(bp, sp, d_model), ys.reshape(bs, ss, d_model),
            kp.reshape(kv_shape(bp)), vp.reshape(kv_shape(bp)), cp[None], hp.reshape(1, bp, d_rnn),
            ks.reshape(kv_shape(bs)), vs.reshape(kv_shape(bs)), cs[None], hs.reshape(1, bs, d_rnn))
```

```python
import functools
import math

import jax
import jax.numpy as jnp
from jax import lax
from jax.experimental import pallas as pl
from jax.experimental.pallas import tpu as pltpu

CHUNK = 64
N_HEADS = 16
N_KV_HEADS = 4
HEAD_DIM = 64
GROUP = N_HEADS // N_KV_HEADS
WINDOW = 128
N_BACK = WINDOW // CHUNK
KV_W = N_KV_HEADS * HEAD_DIM
RNN_BLOCK = 128
CONV_W = 4
LRU_C = 8.0
EPS = 1e-6
NEG_INF = -1e30

LANES = 128
SUBLANES = 8
MXU_COLS = 256
KEY_WIN = WINDOW + 2 * CHUNK
VMEM_LIMIT_BYTES = 56 * 1024 * 1024
LOG2E = math.log2(math.e)
ROW_B_CONV, ROW_B_A, ROW_B_X, ROW_LAMBDA = CONV_W, CONV_W + 1, CONV_W + 2, CONV_W + 3
MOD_SHIFT_MIX, MOD_SCALE_MIX, MOD_GATE_MIX, MOD_SHIFT_FFN, MOD_SCALE_FFN, MOD_GATE_FFN = range(6)

_F32 = jnp.float32
_BF16 = jnp.bfloat16


def _dot(a, b):
    return lax.dot_general(a, b, (((1,), (0,)), ((), ())), preferred_element_type=_F32)


def _dot_nt(a, b):
    return lax.dot_general(a, b, (((1,), (1,)), ((), ())), preferred_element_type=_F32)


def _rms(x):
    return x * lax.rsqrt(jnp.mean(x * x, axis=-1, keepdims=True) + EPS)


def _softplus(x):
    return jnp.maximum(x, 0.0) + jnp.log1p(jnp.exp(-jnp.abs(x)))


def _sigmoid(x):
    return 0.5 * jnp.tanh(0.5 * x) + 0.5


def _silu(x):
    h = 0.5 * x
    return h * jnp.tanh(h) + h


def _sqrt_nonneg(x):
    return jnp.where(x > 0.0, x * lax.rsqrt(x), 0.0)


def _neg_expm1_twice(x):
    t = jnp.tanh(x)
    return (-2.0 * t) / (1.0 - t)


def _slabs(w):
    k, n = w.shape
    return w.astype(_BF16).reshape(k, n // MXU_COLS, MXU_COLS).transpose(1, 0, 2)


def _gelu_tanh(x):
    c = math.sqrt(2.0 / math.pi)
    hx = 0.5 * x
    return hx + hx * jnp.tanh(x * (c + (c * 0.044715) * (x * x)))


def _split_heads(kv, scale=None):
    n = kv.shape[0]
    lo = lax.broadcasted_iota(jnp.int32, (n, LANES), 1) < HEAD_DIM
    outs = []
    for blk in range(KV_W // LANES):
        b = kv[:, blk * LANES:(blk + 1) * LANES]
        if scale is not None:
            b = b * scale
        r = pltpu.roll(b, HEAD_DIM, axis=1)
        outs.append((jnp.where(lo, b, 0.0).astype(_BF16), jnp.where(lo, 0.0, r).astype(_BF16)))
        outs.append((jnp.where(lo, r, 0.0).astype(_BF16), jnp.where(lo, 0.0, b).astype(_BF16)))
    return outs


def _ada_kernel(c_ref, w_ref, b_ref, o_ref):
    c = c_ref[...]
    a = _silu(c).astype(_BF16)
    o_ref[...] = _dot(a, w_ref[...].astype(_BF16)) + b_ref[...]


def _ada_call(c, w_ada, b_ada):
    rows, d = c.shape
    n = w_ada.shape[1]
    bn = d
    return pl.pallas_call(
        _ada_kernel,
        grid=(n // bn,),
        in_specs=[
            pl.BlockSpec((rows, d), lambda j: (0, 0)),
            pl.BlockSpec((d, bn), lambda j: (0, j)),
            pl.BlockSpec((1, bn), lambda j: (0, j)),
        ],
        out_specs=pl.BlockSpec((rows, bn), lambda j: (0, j)),
        out_shape=jax.ShapeDtypeStruct((rows, n), _F32),
        compiler_params=pltpu.CompilerParams(dimension_semantics=("arbitrary",)),
        name="ada",
    )(c, w_ada, b_ada.reshape(1, n))


def _mixer_kernel(carry, ns, ls, tiles_per_seq, d_model, d_rnn, *refs):
    n_in = 4 if carry else 8
    xa_ref, xb_ref, moda_ref, modb_ref = refs[:4]
    if not carry:
        ck_ref, cv_ref, c0_ref, h0_ref = refs[4:8]
    (sinks_ref, v1024_ref, v1280_ref, w_in_ref, w_rg_ref, w_ao_ref, w_ro_ref, w_out_ref,
     y_ref, ks_ref, vs_ref, cs_ref, hs_ref,
     u_ref, q_ref, k_ref, v_ref, xr_ref, yr_ref, ga_ref, gr_ref, xc_ref, sga_ref, sgr_ref,
     kwin_ref, vwin_ref, kst_ref, vst_ref, cst_ref, bias_ref, xpad_ref, hcar_ref,
     attn_ref, rnn_ref) = refs[n_in:]

    i = pl.program_id(0)
    rows = ns * ls
    lq = min(ls, 2 * CHUNK)
    n_win = ls // lq
    q_w = N_HEADS * HEAD_DIM
    n_blk = d_rnn // RNN_BLOCK
    if carry:
        keep_mix = jnp.where(lax.rem(i + tiles_per_seq - 1, tiles_per_seq) == 0, 0.0, 1.0)
        keep_prep = jnp.where(lax.rem(i, tiles_per_seq) == 0, 0.0, 1.0)

    @pl.when(i == 0)
    def _():
        for ref in (q_ref, yr_ref, xc_ref, sga_ref, sgr_ref, kwin_ref, vwin_ref, kst_ref,
                    vst_ref, cst_ref, xpad_ref, hcar_ref):
            ref[...] = jnp.zeros(ref.shape, ref.dtype)
        t = lax.broadcasted_iota(jnp.int32, (lq, KEY_WIN), 0)
        j = lax.broadcasted_iota(jnp.int32, (lq, KEY_WIN), 1)
        dist = jnp.abs(t + WINDOW - j).astype(_F32)
        qc = t // CHUNK
        kc = j // CHUNK
        base = jnp.where(kc >= qc, jnp.where(kc <= qc + N_BACK, 0.0, NEG_INF), NEG_INF)
        for h in range(N_KV_HEADS):
            for g in range(GROUP):
                slope = 2.0 ** (-8.0 * (h * GROUP + g + 1) / N_HEADS)
                bias_ref[h, g % 2, (g // 2) * lq:(g // 2 + 1) * lq, :] = LOG2E * (base - slope * dist)

    ks_ref[...] = kst_ref[...]
    vs_ref[...] = vst_ref[...]
    cs_ref[...] = cst_ref[...]

    xn = _rms(xa_ref[...]) * v1024_ref[0:1, :]
    for s in range(ns):
        sl = slice(s * ls, (s + 1) * ls)
        u_ref[sl, :] = (xn[sl] * (1.0 + moda_ref[s, MOD_SCALE_MIX:MOD_SCALE_MIX + 1, :])
                        + moda_ref[s, MOD_SHIFT_MIX:MOD_SHIFT_MIX + 1, :]).astype(_BF16)

    proj = []
    off = 0
    for ref, width, free_after in (
            (q_ref, q_w, lambda c: 2 * (c // MXU_COLS) + 1),
            (k_ref, KV_W, lambda c: -1), (v_ref, KV_W, lambda c: -1),
            (xr_ref, d_rnn, lambda c: -1),
            (yr_ref, d_rnn, lambda c: (c + MXU_COLS - 1) // RNN_BLOCK),
            (ga_ref, d_model, lambda c: -1), (gr_ref, d_model, lambda c: -1)):
        for c in range(0, width, MXU_COLS):
            proj.append((free_after(c), ref, c, off + c))
        off += width
    proj.sort(key=lambda p: p[0])

    def emit_proj(slot, slots_left):
        ready = [p for p in proj if p[0] <= slot]
        quota = -(-len(proj) // slots_left) if slots_left else len(proj)
        for p in ready[:quota]:
            proj.remove(p)
            _, ref, c, wc = p
            ref[:, c:c + MXU_COLS] = _dot(u_ref[...], w_in_ref[wc // MXU_COLS]).astype(ref.dtype)

    lo_sel = lax.broadcasted_iota(jnp.int32, (lq, LANES), 1) < HEAD_DIM
    if carry:
        kcol = lax.broadcasted_iota(jnp.int32, (1, KEY_WIN), 1)
        hist_bias = jnp.where(kcol < WINDOW, NEG_INF, 0.0).astype(_F32) * (1.0 - keep_mix)

    def attend(s, w, h):
        r0 = s * ls + w * lq
        kw = slice(w * lq, w * lq + KEY_WIN)
        c0 = h * GROUP * HEAD_DIM
        qab = jnp.concatenate([q_ref[r0:r0 + lq, c0:c0 + LANES],
                               q_ref[r0:r0 + lq, c0 + LANES:c0 + 2 * LANES]], axis=0)
        ps, inv = [[], []], [None] * GROUP
        for half in range(2):
            sc = _dot_nt(qab, kwin_ref[s, h, half, kw, :]) + bias_ref[h, half]
            if carry and w == 0:
                sc = sc + hist_bias
            for pair in range(2):
                g = 2 * pair + half
                sg = sc[pair * lq:(pair + 1) * lq]
                sink = LOG2E * sinks_ref[h * GROUP + g]
                mg = jnp.maximum(jnp.max(sg, axis=-1, keepdims=True), sink)
                pg = jnp.exp2(sg - mg)
                inv[g] = 1.0 / (jnp.sum(pg, axis=-1, keepdims=True) + jnp.exp2(sink - mg))
                ps[half].append(pg.astype(_BF16))
        o = (_dot(jnp.concatenate(ps[0], axis=0), vwin_ref[s, h, 0, kw, :])
             + _dot(jnp.concatenate(ps[1], axis=0), vwin_ref[s, h, 1, kw, :]))
        for pair in range(2):
            norm = jnp.where(lo_sel, inv[2 * pair], inv[2 * pair + 1])
            attn_ref[r0:r0 + lq, c0 + pair * LANES:c0 + (pair + 1) * LANES] = (
                o[pair * lq:(pair + 1) * lq] * norm).astype(_BF16)

    units = [(s, w, h) for h in range(N_KV_HEADS) for s in range(ns) for w in range(n_win)]
    units_per_slot = len(units) // (2 * N_KV_HEADS)

    sub = lax.broadcasted_iota(jnp.int32, (ls // SUBLANES, SUBLANES, RNN_BLOCK), 1)

    def recur(n):
        cs_ = slice(n * RNN_BLOCK, (n + 1) * RNN_BLOCK)
        xc = xc_ref[:, cs_]
        gates = _dot(xc.astype(_BF16), w_rg_ref[n])
        half_c = (-0.5 * LRU_C) * _softplus(-v1280_ref[ROW_LAMBDA:ROW_LAMBDA + 1, cs_])
        half_ba = 0.5 * v1280_ref[ROW_B_A:ROW_B_A + 1, cs_]
        log_a = half_c * jnp.tanh(0.5 * gates[:, :RNN_BLOCK] + half_ba) + half_c
        ig = _sigmoid(gates[:, RNN_BLOCK:] + v1280_ref[ROW_B_X:ROW_B_X + 1, cs_])
        a = jnp.exp(log_a)
        b = _sqrt_nonneg(_neg_expm1_twice(log_a)) * (ig * xc)
        hs = []
        for s in range(ns):
            sl = slice(s * ls, (s + 1) * ls)
            a3 = a[sl].reshape(ls // SUBLANES, SUBLANES, RNN_BLOCK)
            b3 = b[sl].reshape(ls // SUBLANES, SUBLANES, RNN_BLOCK)
            d = 1
            while d < SUBLANES:
                keep = sub >= d
                b3 = b3 + a3 * jnp.where(keep, pltpu.roll(b3, d, axis=1), 0.0)
                a3 = a3 * jnp.where(keep, pltpu.roll(a3, d, axis=1), 1.0)
                d *= 2
            hprev = hcar_ref[0:1, cs_] * keep_mix if carry else h0_ref[s, 0:1, cs_]
            for jb in range(ls // SUBLANES):
                hj = b3[jb] + a3[jb] * hprev
                hs.append(hj)
                hprev = hj[SUBLANES - 1:SUBLANES, :]
            hs_ref[s, 0:1, cs_] = hprev
            if carry:
                hcar_ref[0:1, cs_] = hprev
        hfull = jnp.concatenate(hs, axis=0)
        rnn_ref[:, cs_] = (hfull * _gelu_tanh(yr_ref[:, cs_])).astype(_BF16)

    for n in range(n_blk):
        recur(n)
        for _ in range(units_per_slot):
            if units:
                attend(*units.pop(0))
        emit_proj(n, n_blk - n)
    assert not units

    n_slab = d_model // MXU_COLS
    ao = [_dot(attn_ref[...], w_ao_ref[j]) for j in range(n_slab)]
    ro = [_dot(rnn_ref[...], w_ro_ref[j]) for j in range(n_slab)]

    qk_scale = LOG2E * HEAD_DIM ** -0.5
    if carry:
        kwin_ref[0, :, :, 0:WINDOW, :] = kwin_ref[0, :, :, ls:ls + WINDOW, :]
        vwin_ref[0, :, :, 0:WINDOW, :] = vwin_ref[0, :, :, ls:ls + WINDOW, :]
        tail_rows = slice(SUBLANES - (CONV_W - 1), SUBLANES)
        xpad_ref[0, tail_rows, :] = xpad_ref[0, tail_rows, :] * keep_prep
    for s in range(ns):
        sl = slice(s * ls, (s + 1) * ls)
        kd = _split_heads(k_ref[sl, :], qk_scale)
        vd = _split_heads(v_ref[sl, :])
        for h in range(N_KV_HEADS):
            for half in range(2):
                kwin_ref[s, h, half, WINDOW:WINDOW + ls, :] = kd[h][half]
                vwin_ref[s, h, half, WINDOW:WINDOW + ls, :] = vd[h][half]
        if carry:
            kst_ref[0] = k_ref[rows - WINDOW:rows, :]
            vst_ref[0] = v_ref[rows - WINDOW:rows, :]
        else:
            hk = _split_heads(ck_ref[s], qk_scale)
            hv = _split_heads(cv_ref[s])
            pad = KEY_WIN - WINDOW - ls
            for h in range(N_KV_HEADS):
                for half in range(2):
                    kwin_ref[s, h, half, 0:WINDOW, :] = hk[h][half]
                    vwin_ref[s, h, half, 0:WINDOW, :] = hv[h][half]
                    kwin_ref[s, h, half, WINDOW + ls:KEY_WIN, :] = jnp.zeros((pad, LANES), _BF16)
                    vwin_ref[s, h, half, WINDOW + ls:KEY_WIN, :] = jnp.zeros((pad, LANES), _BF16)
            kst_ref[s, 0:WINDOW - ls, :] = ck_ref[s, ls:WINDOW, :]
            kst_ref[s, WINDOW - ls:WINDOW, :] = k_ref[sl, :]
            vst_ref[s, 0:WINDOW - ls, :] = cv_ref[s, ls:WINDOW, :]
            vst_ref[s, WINDOW - ls:WINDOW, :] = v_ref[sl, :]

        xr = xr_ref[sl, :]
        if not carry:
            xpad_ref[s, SUBLANES - (CONV_W - 1):SUBLANES, :] = c0_ref[s]
        xpad_ref[s, SUBLANES:SUBLANES + ls, :] = xr
        acc = v1280_ref[ROW_B_CONV:ROW_B_CONV + 1, :] + v1280_ref[CONV_W - 1:CONV_W, :] * xr
        for jj in range(CONV_W - 1):
            st = SUBLANES - (CONV_W - 1) + jj
            acc = acc + v1280_ref[jj:jj + 1, :] * xpad_ref[s, st:st + ls, :]
        xc_ref[sl, :] = acc
        tail = xpad_ref[s, SUBLANES + ls - (CONV_W - 1):SUBLANES + ls, :]
        cst_ref[s] = tail
        if carry:
            xpad_ref[s, SUBLANES - (CONV_W - 1):SUBLANES, :] = tail

    merged = jnp.concatenate(
        [(sga_ref[:, j * MXU_COLS:(j + 1) * MXU_COLS] * ao[j]
          + sgr_ref[:, j * MXU_COLS:(j + 1) * MXU_COLS] * ro[j]).astype(_BF16) for j in range(n_slab)],
        axis=1)
    mo = jnp.concatenate([_dot(merged, w_out_ref[j]) for j in range(n_slab)], axis=1)
    sga_ref[...] = _sigmoid(ga_ref[...])
    sgr_ref[...] = _sigmoid(gr_ref[...])
    yn = _rms(mo) * v1024_ref[1:2, :]
    for s in range(ns):
        sl = slice(s * ls, (s + 1) * ls)
        y_ref[sl, :] = xb_ref[sl, :] + modb_ref[s, MOD_GATE_MIX:MOD_GATE_MIX + 1, :] * yn[sl]
    emit_proj(n_blk, 0)
    assert not proj


def _whole(memory_space=pltpu.VMEM):
    return pl.BlockSpec(memory_space=memory_space)


def _mixer_call(x, mod, hist, sinks, v1024, v1280, w_in, w_rg, w_ao, w_ro, w_out, *, ns, ls):
    n_seq, seq_len, d_model = x.shape
    d_rnn = v1280.shape[1]
    carry = hist is None
    rows = ns * ls
    if carry:
        assert ns == 1 and seq_len % ls == 0 and ls % (2 * CHUNK) == 0
        tiles_per_seq = seq_len // ls
    else:
        assert seq_len == ls == CHUNK and n_seq % ns == 0
        tiles_per_seq = 1
    n_tiles = n_seq * seq_len // rows
    lq = min(ls, 2 * CHUNK)
    xf = x.reshape(n_seq * seq_len, d_model)

    tile_a = lambda i: jnp.minimum(i, n_tiles - 1)
    tile_b = lambda i: jnp.maximum(i - 1, 0)
    seq_a = lambda i: (tile_a(i) // tiles_per_seq, 0, 0)
    seq_b = lambda i: (tile_b(i) // tiles_per_seq, 0, 0)

    in_specs = [pl.BlockSpec((rows, d_model), lambda i: (tile_a(i), 0)),
                pl.BlockSpec((rows, d_model), lambda i: (tile_b(i), 0)),
                pl.BlockSpec((ns, 6, d_model), seq_a),
                pl.BlockSpec((ns, 6, d_model), seq_b)]
    args = [xf, xf, mod, mod]
    if not carry:
        ck, cv, c0, h0 = hist
        in_specs += [pl.BlockSpec((ns, WINDOW, KV_W), seq_a),
                     pl.BlockSpec((ns, WINDOW, KV_W), seq_a),
                     pl.BlockSpec((ns, CONV_W - 1, d_rnn), seq_a),
                     pl.BlockSpec((ns, 1, d_rnn), seq_b)]
        args += [ck, cv, c0, h0]
    in_specs += [_whole(pltpu.SMEM)] + [_whole()] * 7
    args += [sinks, v1024, v1280, w_in, w_rg, w_ao, w_ro, w_out]

    out_shape = (jax.ShapeDtypeStruct((n_seq * seq_len, d_model), _F32),
                 jax.ShapeDtypeStruct((n_seq, WINDOW, KV_W), _F32),
                 jax.ShapeDtypeStruct((n_seq, WINDOW, KV_W), _F32),
                 jax.ShapeDtypeStruct((n_seq, CONV_W - 1, d_rnn), _F32),
                 jax.ShapeDtypeStruct((n_seq, 1, d_rnn), _F32))
    out_specs = (pl.BlockSpec((rows, d_model), lambda i: (tile_b(i), 0)),
                 pl.BlockSpec((ns, WINDOW, KV_W), seq_b),
                 pl.BlockSpec((ns, WINDOW, KV_W), seq_b),
                 pl.BlockSpec((ns, CONV_W - 1, d_rnn), seq_b),
                 pl.BlockSpec((ns, 1, d_rnn), seq_b))
    win_rows = WINDOW + max(ls, 2 * CHUNK)
    scratch = [pltpu.VMEM((rows, d_model), _BF16),
               pltpu.VMEM((rows, N_HEADS * HEAD_DIM), _BF16),
               pltpu.VMEM((rows, KV_W), _F32),
               pltpu.VMEM((rows, KV_W), _F32),
               pltpu.VMEM((rows, d_rnn), _F32),
               pltpu.VMEM((rows, d_rnn), _F32),
               pltpu.VMEM((rows, d_model), _F32),
               pltpu.VMEM((rows, d_model), _F32),
               pltpu.VMEM((rows, d_rnn), _F32),
               pltpu.VMEM((rows, d_model), _F32),
               pltpu.VMEM((rows, d_model), _F32),
               pltpu.VMEM((ns, N_KV_HEADS, 2, win_rows, LANES), _BF16),
               pltpu.VMEM((ns, N_KV_HEADS, 2, win_rows, LANES), _BF16),
               pltpu.VMEM((ns, WINDOW, KV_W), _F32),
               pltpu.VMEM((ns, WINDOW, KV_W), _F32),
               pltpu.VMEM((ns, CONV_W - 1, d_rnn), _F32),
               pltpu.VMEM((N_KV_HEADS, 2, 2 * lq, KEY_WIN), _F32),
               pltpu.VMEM((ns, SUBLANES + ls, d_rnn), _F32),
               pltpu.VMEM((1, d_rnn), _F32),
               pltpu.VMEM((rows, N_HEADS * HEAD_DIM), _BF16),
               pltpu.VMEM((rows, d_rnn), _BF16)]
    kern = functools.partial(_mixer_kernel, carry, ns, ls, tiles_per_seq, d_model, d_rnn)
    return pl.pallas_call(
        kern,
        grid=(n_tiles + 1,),
        in_specs=in_specs,
        out_specs=out_specs,
        out_shape=out_shape,
        scratch_shapes=scratch,
        compiler_params=pltpu.CompilerParams(dimension_semantics=("arbitrary",),
                                             vmem_limit_bytes=VMEM_LIMIT_BYTES),
        name="mixer_stream" if carry else "mixer_step",
    )(*args)


def _ffn_kernel(ns, ls, x_ref, mod_ref, v1024_ref, wg_ref, wu_ref, wd_ref, y_ref, u_ref):
    x = x_ref[...]
    xn = _rms(x) * v1024_ref[0:1, :]
    for s in range(ns):
        sl = slice(s * ls, (s + 1) * ls)
        u_ref[sl, :] = (xn[sl] * (1.0 + mod_ref[s, MOD_SCALE_FFN:MOD_SCALE_FFN + 1, :])
                        + mod_ref[s, MOD_SHIFT_FFN:MOD_SHIFT_FFN + 1, :]).astype(_BF16)
    u = u_ref[...]
    g = _dot(u, wg_ref[...])
    up = _dot(u, wu_ref[...])
    hmid = (_silu(g) * up).astype(_BF16)
    yn = _rms(_dot(hmid, wd_ref[...])) * v1024_ref[1:2, :]
    for s in range(ns):
        sl = slice(s * ls, (s + 1) * ls)
        y_ref[sl, :] = x[sl] + mod_ref[s, MOD_GATE_FFN:MOD_GATE_FFN + 1, :] * yn[sl]


def _ffn_call(xf, mod, v1024, wg, wu, wd, *, ns, ls, seq_len):
    n_rows, d_model = xf.shape
    rows = ns * ls
    if ns == 1:
        tiles_per_seq = seq_len // ls
        seq_of = lambda i: (i // tiles_per_seq, 0, 0)
    else:
        assert ls == seq_len
        seq_of = lambda i: (i, 0, 0)
    return pl.pallas_call(
        functools.partial(_ffn_kernel, ns, ls),
        grid=(n_rows // rows,),
        in_specs=[pl.BlockSpec((rows, d_model), lambda i: (i, 0)),
                  pl.BlockSpec((ns, 6, d_model), seq_of),
                  _whole(), _whole(), _whole(), _whole()],
        out_specs=pl.BlockSpec((rows, d_model), lambda i: (i, 0)),
        out_shape=jax.ShapeDtypeStruct((n_rows, d_model), _F32),
        scratch_shapes=[pltpu.VMEM((rows, d_model), _BF16)],
        compiler_params=pltpu.CompilerParams(dimension_semantics=("arbitrary",),
                                             vmem_limit_bytes=VMEM_LIMIT_BYTES),
        name="ffn",
    )(xf, mod, v1024, wg, wu, wd)


MIX_TILE = 256
MIX_SEQS = 4
FFN_TILE = 512
FFN_SEQS = 8


def kernel(x_prompt, x_sample, c_prompt, c_sample, cache_k, cache_v, state_conv, state_h, w_ada, b_ada, g_pre_mix, g_post_mix, w_in, attn_sinks, w_conv, b_conv, w_rg_a, b_rg_a, w_rg_x, b_rg_x, rg_lambda, w_attn_o, w_rnn_o, w_out, g_pre_ffn, g_post_ffn, w_ffn_gate, w_ffn_up, w_ffn_down):
    depth = w_in.shape[0]
    assert depth == 1
    bp, sp, d_model = x_prompt.shape
    bs, ss, _ = x_sample.shape
    d_rnn = w_conv.shape[-1]
    l = 0

    n_c = bp + bs
    pad = (-n_c) % 16
    c_all = jnp.concatenate([c_prompt, c_sample, jnp.zeros((pad, d_model), _F32)], axis=0)
    mod = _ada_call(c_all, w_ada[l], b_ada[l])[:n_c].reshape(n_c, 6, d_model)
    mod_p, mod_s = mod[:bp], mod[bp:]

    bf = lambda w: w.astype(_BF16)
    v1024_mix = jnp.stack([g_pre_mix[l], g_post_mix[l]])
    v1024_ffn = jnp.stack([g_pre_ffn[l], g_post_ffn[l]])
    v1280 = jnp.concatenate([w_conv[l], b_conv[l][None], b_rg_a[l][None], b_rg_x[l][None],
                             rg_lambda[l][None]], axis=0)
    w_rg = bf(jnp.concatenate([w_rg_a[l], w_rg_x[l]], axis=-1))
    mix_w = (attn_sinks[l], v1024_mix, v1280, _slabs(w_in[l]), w_rg, _slabs(w_attn_o[l]),
             _slabs(w_rnn_o[l]), _slabs(w_out[l]))
    ffn_w = (v1024_ffn, bf(w_ffn_gate[l]), bf(w_ffn_up[l]), bf(w_ffn_down[l]))

    xp1, kp, vp, cp, hp = _mixer_call(x_prompt, mod_p, None, *mix_w, ns=1, ls=MIX_TILE)
    hist = (cache_k[l].reshape(bs, WINDOW, KV_W), cache_v[l].reshape(bs, WINDOW, KV_W),
            state_conv[l], state_h[l].reshape(bs, 1, d_rnn))
    xs1, ks, vs, cs, hs = _mixer_call(x_sample, mod_s, hist, *mix_w, ns=MIX_SEQS, ls=ss)

    yp = _ffn_call(xp1, mod_p, *ffn_w, ns=1, ls=FFN_TILE, seq_len=sp)
    ys = _ffn_call(xs1, mod_s, *ffn_w, ns=FFN_SEQS, ls=ss, seq_len=ss)

    kv_shape = lambda n: (1, n, WINDOW, N_KV_HEADS, HEAD_DIM)
    return (yp.reshape(bp, sp, d_model), ys.reshape(bs, ss, d_model),
            kp.reshape(kv_shape(bp)), vp.reshape(kv_shape(bp)), cp[None], hp.reshape(1, bp, d_rnn),
            ks.reshape(kv_shape(bs)), vs.reshape(kv_shape(bs)), cs[None], hs.reshape(1, bs, d_rnn))
```

```python
import functools
import math

import jax
import jax.numpy as jnp
from jax import lax
from jax.experimental import pallas as pl
from jax.experimental.pallas import tpu as pltpu

CHUNK = 64
N_HEADS = 16
N_KV_HEADS = 4
HEAD_DIM = 64
GROUP = N_HEADS // N_KV_HEADS
WINDOW = 128
N_BACK = WINDOW // CHUNK
KV_W = N_KV_HEADS * HEAD_DIM
RNN_BLOCK = 128
CONV_W = 4
LRU_C = 8.0
EPS = 1e-6
NEG_INF = -1e30

LANES = 128
SUBLANES = 8
MXU_COLS = 256
SLABS_PER_CAST_STEP = 4
KEY_WIN = WINDOW + 2 * CHUNK
VMEM_LIMIT_BYTES = 56 * 1024 * 1024
LOG2E = math.log2(math.e)
ROW_B_CONV, ROW_B_A, ROW_B_X, ROW_LAMBDA = CONV_W, CONV_W + 1, CONV_W + 2, CONV_W + 3
MOD_SHIFT_MIX, MOD_SCALE_MIX, MOD_GATE_MIX, MOD_SHIFT_FFN, MOD_SCALE_FFN, MOD_GATE_FFN = range(6)

_F32 = jnp.float32
_BF16 = jnp.bfloat16


def _dot(a, b):
    return lax.dot_general(a, b, (((1,), (0,)), ((), ())), preferred_element_type=_F32)


def _dot_nt(a, b):
    return lax.dot_general(a, b, (((1,), (1,)), ((), ())), preferred_element_type=_F32)


def _rms(x):
    return x * lax.rsqrt(jnp.mean(x * x, axis=-1, keepdims=True) + EPS)


def _softplus(x):
    return jnp.maximum(x, 0.0) + jnp.log1p(jnp.exp(-jnp.abs(x)))


def _sigmoid(x):
    return 0.5 * jnp.tanh(0.5 * x) + 0.5


def _silu(x):
    h = 0.5 * x
    return h * jnp.tanh(h) + h


def _sqrt_nonneg(x):
    return jnp.where(x > 0.0, x * lax.rsqrt(x), 0.0)


def _neg_expm1_twice(x):
    t = jnp.tanh(x)
    return (-2.0 * t) / (1.0 - t)


def _slab_cast_kernel(n_slabs, w_ref, o_ref):
    for j in range(n_slabs):
        o_ref[j] = w_ref[:, j * MXU_COLS:(j + 1) * MXU_COLS].astype(_BF16)


def _slabs(w):
    k, n = w.shape
    total = n // MXU_COLS
    per_step = math.gcd(total, SLABS_PER_CAST_STEP)
    return pl.pallas_call(
        functools.partial(_slab_cast_kernel, per_step),
        grid=(total // per_step,),
        in_specs=[pl.BlockSpec((k, per_step * MXU_COLS), lambda j: (0, j))],
        out_specs=pl.BlockSpec((per_step, k, MXU_COLS), lambda j: (j, 0, 0)),
        out_shape=jax.ShapeDtypeStruct((total, k, MXU_COLS), _BF16),
        compiler_params=pltpu.CompilerParams(dimension_semantics=("arbitrary",)),
        name="slab_cast",
    )(w)


def _gelu_tanh(x):
    c = math.sqrt(2.0 / math.pi)
    hx = 0.5 * x
    return hx + hx * jnp.tanh(x * (c + (c * 0.044715) * (x * x)))


def _split_heads(kv, scale=None):
    n = kv.shape[0]
    lo = lax.broadcasted_iota(jnp.int32, (n, LANES), 1) < HEAD_DIM
    outs = []
    for blk in range(KV_W // LANES):
        b = kv[:, blk * LANES:(blk + 1) * LANES]
        if scale is not None:
            b = b * scale
        r = pltpu.roll(b, HEAD_DIM, axis=1)
        outs.append((jnp.where(lo, b, 0.0).astype(_BF16), jnp.where(lo, 0.0, r).astype(_BF16)))
        outs.append((jnp.where(lo, r, 0.0).astype(_BF16), jnp.where(lo, 0.0, b).astype(_BF16)))
    return outs


def _ada_kernel(c_ref, w_ref, b_ref, o_ref):
    c = c_ref[...]
    a = _silu(c).astype(_BF16)
    o_ref[...] = _dot(a, w_ref[...].astype(_BF16)) + b_ref[...]


def _ada_call(c, w_ada, b_ada):
    rows, d = c.shape
    n = w_ada.shape[1]
    bn = d
    return pl.pallas_call(
        _ada_kernel,
        grid=(n // bn,),
        in_specs=[
            pl.BlockSpec((rows, d), lambda j: (0, 0)),
            pl.BlockSpec((d, bn), lambda j: (0, j)),
            pl.BlockSpec((1, bn), lambda j: (0, j)),
        ],
        out_specs=pl.BlockSpec((rows, bn), lambda j: (0, j)),
        out_shape=jax.ShapeDtypeStruct((rows, n), _F32),
        compiler_params=pltpu.CompilerParams(dimension_semantics=("arbitrary",)),
        name="ada",
    )(c, w_ada, b_ada.reshape(1, n))


def _mixer_kernel(skew, carry, ns, ls, tiles_per_seq, d_model, d_rnn, *refs):
    if skew:
        xa_ref, xb_ref, moda_ref, modb_ref = refs[:4]
        refs = refs[4:]
    else:
        xa_ref, moda_ref = xb_ref, modb_ref = refs[:2]
        refs = refs[2:]
    n_in = 0 if carry else 4
    if not carry:
        ck_ref, cv_ref, c0_ref, h0_ref = refs[:4]
    (sinks_ref, v1024_ref, v1280_ref, w_in_ref, w_rg_ref, w_ao_ref, w_ro_ref, w_out_ref,
     y_ref, ks_ref, vs_ref, cs_ref, hs_ref,
     u_ref, q_ref, k_ref, v_ref, xr_ref, yr_ref, ga_ref, gr_ref, xc_ref, sga_ref, sgr_ref,
     kwin_ref, vwin_ref, kst_ref, vst_ref, cst_ref, bias_ref, xpad_ref, hcar_ref,
     attn_ref, rnn_ref) = refs[n_in:]

    i = pl.program_id(0)
    rows = ns * ls
    lq = min(ls, 2 * CHUNK)
    n_win = ls // lq
    q_w = N_HEADS * HEAD_DIM
    n_blk = d_rnn // RNN_BLOCK
    if carry:
        mixed = i + tiles_per_seq - 1 if skew else i
        keep_mix = jnp.where(lax.rem(mixed, tiles_per_seq) == 0, 0.0, 1.0)
        keep_prep = jnp.where(lax.rem(i, tiles_per_seq) == 0, 0.0, 1.0)

    @pl.when(i == 0)
    def _():
        for ref in (q_ref, yr_ref, xc_ref, sga_ref, sgr_ref, kwin_ref, vwin_ref, kst_ref,
                    vst_ref, cst_ref, xpad_ref, hcar_ref):
            ref[...] = jnp.zeros(ref.shape, ref.dtype)
        t = lax.broadcasted_iota(jnp.int32, (lq, KEY_WIN), 0)
        j = lax.broadcasted_iota(jnp.int32, (lq, KEY_WIN), 1)
        dist = jnp.abs(t + WINDOW - j).astype(_F32)
        qc = t // CHUNK
        kc = j // CHUNK
        base = jnp.where(kc >= qc, jnp.where(kc <= qc + N_BACK, 0.0, NEG_INF), NEG_INF)
        for h in range(N_KV_HEADS):
            for g in range(GROUP):
                slope = 2.0 ** (-8.0 * (h * GROUP + g + 1) / N_HEADS)
                bias_ref[h, g % 2, (g // 2) * lq:(g // 2 + 1) * lq, :] = LOG2E * (base - slope * dist)

    def publish_state():
        ks_ref[...] = kst_ref[...]
        vs_ref[...] = vst_ref[...]
        cs_ref[...] = cst_ref[...]

    def norm_input():
        xn = _rms(xa_ref[...]) * v1024_ref[0:1, :]
        for s in range(ns):
            sl = slice(s * ls, (s + 1) * ls)
            u_ref[sl, :] = (xn[sl] * (1.0 + moda_ref[s, MOD_SCALE_MIX:MOD_SCALE_MIX + 1, :])
                            + moda_ref[s, MOD_SHIFT_MIX:MOD_SHIFT_MIX + 1, :]).astype(_BF16)

    proj = []
    off = 0
    for ref, width, free_after in (
            (q_ref, q_w, lambda c: 2 * (c // MXU_COLS) + 1),
            (k_ref, KV_W, lambda c: -1), (v_ref, KV_W, lambda c: -1),
            (xr_ref, d_rnn, lambda c: -1),
            (yr_ref, d_rnn, lambda c: (c + MXU_COLS - 1) // RNN_BLOCK),
            (ga_ref, d_model, lambda c: -1), (gr_ref, d_model, lambda c: -1)):
        for c in range(0, width, MXU_COLS):
            proj.append((free_after(c), ref, c, off + c))
        off += width
    proj.sort(key=lambda p: p[0])

    def emit_proj(slot, slots_left):
        ready = [p for p in proj if p[0] <= slot]
        quota = -(-len(proj) // slots_left) if slots_left else len(proj)
        for p in ready[:quota]:
            proj.remove(p)
            _, ref, c, wc = p
            ref[:, c:c + MXU_COLS] = _dot(u_ref[...], w_in_ref[wc // MXU_COLS]).astype(ref.dtype)

    lo_sel = lax.broadcasted_iota(jnp.int32, (lq, LANES), 1) < HEAD_DIM
    if carry:
        kcol = lax.broadcasted_iota(jnp.int32, (1, KEY_WIN), 1)
        hist_bias = jnp.where(kcol < WINDOW, NEG_INF, 0.0).astype(_F32) * (1.0 - keep_mix)

    def attend(s, w, h):
        r0 = s * ls + w * lq
        kw = slice(w * lq, w * lq + KEY_WIN)
        c0 = h * GROUP * HEAD_DIM
        qab = jnp.concatenate([q_ref[r0:r0 + lq, c0:c0 + LANES],
                               q_ref[r0:r0 + lq, c0 + LANES:c0 + 2 * LANES]], axis=0)
        ps, inv = [[], []], [None] * GROUP
        for half in range(2):
            sc = _dot_nt(qab, kwin_ref[s, h, half, kw, :]) + bias_ref[h, half]
            if carry and w == 0:
                sc = sc + hist_bias
            for pair in range(2):
                g = 2 * pair + half
                sg = sc[pair * lq:(pair + 1) * lq]
                sink = LOG2E * sinks_ref[h * GROUP + g]
                mg = jnp.maximum(jnp.max(sg, axis=-1, keepdims=True), sink)
                pg = jnp.exp2(sg - mg)
                inv[g] = 1.0 / (jnp.sum(pg, axis=-1, keepdims=True) + jnp.exp2(sink - mg))
                ps[half].append(pg.astype(_BF16))
        o = (_dot(jnp.concatenate(ps[0], axis=0), vwin_ref[s, h, 0, kw, :])
             + _dot(jnp.concatenate(ps[1], axis=0), vwin_ref[s, h, 1, kw, :]))
        for pair in range(2):
            norm = jnp.where(lo_sel, inv[2 * pair], inv[2 * pair + 1])
            attn_ref[r0:r0 + lq, c0 + pair * LANES:c0 + (pair + 1) * LANES] = (
                o[pair * lq:(pair + 1) * lq] * norm).astype(_BF16)

    units = [(s, w, h) for h in range(N_KV_HEADS) for s in range(ns) for w in range(n_win)]
    units_per_slot = len(units) // (2 * N_KV_HEADS)

    sub = lax.broadcasted_iota(jnp.int32, (ls // SUBLANES, SUBLANES, RNN_BLOCK), 1)

    def recur(n):
        cs_ = slice(n * RNN_BLOCK, (n + 1) * RNN_BLOCK)
        xc = xc_ref[:, cs_]
        gates = _dot(xc.astype(_BF16), w_rg_ref[n])
        half_c = (-0.5 * LRU_C) * _softplus(-v1280_ref[ROW_LAMBDA:ROW_LAMBDA + 1, cs_])
        half_ba = 0.5 * v1280_ref[ROW_B_A:ROW_B_A + 1, cs_]
        log_a = half_c * jnp.tanh(0.5 * gates[:, :RNN_BLOCK] + half_ba) + half_c
        ig = _sigmoid(gates[:, RNN_BLOCK:] + v1280_ref[ROW_B_X:ROW_B_X + 1, cs_])
        a = jnp.exp(log_a)
        b = _sqrt_nonneg(_neg_expm1_twice(log_a)) * (ig * xc)
        hs = []
        for s in range(ns):
            sl = slice(s * ls, (s + 1) * ls)
            a3 = a[sl].reshape(ls // SUBLANES, SUBLANES, RNN_BLOCK)
            b3 = b[sl].reshape(ls // SUBLANES, SUBLANES, RNN_BLOCK)
            d = 1
            while d < SUBLANES:
                keep = sub >= d
                b3 = b3 + a3 * jnp.where(keep, pltpu.roll(b3, d, axis=1), 0.0)
                a3 = a3 * jnp.where(keep, pltpu.roll(a3, d, axis=1), 1.0)
                d *= 2
            hprev = hcar_ref[0:1, cs_] * keep_mix if carry else h0_ref[s, 0:1, cs_]
            for jb in range(ls // SUBLANES):
                hj = b3[jb] + a3[jb] * hprev
                hs.append(hj)
                hprev = hj[SUBLANES - 1:SUBLANES, :]
            hs_ref[s, 0:1, cs_] = hprev
            if carry:
                hcar_ref[0:1, cs_] = hprev
        hfull = jnp.concatenate(hs, axis=0)
        rnn_ref[:, cs_] = (hfull * _gelu_tanh(yr_ref[:, cs_])).astype(_BF16)

    def main_loop(with_proj):
        for n in range(n_blk):
            recur(n)
            for _ in range(units_per_slot):
                if units:
                    attend(*units.pop(0))
            if with_proj:
                emit_proj(n, n_blk - n)
        assert not units

    n_slab = d_model // MXU_COLS

    def branch_projections():
        return ([_dot(attn_ref[...], w_ao_ref[j]) for j in range(n_slab)],
                [_dot(rnn_ref[...], w_ro_ref[j]) for j in range(n_slab)])

    def prepare_window_and_conv():
        qk_scale = LOG2E * HEAD_DIM ** -0.5
        if carry:
            kwin_ref[0, :, :, 0:WINDOW, :] = kwin_ref[0, :, :, ls:ls + WINDOW, :]
            vwin_ref[0, :, :, 0:WINDOW, :] = vwin_ref[0, :, :, ls:ls + WINDOW, :]
            tail_rows = slice(SUBLANES - (CONV_W - 1), SUBLANES)
            xpad_ref[0, tail_rows, :] = xpad_ref[0, tail_rows, :] * keep_prep
        for s in range(ns):
            sl = slice(s * ls, (s + 1) * ls)
            kd = _split_heads(k_ref[sl, :], qk_scale)
            vd = _split_heads(v_ref[sl, :])
            for h in range(N_KV_HEADS):
                for half in range(2):
                    kwin_ref[s, h, half, WINDOW:WINDOW + ls, :] = kd[h][half]
                    vwin_ref[s, h, half, WINDOW:WINDOW + ls, :] = vd[h][half]
            if carry:
                kst_ref[0] = k_ref[rows - WINDOW:rows, :]
                vst_ref[0] = v_ref[rows - WINDOW:rows, :]
            else:
                hk = _split_heads(ck_ref[s], qk_scale)
                hv = _split_heads(cv_ref[s])
                pad = KEY_WIN - WINDOW - ls
                for h in range(N_KV_HEADS):
                    for half in range(2):
                        kwin_ref[s, h, half, 0:WINDOW, :] = hk[h][half]
                        vwin_ref[s, h, half, 0:WINDOW, :] = hv[h][half]
                        kwin_ref[s, h, half, WINDOW + ls:KEY_WIN, :] = jnp.zeros((pad, LANES), _BF16)
                        vwin_ref[s, h, half, WINDOW + ls:KEY_WIN, :] = jnp.zeros((pad, LANES), _BF16)
                kst_ref[s, 0:WINDOW - ls, :] = ck_ref[s, ls:WINDOW, :]
                kst_ref[s, WINDOW - ls:WINDOW, :] = k_ref[sl, :]
                vst_ref[s, 0:WINDOW - ls, :] = cv_ref[s, ls:WINDOW, :]
                vst_ref[s, WINDOW - ls:WINDOW, :] = v_ref[sl, :]

            xr = xr_ref[sl, :]
            if not carry:
                xpad_ref[s, SUBLANES - (CONV_W - 1):SUBLANES, :] = c0_ref[s]
            xpad_ref[s, SUBLANES:SUBLANES + ls, :] = xr
            acc = v1280_ref[ROW_B_CONV:ROW_B_CONV + 1, :] + v1280_ref[CONV_W - 1:CONV_W, :] * xr
            for jj in range(CONV_W - 1):
                st = SUBLANES - (CONV_W - 1) + jj
                acc = acc + v1280_ref[jj:jj + 1, :] * xpad_ref[s, st:st + ls, :]
            xc_ref[sl, :] = acc
            tail = xpad_ref[s, SUBLANES + ls - (CONV_W - 1):SUBLANES + ls, :]
            cst_ref[s] = tail
            if carry:
                xpad_ref[s, SUBLANES - (CONV_W - 1):SUBLANES, :] = tail

    def output_projection(ao, ro):
        merged = jnp.concatenate(
            [(sga_ref[:, j * MXU_COLS:(j + 1) * MXU_COLS] * ao[j]
              + sgr_ref[:, j * MXU_COLS:(j + 1) * MXU_COLS] * ro[j]).astype(_BF16)
             for j in range(n_slab)], axis=1)
        return jnp.concatenate([_dot(merged, w_out_ref[j]) for j in range(n_slab)], axis=1)

    def gate_sigmoids():
        sga_ref[...] = _sigmoid(ga_ref[...])
        sgr_ref[...] = _sigmoid(gr_ref[...])

    def residual(mo):
        yn = _rms(mo) * v1024_ref[1:2, :]
        for s in range(ns):
            sl = slice(s * ls, (s + 1) * ls)
            y_ref[sl, :] = xb_ref[sl, :] + modb_ref[s, MOD_GATE_MIX:MOD_GATE_MIX + 1, :] * yn[sl]

    if skew:
        publish_state()
        norm_input()
        main_loop(with_proj=True)
        ao, ro = branch_projections()
        prepare_window_and_conv()
        mo = output_projection(ao, ro)
        gate_sigmoids()
        residual(mo)
        emit_proj(n_blk, 0)
    else:
        norm_input()
        emit_proj(n_blk, 0)
        prepare_window_and_conv()
        gate_sigmoids()
        main_loop(with_proj=False)
        residual(output_projection(*branch_projections()))
        publish_state()
    assert not proj


def _whole(memory_space=pltpu.VMEM):
    return pl.BlockSpec(memory_space=memory_space)


def _mixer_call(x, mod, hist, sinks, v1024, v1280, w_in, w_rg, w_ao, w_ro, w_out, *, ns, ls, skew):
    n_seq, seq_len, d_model = x.shape
    d_rnn = v1280.shape[1]
    carry = hist is None
    rows = ns * ls
    if carry:
        assert ns == 1 and seq_len % ls == 0 and ls % (2 * CHUNK) == 0
        tiles_per_seq = seq_len // ls
    else:
        assert seq_len == ls == CHUNK and n_seq % ns == 0
        tiles_per_seq = 1
    n_tiles = n_seq * seq_len // rows
    lq = min(ls, 2 * CHUNK)
    xf = x.reshape(n_seq * seq_len, d_model)

    if skew:
        tile_a = lambda i: jnp.minimum(i, n_tiles - 1)
        tile_b = lambda i: jnp.maximum(i - 1, 0)
    else:
        tile_a = tile_b = lambda i: i
    seq_a = lambda i: (tile_a(i) // tiles_per_seq, 0, 0)
    seq_b = lambda i: (tile_b(i) // tiles_per_seq, 0, 0)

    in_specs = [pl.BlockSpec((rows, d_model), lambda i: (tile_a(i), 0)),
                pl.BlockSpec((ns, 6, d_model), seq_a)]
    args = [xf, mod]
    if skew:
        in_specs = [in_specs[0], pl.BlockSpec((rows, d_model), lambda i: (tile_b(i), 0)),
                    in_specs[1], pl.BlockSpec((ns, 6, d_model), seq_b)]
        args = [xf, xf, mod, mod]
    if not carry:
        ck, cv, c0, h0 = hist
        in_specs += [pl.BlockSpec((ns, WINDOW, KV_W), seq_a),
                     pl.BlockSpec((ns, WINDOW, KV_W), seq_a),
                     pl.BlockSpec((ns, CONV_W - 1, d_rnn), seq_a),
                     pl.BlockSpec((ns, 1, d_rnn), seq_b)]
        args += [ck, cv, c0, h0]
    in_specs += [_whole(pltpu.SMEM)] + [_whole()] * 7
    args += [sinks, v1024, v1280, w_in, w_rg, w_ao, w_ro, w_out]

    out_shape = (jax.ShapeDtypeStruct((n_seq * seq_len, d_model), _F32),
                 jax.ShapeDtypeStruct((n_seq, WINDOW, KV_W), _F32),
                 jax.ShapeDtypeStruct((n_seq, WINDOW, KV_W), _F32),
                 jax.ShapeDtypeStruct((n_seq, CONV_W - 1, d_rnn), _F32),
                 jax.ShapeDtypeStruct((n_seq, 1, d_rnn), _F32))
    out_specs = (pl.BlockSpec((rows, d_model), lambda i: (tile_b(i), 0)),
                 pl.BlockSpec((ns, WINDOW, KV_W), seq_b),
                 pl.BlockSpec((ns, WINDOW, KV_W), seq_b),
                 pl.BlockSpec((ns, CONV_W - 1, d_rnn), seq_b),
                 pl.BlockSpec((ns, 1, d_rnn), seq_b))
    win_rows = WINDOW + max(ls, 2 * CHUNK)
    scratch = [pltpu.VMEM((rows, d_model), _BF16),
               pltpu.VMEM((rows, N_HEADS * HEAD_DIM), _BF16),
               pltpu.VMEM((rows, KV_W), _F32),
               pltpu.VMEM((rows, KV_W), _F32),
               pltpu.VMEM((rows, d_rnn), _F32),
               pltpu.VMEM((rows, d_rnn), _F32),
               pltpu.VMEM((rows, d_model), _F32),
               pltpu.VMEM((rows, d_model), _F32),
               pltpu.VMEM((rows, d_rnn), _F32),
               pltpu.VMEM((rows, d_model), _F32),
               pltpu.VMEM((rows, d_model), _F32),
               pltpu.VMEM((ns, N_KV_HEADS, 2, win_rows, LANES), _BF16),
               pltpu.VMEM((ns, N_KV_HEADS, 2, win_rows, LANES), _BF16),
               pltpu.VMEM((ns, WINDOW, KV_W), _F32),
               pltpu.VMEM((ns, WINDOW, KV_W), _F32),
               pltpu.VMEM((ns, CONV_W - 1, d_rnn), _F32),
               pltpu.VMEM((N_KV_HEADS, 2, 2 * lq, KEY_WIN), _F32),
               pltpu.VMEM((ns, SUBLANES + ls, d_rnn), _F32),
               pltpu.VMEM((1, d_rnn), _F32),
               pltpu.VMEM((rows, N_HEADS * HEAD_DIM), _BF16),
               pltpu.VMEM((rows, d_rnn), _BF16)]
    kern = functools.partial(_mixer_kernel, skew, carry, ns, ls, tiles_per_seq, d_model, d_rnn)
    return pl.pallas_call(
        kern,
        grid=(n_tiles + (1 if skew else 0),),
        in_specs=in_specs,
        out_specs=out_specs,
        out_shape=out_shape,
        scratch_shapes=scratch,
        compiler_params=pltpu.CompilerParams(dimension_semantics=("arbitrary",),
                                             vmem_limit_bytes=VMEM_LIMIT_BYTES),
        name="mixer_stream" if carry else "mixer_step",
    )(*args)


def _ffn_kernel(ns, ls, x_ref, mod_ref, v1024_ref, wg_ref, wu_ref, wd_ref, y_ref, u_ref):
    x = x_ref[...]
    xn = _rms(x) * v1024_ref[0:1, :]
    for s in range(ns):
        sl = slice(s * ls, (s + 1) * ls)
        u_ref[sl, :] = (xn[sl] * (1.0 + mod_ref[s, MOD_SCALE_FFN:MOD_SCALE_FFN + 1, :])
                        + mod_ref[s, MOD_SHIFT_FFN:MOD_SHIFT_FFN + 1, :]).astype(_BF16)
    u = u_ref[...]
    g = _dot(u, wg_ref[...])
    up = _dot(u, wu_ref[...])
    hmid = (_silu(g) * up).astype(_BF16)
    yn = _rms(_dot(hmid, wd_ref[...])) * v1024_ref[1:2, :]
    for s in range(ns):
        sl = slice(s * ls, (s + 1) * ls)
        y_ref[sl, :] = x[sl] + mod_ref[s, MOD_GATE_FFN:MOD_GATE_FFN + 1, :] * yn[sl]


def _ffn_call(xf, mod, v1024, wg, wu, wd, *, ns, ls, seq_len):
    n_rows, d_model = xf.shape
    rows = ns * ls
    if ns == 1:
        tiles_per_seq = seq_len // ls
        seq_of = lambda i: (i // tiles_per_seq, 0, 0)
    else:
        assert ls == seq_len
        seq_of = lambda i: (i, 0, 0)
    return pl.pallas_call(
        functools.partial(_ffn_kernel, ns, ls),
        grid=(n_rows // rows,),
        in_specs=[pl.BlockSpec((rows, d_model), lambda i: (i, 0)),
                  pl.BlockSpec((ns, 6, d_model), seq_of),
                  _whole(), _whole(), _whole(), _whole()],
        out_specs=pl.BlockSpec((rows, d_model), lambda i: (i, 0)),
        out_shape=jax.ShapeDtypeStruct((n_rows, d_model), _F32),
        scratch_shapes=[pltpu.VMEM((rows, d_model), _BF16)],
        compiler_params=pltpu.CompilerParams(dimension_semantics=("arbitrary",),
                                             vmem_limit_bytes=VMEM_LIMIT_BYTES),
        name="ffn",
    )(xf, mod, v1024, wg, wu, wd)


MIX_TILE = 256
MIX_SEQS = 4
SKEW_PROMPT = True
SKEW_SAMPLE = False
FFN_TILE = 512
FFN_SEQS = 8


def kernel(x_prompt, x_sample, c_prompt, c_sample, cache_k, cache_v, state_conv, state_h, w_ada, b_ada, g_pre_mix, g_post_mix, w_in, attn_sinks, w_conv, b_conv, w_rg_a, b_rg_a, w_rg_x, b_rg_x, rg_lambda, w_attn_o, w_rnn_o, w_out, g_pre_ffn, g_post_ffn, w_ffn_gate, w_ffn_up, w_ffn_down):
    depth = w_in.shape[0]
    assert depth == 1
    bp, sp, d_model = x_prompt.shape
    bs, ss, _ = x_sample.shape
    d_rnn = w_conv.shape[-1]
    l = 0

    n_c = bp + bs
    pad = (-n_c) % 16
    c_all = jnp.concatenate([c_prompt, c_sample, jnp.zeros((pad, d_model), _F32)], axis=0)
    mod = _ada_call(c_all, w_ada[l], b_ada[l])[:n_c].reshape(n_c, 6, d_model)
    mod_p, mod_s = mod[:bp], mod[bp:]

    bf = lambda w: w.astype(_BF16)
    v1024_mix = jnp.stack([g_pre_mix[l], g_post_mix[l]])
    v1024_ffn = jnp.stack([g_pre_ffn[l], g_post_ffn[l]])
    v1280 = jnp.concatenate([w_conv[l], b_conv[l][None], b_rg_a[l][None], b_rg_x[l][None],
                             rg_lambda[l][None]], axis=0)
    w_rg = bf(jnp.concatenate([w_rg_a[l], w_rg_x[l]], axis=-1))
    mix_w = (attn_sinks[l], v1024_mix, v1280, _slabs(w_in[l]), w_rg, _slabs(w_attn_o[l]),
             _slabs(w_rnn_o[l]), _slabs(w_out[l]))
    ffn_w = (v1024_ffn, bf(w_ffn_gate[l]), bf(w_ffn_up[l]), bf(w_ffn_down[l]))

    xp1, kp, vp, cp, hp = _mixer_call(x_prompt, mod_p, None, *mix_w, ns=1, ls=MIX_TILE,
                                      skew=SKEW_PROMPT)
    hist = (cache_k[l].reshape(bs, WINDOW, KV_W), cache_v[l].reshape(bs, WINDOW, KV_W),
            state_conv[l], state_h[l].reshape(bs, 1, d_rnn))
    xs1, ks, vs, cs, hs = _mixer_call(x_sample, mod_s, hist, *mix_w, ns=MIX_SEQS, ls=ss,
                                      skew=SKEW_SAMPLE)

    yp = _ffn_call(xp1, mod_p, *ffn_w, ns=1, ls=FFN_TILE, seq_len=sp)
    ys = _ffn_call(xs1, mod_s, *ffn_w, ns=FFN_SEQS, ls=ss, seq_len=ss)

    kv_shape = lambda n: (1, n, WINDOW, N_KV_HEADS, HEAD_DIM)
    return (yp.reshape(bp, sp, d_model), ys.reshape(bs, ss, d_model),
            kp.reshape(kv_shape(bp)), vp.reshape(kv_shape(bp)), cp[None], hp.reshape(1, bp, d_rnn),
            ks.reshape(kv_shape(bs)), vs.reshape(kv_shape(bs)), cs[None], hs.reshape(1, bs, d_rnn))
```

```python
import functools
import math

import jax
import jax.numpy as jnp
from jax import lax
from jax.experimental import pallas as pl
from jax.experimental.pallas import tpu as pltpu

CHUNK = 64
N_HEADS = 16
N_KV_HEADS = 4
HEAD_DIM = 64
GROUP = N_HEADS // N_KV_HEADS
WINDOW = 128
N_BACK = WINDOW // CHUNK
KV_W = N_KV_HEADS * HEAD_DIM
RNN_BLOCK = 128
CONV_W = 4
LRU_C = 8.0
EPS = 1e-6
NEG_INF = -1e30

LANES = 128
SUBLANES = 8
MXU_COLS = 256
SLABS_PER_CAST_STEP = 4
KEY_WIN = WINDOW + 2 * CHUNK
VMEM_LIMIT_BYTES = 56 * 1024 * 1024
LOG2E = math.log2(math.e)
ROW_B_CONV, ROW_B_A, ROW_B_X, ROW_LAMBDA = CONV_W, CONV_W + 1, CONV_W + 2, CONV_W + 3
MOD_SHIFT_MIX, MOD_SCALE_MIX, MOD_GATE_MIX, MOD_SHIFT_FFN, MOD_SCALE_FFN, MOD_GATE_FFN = range(6)

_F32 = jnp.float32
_BF16 = jnp.bfloat16


def _dot(a, b):
    return lax.dot_general(a, b, (((1,), (0,)), ((), ())), preferred_element_type=_F32)


def _dot_nt(a, b):
    return lax.dot_general(a, b, (((1,), (1,)), ((), ())), preferred_element_type=_F32)


def _rms(x):
    return x * lax.rsqrt(jnp.mean(x * x, axis=-1, keepdims=True) + EPS)


def _softplus(x):
    return jnp.maximum(x, 0.0) + jnp.log1p(jnp.exp(-jnp.abs(x)))


def _sigmoid(x):
    return 0.5 * jnp.tanh(0.5 * x) + 0.5


def _silu(x):
    h = 0.5 * x
    return h * jnp.tanh(h) + h


def _sqrt_nonneg(x):
    return jnp.where(x > 0.0, x * lax.rsqrt(x), 0.0)


def _neg_expm1_twice(x):
    t = jnp.tanh(x)
    return (-2.0 * t) / (1.0 - t)


def _slab_cast_kernel(n_slabs, w_ref, o_ref):
    for j in range(n_slabs):
        o_ref[j] = w_ref[:, j * MXU_COLS:(j + 1) * MXU_COLS].astype(_BF16)


def _slabs(w):
    k, n = w.shape
    total = n // MXU_COLS
    per_step = math.gcd(total, SLABS_PER_CAST_STEP)
    return pl.pallas_call(
        functools.partial(_slab_cast_kernel, per_step),
        grid=(total // per_step,),
        in_specs=[pl.BlockSpec((k, per_step * MXU_COLS), lambda j: (0, j))],
        out_specs=pl.BlockSpec((per_step, k, MXU_COLS), lambda j: (j, 0, 0)),
        out_shape=jax.ShapeDtypeStruct((total, k, MXU_COLS), _BF16),
        compiler_params=pltpu.CompilerParams(dimension_semantics=("arbitrary",)),
        name="slab_cast",
    )(w)


def _gelu_tanh(x):
    c = math.sqrt(2.0 / math.pi)
    hx = 0.5 * x
    return hx + hx * jnp.tanh(x * (c + (c * 0.044715) * (x * x)))


def _split_heads(kv, scale=None):
    n = kv.shape[0]
    lo = lax.broadcasted_iota(jnp.int32, (n, LANES), 1) < HEAD_DIM
    outs = []
    for blk in range(KV_W // LANES):
        b = kv[:, blk * LANES:(blk + 1) * LANES]
        if scale is not None:
            b = b * scale
        r = pltpu.roll(b, HEAD_DIM, axis=1)
        outs.append((jnp.where(lo, b, 0.0).astype(_BF16), jnp.where(lo, 0.0, r).astype(_BF16)))
        outs.append((jnp.where(lo, r, 0.0).astype(_BF16), jnp.where(lo, 0.0, b).astype(_BF16)))
    return outs


def _ada_kernel(c_ref, w_ref, b_ref, o_ref):
    c = c_ref[...]
    a = _silu(c).astype(_BF16)
    o_ref[...] = _dot(a, w_ref[...].astype(_BF16)) + b_ref[...]


def _ada_call(c, w_ada, b_ada):
    rows, d = c.shape
    n = w_ada.shape[1]
    bn = d
    return pl.pallas_call(
        _ada_kernel,
        grid=(n // bn,),
        in_specs=[
            pl.BlockSpec((rows, d), lambda j: (0, 0)),
            pl.BlockSpec((d, bn), lambda j: (0, j)),
            pl.BlockSpec((1, bn), lambda j: (0, j)),
        ],
        out_specs=pl.BlockSpec((rows, bn), lambda j: (0, j)),
        out_shape=jax.ShapeDtypeStruct((rows, n), _F32),
        compiler_params=pltpu.CompilerParams(dimension_semantics=("arbitrary",)),
        name="ada",
    )(c, w_ada, b_ada.reshape(1, n))


def _mixer_kernel(skew, carry, ns, ls, tiles_per_seq, d_model, d_rnn, *refs):
    if skew:
        xa_ref, xb_ref, moda_ref, modb_ref = refs[:4]
        refs = refs[4:]
    else:
        xa_ref, moda_ref = xb_ref, modb_ref = refs[:2]
        refs = refs[2:]
    n_in = 0 if carry else 4
    if not carry:
        ck_ref, cv_ref, c0_ref, h0_ref = refs[:4]
    (sinks_ref, v1024_ref, v1280_ref, w_in_ref, w_rg_ref, w_ao_ref, w_ro_ref, w_out_ref,
     y_ref, ks_ref, vs_ref, cs_ref, hs_ref,
     u_ref, us_ref, q_ref, k_ref, v_ref, xr_ref, yr_ref, ga_ref, gr_ref, xc_ref, sga_ref, sgr_ref,
     kwin_ref, vwin_ref, kst_ref, vst_ref, cst_ref, bias_ref, hist_ref, hcar_ref,
     attn_ref, rnn_ref, to_strand_ref, from_strand_ref) = refs[n_in:]

    i = pl.program_id(0)
    rows = ns * ls
    lq = min(ls, 2 * CHUNK)
    n_win = ls // lq
    q_w = N_HEADS * HEAD_DIM
    n_blk = d_rnn // RNN_BLOCK
    steps = ls // SUBLANES
    if carry:
        mixed = i + tiles_per_seq - 1 if skew else i
        keep_mix = jnp.where(lax.rem(mixed, tiles_per_seq) == 0, 0.0, 1.0)
        keep_prep = jnp.where(lax.rem(i, tiles_per_seq) == 0, 0.0, 1.0)

    @pl.when(i == 0)
    def _():
        for ref in (q_ref, yr_ref, xc_ref, sga_ref, sgr_ref, kwin_ref, vwin_ref, kst_ref,
                    vst_ref, cst_ref, hist_ref, hcar_ref):
            ref[...] = jnp.zeros(ref.shape, ref.dtype)
        r = lax.broadcasted_iota(jnp.int32, (rows, rows), 0)
        c = lax.broadcasted_iota(jnp.int32, (rows, rows), 1)
        seg, rr = (r // ls) * ls, r % ls
        to_strand_ref[...] = jnp.where(
            c == seg + (rr % SUBLANES) * steps + rr // SUBLANES, 1.0, 0.0).astype(_BF16)
        from_strand_ref[...] = jnp.where(
            c == seg + (rr % steps) * SUBLANES + rr // steps, 1.0, 0.0).astype(_BF16)
        t = lax.broadcasted_iota(jnp.int32, (lq, KEY_WIN), 0)
        j = lax.broadcasted_iota(jnp.int32, (lq, KEY_WIN), 1)
        dist = jnp.abs(t + WINDOW - j).astype(_F32)
        qc = t // CHUNK
        kc = j // CHUNK
        base = jnp.where(kc >= qc, jnp.where(kc <= qc + N_BACK, 0.0, NEG_INF), NEG_INF)
        for h in range(N_KV_HEADS):
            for g in range(GROUP):
                slope = 2.0 ** (-8.0 * (h * GROUP + g + 1) / N_HEADS)
                bias_ref[h, g % 2, (g // 2) * lq:(g // 2 + 1) * lq, :] = LOG2E * (base - slope * dist)

    def publish_state():
        ks_ref[...] = kst_ref[...]
        vs_ref[...] = vst_ref[...]
        cs_ref[...] = cst_ref[...]

    def norm_input():
        xn = _rms(xa_ref[...]) * v1024_ref[0:1, :]
        for s in range(ns):
            sl = slice(s * ls, (s + 1) * ls)
            u_ref[sl, :] = (xn[sl] * (1.0 + moda_ref[s, MOD_SCALE_MIX:MOD_SCALE_MIX + 1, :])
                            + moda_ref[s, MOD_SHIFT_MIX:MOD_SHIFT_MIX + 1, :]).astype(_BF16)
        us_ref[...] = _dot(to_strand_ref[...], u_ref[...]).astype(_BF16)

    proj = []
    off = 0
    for src, ref, width, free_after in (
            (u_ref, q_ref, q_w, lambda c: 2 * (c // MXU_COLS) + 1),
            (u_ref, k_ref, KV_W, lambda c: -1), (u_ref, v_ref, KV_W, lambda c: -1),
            (us_ref, xr_ref, d_rnn, lambda c: -1),
            (us_ref, yr_ref, d_rnn, lambda c: (c + MXU_COLS - 1) // RNN_BLOCK),
            (u_ref, ga_ref, d_model, lambda c: -1), (u_ref, gr_ref, d_model, lambda c: -1)):
        for c in range(0, width, MXU_COLS):
            proj.append((free_after(c), src, ref, c, off + c))
        off += width
    proj.sort(key=lambda p: p[0])

    def emit_proj(slot, slots_left):
        ready = [p for p in proj if p[0] <= slot]
        quota = -(-len(proj) // slots_left) if slots_left else len(proj)
        for p in ready[:quota]:
            proj.remove(p)
            _, src, ref, c, wc = p
            ref[:, c:c + MXU_COLS] = _dot(src[...], w_in_ref[wc // MXU_COLS]).astype(ref.dtype)

    lo_sel = lax.broadcasted_iota(jnp.int32, (lq, LANES), 1) < HEAD_DIM
    if carry:
        kcol = lax.broadcasted_iota(jnp.int32, (1, KEY_WIN), 1)
        hist_bias = jnp.where(kcol < WINDOW, NEG_INF, 0.0).astype(_F32) * (1.0 - keep_mix)

    def attend(s, w, h):
        r0 = s * ls + w * lq
        kw = slice(w * lq, w * lq + KEY_WIN)
        c0 = h * GROUP * HEAD_DIM
        qab = jnp.concatenate([q_ref[r0:r0 + lq, c0:c0 + LANES],
                               q_ref[r0:r0 + lq, c0 + LANES:c0 + 2 * LANES]], axis=0)
        ps, inv = [[], []], [None] * GROUP
        for half in range(2):
            sc = _dot_nt(qab, kwin_ref[s, h, half, kw, :]) + bias_ref[h, half]
            if carry and w == 0:
                sc = sc + hist_bias
            for pair in range(2):
                g = 2 * pair + half
                sg = sc[pair * lq:(pair + 1) * lq]
                sink = LOG2E * sinks_ref[h * GROUP + g]
                mg = jnp.maximum(jnp.max(sg, axis=-1, keepdims=True), sink)
                pg = jnp.exp2(sg - mg)
                inv[g] = 1.0 / (jnp.sum(pg, axis=-1, keepdims=True) + jnp.exp2(sink - mg))
                ps[half].append(pg.astype(_BF16))
        o = (_dot(jnp.concatenate(ps[0], axis=0), vwin_ref[s, h, 0, kw, :])
             + _dot(jnp.concatenate(ps[1], axis=0), vwin_ref[s, h, 1, kw, :]))
        for pair in range(2):
            norm = jnp.where(lo_sel, inv[2 * pair], inv[2 * pair + 1])
            attn_ref[r0:r0 + lq, c0 + pair * LANES:c0 + (pair + 1) * LANES] = (
                o[pair * lq:(pair + 1) * lq] * norm).astype(_BF16)

    units = [(s, w, h) for h in range(N_KV_HEADS) for s in range(ns) for w in range(n_win)]
    units_per_slot = len(units) // (2 * N_KV_HEADS)

    sub = lax.broadcasted_iota(jnp.int32, (SUBLANES, RNN_BLOCK), 0)

    def recur(n):
        cs_ = slice(n * RNN_BLOCK, (n + 1) * RNN_BLOCK)
        xc = xc_ref[:, cs_]
        gates = _dot(xc.astype(_BF16), w_rg_ref[n])
        half_c = (-0.5 * LRU_C) * _softplus(-v1280_ref[ROW_LAMBDA:ROW_LAMBDA + 1, cs_])
        half_ba = 0.5 * v1280_ref[ROW_B_A:ROW_B_A + 1, cs_]
        log_a = half_c * jnp.tanh(0.5 * gates[:, :RNN_BLOCK] + half_ba) + half_c
        ig = _sigmoid(gates[:, RNN_BLOCK:] + v1280_ref[ROW_B_X:ROW_B_X + 1, cs_])
        a = jnp.exp(log_a)
        b = _sqrt_nonneg(_neg_expm1_twice(log_a)) * (ig * xc)
        hs = []
        for s in range(ns):
            sl = slice(s * ls, (s + 1) * ls)
            a3 = a[sl].reshape(steps, SUBLANES, RNN_BLOCK)
            b3 = b[sl].reshape(steps, SUBLANES, RNN_BLOCK)
            hz, ap = [b3[0]], [a3[0]]
            for jb in range(1, steps):
                hz.append(a3[jb] * hz[-1] + b3[jb])
                ap.append(a3[jb] * ap[-1])
            ae, he = ap[-1], hz[-1]
            d = 1
            while d < SUBLANES:
                keep = sub >= d
                he = he + ae * jnp.where(keep, pltpu.roll(he, d, axis=0), 0.0)
                ae = ae * jnp.where(keep, pltpu.roll(ae, d, axis=0), 1.0)
                d *= 2
            hprev = hcar_ref[0:1, cs_] * keep_mix if carry else h0_ref[s, 0:1, cs_]
            after = he + ae * hprev
            before = jnp.where(sub == 0, hprev, pltpu.roll(after, 1, axis=0))
            hs.extend(hz[jb] + ap[jb] * before for jb in range(steps))
            hlast = after[SUBLANES - 1:SUBLANES, :]
            hs_ref[s, 0:1, cs_] = hlast
            if carry:
                hcar_ref[0:1, cs_] = hlast
        hfull = jnp.concatenate(hs, axis=0)
        rnn_ref[:, cs_] = (hfull * _gelu_tanh(yr_ref[:, cs_])).astype(_BF16)

    def main_loop(with_proj):
        for n in range(n_blk):
            recur(n)
            for _ in range(units_per_slot):
                if units:
                    attend(*units.pop(0))
            if with_proj:
                emit_proj(n, n_blk - n)
        assert not units

    n_slab = d_model // MXU_COLS

    def branch_projections():
        rnn = _dot(from_strand_ref[...], rnn_ref[...]).astype(_BF16)
        return ([_dot(attn_ref[...], w_ao_ref[j]) for j in range(n_slab)],
                [_dot(rnn, w_ro_ref[j]) for j in range(n_slab)])

    def prepare_window_and_conv():
        qk_scale = LOG2E * HEAD_DIM ** -0.5
        if carry:
            kwin_ref[0, :, :, 0:WINDOW, :] = kwin_ref[0, :, :, ls:ls + WINDOW, :]
            vwin_ref[0, :, :, 0:WINDOW, :] = vwin_ref[0, :, :, ls:ls + WINDOW, :]
        for s in range(ns):
            sl = slice(s * ls, (s + 1) * ls)
            kd = _split_heads(k_ref[sl, :], qk_scale)
            vd = _split_heads(v_ref[sl, :])
            for h in range(N_KV_HEADS):
                for half in range(2):
                    kwin_ref[s, h, half, WINDOW:WINDOW + ls, :] = kd[h][half]
                    vwin_ref[s, h, half, WINDOW:WINDOW + ls, :] = vd[h][half]
            if carry:
                kst_ref[0] = k_ref[rows - WINDOW:rows, :]
                vst_ref[0] = v_ref[rows - WINDOW:rows, :]
            else:
                hk = _split_heads(ck_ref[s], qk_scale)
                hv = _split_heads(cv_ref[s])
                pad = KEY_WIN - WINDOW - ls
                for h in range(N_KV_HEADS):
                    for half in range(2):
                        kwin_ref[s, h, half, 0:WINDOW, :] = hk[h][half]
                        vwin_ref[s, h, half, 0:WINDOW, :] = hv[h][half]
                        kwin_ref[s, h, half, WINDOW + ls:KEY_WIN, :] = jnp.zeros((pad, LANES), _BF16)
                        vwin_ref[s, h, half, WINDOW + ls:KEY_WIN, :] = jnp.zeros((pad, LANES), _BF16)
                kst_ref[s, 0:WINDOW - ls, :] = ck_ref[s, ls:WINDOW, :]
                kst_ref[s, WINDOW - ls:WINDOW, :] = k_ref[sl, :]
                vst_ref[s, 0:WINDOW - ls, :] = cv_ref[s, ls:WINDOW, :]
                vst_ref[s, WINDOW - ls:WINDOW, :] = v_ref[sl, :]

            hist = c0_ref[s] if not carry else hist_ref[0] * keep_prep
            first = lax.broadcasted_iota(jnp.int32, (SUBLANES, LANES), 0) == 0
            for col in range(0, d_rnn, LANES):
                lanes = slice(col, col + LANES)
                taps = [jnp.broadcast_to(v1280_ref[r:r + 1, lanes], (SUBLANES, LANES))
                        for r in range(CONV_W + 1)]
                blocks = [xr_ref[s * ls + jb * SUBLANES:s * ls + (jb + 1) * SUBLANES, lanes]
                          for jb in range(steps)]

                def earlier(jb, d):
                    if jb >= d:
                        return blocks[jb - d]
                    row = CONV_W - 1 + jb - d
                    prev_strand = pltpu.roll(blocks[steps + jb - d], 1, axis=0)
                    return jnp.where(first, hist[row:row + 1, lanes], prev_strand)

                for jb in range(steps):
                    acc = taps[ROW_B_CONV] + taps[CONV_W - 1] * blocks[jb]
                    for d in range(1, CONV_W):
                        acc = acc + taps[CONV_W - 1 - d] * earlier(jb, d)
                    xc_ref[s * ls + jb * SUBLANES:s * ls + (jb + 1) * SUBLANES, lanes] = acc
            tail = jnp.concatenate(
                [xr_ref[s * ls + (steps - d + 1) * SUBLANES - 1:s * ls + (steps - d + 1) * SUBLANES, :]
                 for d in range(CONV_W - 1, 0, -1)], axis=0)
            cst_ref[s] = tail
            if carry:
                hist_ref[s] = tail

    def output_projection(ao, ro):
        merged = jnp.concatenate(
            [(sga_ref[:, j * MXU_COLS:(j + 1) * MXU_COLS] * ao[j]
              + sgr_ref[:, j * MXU_COLS:(j + 1) * MXU_COLS] * ro[j]).astype(_BF16)
             for j in range(n_slab)], axis=1)
        return jnp.concatenate([_dot(merged, w_out_ref[j]) for j in range(n_slab)], axis=1)

    def gate_sigmoids():
        sga_ref[...] = _sigmoid(ga_ref[...])
        sgr_ref[...] = _sigmoid(gr_ref[...])

    def residual(mo):
        yn = _rms(mo) * v1024_ref[1:2, :]
        for s in range(ns):
            sl = slice(s * ls, (s + 1) * ls)
            y_ref[sl, :] = xb_ref[sl, :] + modb_ref[s, MOD_GATE_MIX:MOD_GATE_MIX + 1, :] * yn[sl]

    if skew:
        publish_state()
        norm_input()
        main_loop(with_proj=True)
        ao, ro = branch_projections()
        prepare_window_and_conv()
        mo = output_projection(ao, ro)
        gate_sigmoids()
        residual(mo)
        emit_proj(n_blk, 0)
    else:
        norm_input()
        emit_proj(n_blk, 0)
        prepare_window_and_conv()
        gate_sigmoids()
        main_loop(with_proj=False)
        residual(output_projection(*branch_projections()))
        publish_state()
    assert not proj


def _whole(memory_space=pltpu.VMEM):
    return pl.BlockSpec(memory_space=memory_space)


def _mixer_call(x, mod, hist, sinks, v1024, v1280, w_in, w_rg, w_ao, w_ro, w_out, *, ns, ls, skew):
    n_seq, seq_len, d_model = x.shape
    d_rnn = v1280.shape[1]
    carry = hist is None
    rows = ns * ls
    if carry:
        assert ns == 1 and seq_len % ls == 0 and ls % (2 * CHUNK) == 0
        tiles_per_seq = seq_len // ls
    else:
        assert seq_len == ls == CHUNK and n_seq % ns == 0
        tiles_per_seq = 1
    n_tiles = n_seq * seq_len // rows
    lq = min(ls, 2 * CHUNK)
    xf = x.reshape(n_seq * seq_len, d_model)

    if skew:
        tile_a = lambda i: jnp.minimum(i, n_tiles - 1)
        tile_b = lambda i: jnp.maximum(i - 1, 0)
    else:
        tile_a = tile_b = lambda i: i
    seq_a = lambda i: (tile_a(i) // tiles_per_seq, 0, 0)
    seq_b = lambda i: (tile_b(i) // tiles_per_seq, 0, 0)

    in_specs = [pl.BlockSpec((rows, d_model), lambda i: (tile_a(i), 0)),
                pl.BlockSpec((ns, 6, d_model), seq_a)]
    args = [xf, mod]
    if skew:
        in_specs = [in_specs[0], pl.BlockSpec((rows, d_model), lambda i: (tile_b(i), 0)),
                    in_specs[1], pl.BlockSpec((ns, 6, d_model), seq_b)]
        args = [xf, xf, mod, mod]
    if not carry:
        ck, cv, c0, h0 = hist
        in_specs += [pl.BlockSpec((ns, WINDOW, KV_W), seq_a),
                     pl.BlockSpec((ns, WINDOW, KV_W), seq_a),
                     pl.BlockSpec((ns, CONV_W - 1, d_rnn), seq_a),
                     pl.BlockSpec((ns, 1, d_rnn), seq_b)]
        args += [ck, cv, c0, h0]
    in_specs += [_whole(pltpu.SMEM)] + [_whole()] * 7
    args += [sinks, v1024, v1280, w_in, w_rg, w_ao, w_ro, w_out]

    out_shape = (jax.ShapeDtypeStruct((n_seq * seq_len, d_model), _F32),
                 jax.ShapeDtypeStruct((n_seq, WINDOW, KV_W), _F32),
                 jax.ShapeDtypeStruct((n_seq, WINDOW, KV_W), _F32),
                 jax.ShapeDtypeStruct((n_seq, CONV_W - 1, d_rnn), _F32),
                 jax.ShapeDtypeStruct((n_seq, 1, d_rnn), _F32))
    out_specs = (pl.BlockSpec((rows, d_model), lambda i: (tile_b(i), 0)),
                 pl.BlockSpec((ns, WINDOW, KV_W), seq_b),
                 pl.BlockSpec((ns, WINDOW, KV_W), seq_b),
                 pl.BlockSpec((ns, CONV_W - 1, d_rnn), seq_b),
                 pl.BlockSpec((ns, 1, d_rnn), seq_b))
    win_rows = WINDOW + max(ls, 2 * CHUNK)
    scratch = [pltpu.VMEM((rows, d_model), _BF16),
               pltpu.VMEM((rows, d_model), _BF16),
               pltpu.VMEM((rows, N_HEADS * HEAD_DIM), _BF16),
               pltpu.VMEM((rows, KV_W), _F32),
               pltpu.VMEM((rows, KV_W), _F32),
               pltpu.VMEM((rows, d_rnn), _F32),
               pltpu.VMEM((rows, d_rnn), _F32),
               pltpu.VMEM((rows, d_model), _F32),
               pltpu.VMEM((rows, d_model), _F32),
               pltpu.VMEM((rows, d_rnn), _F32),
               pltpu.VMEM((rows, d_model), _F32),
               pltpu.VMEM((rows, d_model), _F32),
               pltpu.VMEM((ns, N_KV_HEADS, 2, win_rows, LANES), _BF16),
               pltpu.VMEM((ns, N_KV_HEADS, 2, win_rows, LANES), _BF16),
               pltpu.VMEM((ns, WINDOW, KV_W), _F32),
               pltpu.VMEM((ns, WINDOW, KV_W), _F32),
               pltpu.VMEM((ns, CONV_W - 1, d_rnn), _F32),
               pltpu.VMEM((N_KV_HEADS, 2, 2 * lq, KEY_WIN), _F32),
               pltpu.VMEM((ns, CONV_W - 1, d_rnn), _F32),
               pltpu.VMEM((1, d_rnn), _F32),
               pltpu.VMEM((rows, N_HEADS * HEAD_DIM), _BF16),
               pltpu.VMEM((rows, d_rnn), _BF16),
               pltpu.VMEM((rows, rows), _BF16),
               pltpu.VMEM((rows, rows), _BF16)]
    kern = functools.partial(_mixer_kernel, skew, carry, ns, ls, tiles_per_seq, d_model, d_rnn)
    return pl.pallas_call(
        kern,
        grid=(n_tiles + (1 if skew else 0),),
        in_specs=in_specs,
        out_specs=out_specs,
        out_shape=out_shape,
        scratch_shapes=scratch,
        compiler_params=pltpu.CompilerParams(dimension_semantics=("arbitrary",),
                                             vmem_limit_bytes=VMEM_LIMIT_BYTES),
        name="mixer_stream" if carry else "mixer_step",
    )(*args)


def _ffn_kernel(ns, ls, x_ref, mod_ref, v1024_ref, wg_ref, wu_ref, wd_ref, y_ref, u_ref):
    x = x_ref[...]
    xn = _rms(x) * v1024_ref[0:1, :]
    for s in range(ns):
        sl = slice(s * ls, (s + 1) * ls)
        u_ref[sl, :] = (xn[sl] * (1.0 + mod_ref[s, MOD_SCALE_FFN:MOD_SCALE_FFN + 1, :])
                        + mod_ref[s, MOD_SHIFT_FFN:MOD_SHIFT_FFN + 1, :]).astype(_BF16)
    u = u_ref[...]
    g = _dot(u, wg_ref[...])
    up = _dot(u, wu_ref[...])
    hmid = (_silu(g) * up).astype(_BF16)
    yn = _rms(_dot(hmid, wd_ref[...])) * v1024_ref[1:2, :]
    for s in range(ns):
        sl = slice(s * ls, (s + 1) * ls)
        y_ref[sl, :] = x[sl] + mod_ref[s, MOD_GATE_FFN:MOD_GATE_FFN + 1, :] * yn[sl]


def _ffn_call(xf, mod, v1024, wg, wu, wd, *, ns, ls, seq_len):
    n_rows, d_model = xf.shape
    rows = ns * ls
    if ns == 1:
        tiles_per_seq = seq_len // ls
        seq_of = lambda i: (i // tiles_per_seq, 0, 0)
    else:
        assert ls == seq_len
        seq_of = lambda i: (i, 0, 0)
    return pl.pallas_call(
        functools.partial(_ffn_kernel, ns, ls),
        grid=(n_rows // rows,),
        in_specs=[pl.BlockSpec((rows, d_model), lambda i: (i, 0)),
                  pl.BlockSpec((ns, 6, d_model), seq_of),
                  _whole(), _whole(), _whole(), _whole()],
        out_specs=pl.BlockSpec((rows, d_model), lambda i: (i, 0)),
        out_shape=jax.ShapeDtypeStruct((n_rows, d_model), _F32),
        scratch_shapes=[pltpu.VMEM((rows, d_model), _BF16)],
        compiler_params=pltpu.CompilerParams(dimension_semantics=("arbitrary",),
                                             vmem_limit_bytes=VMEM_LIMIT_BYTES),
        name="ffn",
    )(xf, mod, v1024, wg, wu, wd)


MIX_TILE = 256
MIX_SEQS = 4
SKEW_PROMPT = True
SKEW_SAMPLE = False
FFN_TILE = 512
FFN_SEQS = 8


def kernel(x_prompt, x_sample, c_prompt, c_sample, cache_k, cache_v, state_conv, state_h, w_ada, b_ada, g_pre_mix, g_post_mix, w_in, attn_sinks, w_conv, b_conv, w_rg_a, b_rg_a, w_rg_x, b_rg_x, rg_lambda, w_attn_o, w_rnn_o, w_out, g_pre_ffn, g_post_ffn, w_ffn_gate, w_ffn_up, w_ffn_down):
    depth = w_in.shape[0]
    assert depth == 1
    bp, sp, d_model = x_prompt.shape
    bs, ss, _ = x_sample.shape
    d_rnn = w_conv.shape[-1]
    l = 0

    n_c = bp + bs
    pad = (-n_c) % 16
    c_all = jnp.concatenate([c_prompt, c_sample, jnp.zeros((pad, d_model), _F32)], axis=0)
    mod = _ada_call(c_all, w_ada[l], b_ada[l])[:n_c].reshape(n_c, 6, d_model)
    mod_p, mod_s = mod[:bp], mod[bp:]

    bf = lambda w: w.astype(_BF16)
    v1024_mix = jnp.stack([g_pre_mix[l], g_post_mix[l]])
    v1024_ffn = jnp.stack([g_pre_ffn[l], g_post_ffn[l]])
    v1280 = jnp.concatenate([w_conv[l], b_conv[l][None], b_rg_a[l][None], b_rg_x[l][None],
                             rg_lambda[l][None]], axis=0)
    w_rg = bf(jnp.concatenate([w_rg_a[l], w_rg_x[l]], axis=-1))
    mix_w = (attn_sinks[l], v1024_mix, v1280, _slabs(w_in[l]), w_rg, _slabs(w_attn_o[l]),
             _slabs(w_rnn_o[l]), _slabs(w_out[l]))
    ffn_w = (v1024_ffn, bf(w_ffn_gate[l]), bf(w_ffn_up[l]), bf(w_ffn_down[l]))

    xp1, kp, vp, cp, hp = _mixer_call(x_prompt, mod_p, None, *mix_w, ns=1, ls=MIX_TILE,
                                      skew=SKEW_PROMPT)
    hist = (cache_k[l].reshape(bs, WINDOW, KV_W), cache_v[l].reshape(bs, WINDOW, KV_W),
            state_conv[l], state_h[l].reshape(bs, 1, d_rnn))
    xs1, ks, vs, cs, hs = _mixer_call(x_sample, mod_s, hist, *mix_w, ns=MIX_SEQS, ls=ss,
                                      skew=SKEW_SAMPLE)

    yp = _ffn_call(xp1, mod_p, *ffn_w, ns=1, ls=FFN_TILE, seq_len=sp)
    ys = _ffn_call(xs1, mod_s, *ffn_w, ns=FFN_SEQS, ls=ss, seq_len=ss)

    kv_shape = lambda n: (1, n, WINDOW, N_KV_HEADS, HEAD_DIM)
    return (yp.reshape(bp, sp, d_model), ys.reshape(bs, ss, d_model),
            kp.reshape(kv_shape(bp)), vp.reshape(kv_shape(bp)), cp[None], hp.reshape(1, bp, d_rnn),
            ks.reshape(kv_shape(bs)), vs.reshape(kv_shape(bs)), cs[None], hs.reshape(1, bs, d_rnn))
```

```python
import functools
import math

import jax
import jax.numpy as jnp
from jax import lax
from jax.experimental import pallas as pl
from jax.experimental.pallas import tpu as pltpu

CHUNK = 64
N_HEADS = 16
N_KV_HEADS = 4
HEAD_DIM = 64
GROUP = N_HEADS // N_KV_HEADS
WINDOW = 128
N_BACK = WINDOW // CHUNK
KV_W = N_KV_HEADS * HEAD_DIM
RNN_BLOCK = 128
CONV_W = 4
LRU_C = 8.0
EPS = 1e-6
NEG_INF = -1e30

LANES = 128
SUBLANES = 8
MXU_COLS = 256
SLABS_PER_CAST_STEP = 4
KEY_WIN = WINDOW + 2 * CHUNK
MIXER_VALUE_TILES = 14
FFN_VALUE_TILES = 2
FFN_WIDE_VALUES = 3
LOG2E = math.log2(math.e)
ROW_B_CONV, ROW_B_A, ROW_B_X, ROW_LAMBDA = CONV_W, CONV_W + 1, CONV_W + 2, CONV_W + 3
MOD_SHIFT_MIX, MOD_SCALE_MIX, MOD_GATE_MIX, MOD_SHIFT_FFN, MOD_SCALE_FFN, MOD_GATE_FFN = range(6)

_F32 = jnp.float32
_BF16 = jnp.bfloat16


def _dot(a, b):
    return lax.dot_general(a, b, (((1,), (0,)), ((), ())), preferred_element_type=_F32)


def _dot_nt(a, b):
    return lax.dot_general(a, b, (((1,), (1,)), ((), ())), preferred_element_type=_F32)


def _rms(x):
    return x * lax.rsqrt(jnp.mean(x * x, axis=-1, keepdims=True) + EPS)


def _softplus(x):
    return jnp.maximum(x, 0.0) + jnp.log1p(jnp.exp(-jnp.abs(x)))


def _sigmoid(x):
    return 0.5 * jnp.tanh(0.5 * x) + 0.5


def _silu(x):
    h = 0.5 * x
    return h * jnp.tanh(h) + h


def _sqrt_nonneg(x):
    return jnp.exp2((0.5 * LOG2E) * jnp.log(x))


def _neg_expm1_twice(x):
    t = jnp.tanh(x)
    return (-2.0 * t) / (1.0 - t)


def _slab_cast_kernel(n_slabs, w_ref, o_ref):
    for j in range(n_slabs):
        o_ref[j] = w_ref[:, j * MXU_COLS:(j + 1) * MXU_COLS].astype(_BF16)


def _slabs(w):
    k, n = w.shape
    total = n // MXU_COLS
    per_step = math.gcd(total, SLABS_PER_CAST_STEP)
    return pl.pallas_call(
        functools.partial(_slab_cast_kernel, per_step),
        grid=(total // per_step,),
        in_specs=[pl.BlockSpec((k, per_step * MXU_COLS), lambda j: (0, j))],
        out_specs=pl.BlockSpec((per_step, k, MXU_COLS), lambda j: (j, 0, 0)),
        out_shape=jax.ShapeDtypeStruct((total, k, MXU_COLS), _BF16),
        compiler_params=pltpu.CompilerParams(dimension_semantics=("arbitrary",)),
        name="slab_cast",
    )(w)


def _gelu_tanh(x):
    c = math.sqrt(2.0 / math.pi)
    hx = 0.5 * x
    return hx + hx * jnp.tanh(x * (c + (c * 0.044715) * (x * x)))


def _split_heads(kv, scale=None):
    n = kv.shape[0]
    lo = lax.broadcasted_iota(jnp.int32, (n, LANES), 1) < HEAD_DIM
    outs = []
    for blk in range(KV_W // LANES):
        b = kv[:, blk * LANES:(blk + 1) * LANES]
        if scale is not None:
            b = b * scale
        r = pltpu.roll(b, HEAD_DIM, axis=1)
        outs.append((jnp.where(lo, b, 0.0).astype(_BF16), jnp.where(lo, 0.0, r).astype(_BF16)))
        outs.append((jnp.where(lo, r, 0.0).astype(_BF16), jnp.where(lo, 0.0, b).astype(_BF16)))
    return outs


def _ada_kernel(c_ref, w_ref, b_ref, o_ref):
    c = c_ref[...]
    a = _silu(c).astype(_BF16)
    o_ref[...] = _dot(a, w_ref[...].astype(_BF16)) + b_ref[...]


def _ada_call(c, w_ada, b_ada):
    rows, d = c.shape
    n = w_ada.shape[1]
    bn = d
    return pl.pallas_call(
        _ada_kernel,
        grid=(n // bn,),
        in_specs=[
            pl.BlockSpec((rows, d), lambda j: (0, 0)),
            pl.BlockSpec((d, bn), lambda j: (0, j)),
            pl.BlockSpec((1, bn), lambda j: (0, j)),
        ],
        out_specs=pl.BlockSpec((rows, bn), lambda j: (0, j)),
        out_shape=jax.ShapeDtypeStruct((rows, n), _F32),
        compiler_params=pltpu.CompilerParams(dimension_semantics=("arbitrary",)),
        name="ada",
    )(c, w_ada, b_ada.reshape(1, n))


def _mixer_kernel(skew, carry, ns, ls, tiles_per_seq, d_model, d_rnn, *refs):
    if skew:
        xa_ref, xb_ref, moda_ref, modb_ref = refs[:4]
        refs = refs[4:]
    else:
        xa_ref, moda_ref = xb_ref, modb_ref = refs[:2]
        refs = refs[2:]
    n_in = 0 if carry else 4
    if not carry:
        ck_ref, cv_ref, c0_ref, h0_ref = refs[:4]
    (sinks_ref, v1024_ref, v1280_ref, w_in_ref, w_rg_ref, w_ao_ref, w_ro_ref, w_out_ref,
     y_ref, ks_ref, vs_ref, cs_ref, hs_ref,
     u_ref, us_ref, q_ref, k_ref, v_ref, xr_ref, yr_ref, ga_ref, gr_ref, xc_ref, sga_ref, sgr_ref,
     kwin_ref, vwin_ref, kst_ref, vst_ref, cst_ref, bias_ref, hist_ref, hcar_ref,
     attn_ref, rnn_ref, to_strand_ref, from_strand_ref) = refs[n_in:]

    i = pl.program_id(0)
    rows = ns * ls
    lq = min(ls, 2 * CHUNK)
    n_win = ls // lq
    q_w = N_HEADS * HEAD_DIM
    n_blk = d_rnn // RNN_BLOCK
    steps = ls // SUBLANES
    if carry:
        mixed = i + tiles_per_seq - 1 if skew else i
        keep_mix = jnp.where(lax.rem(mixed, tiles_per_seq) == 0, 0.0, 1.0)
        keep_prep = jnp.where(lax.rem(i, tiles_per_seq) == 0, 0.0, 1.0)

    @pl.when(i == 0)
    def _():
        for ref in (q_ref, yr_ref, xc_ref, sga_ref, sgr_ref, kwin_ref, vwin_ref, kst_ref,
                    vst_ref, cst_ref, hist_ref, hcar_ref):
            ref[...] = jnp.zeros(ref.shape, ref.dtype)
        r = lax.broadcasted_iota(jnp.int32, (rows, rows), 0)
        c = lax.broadcasted_iota(jnp.int32, (rows, rows), 1)
        seg, rr = (r // ls) * ls, r % ls
        to_strand_ref[...] = jnp.where(
            c == seg + (rr % SUBLANES) * steps + rr // SUBLANES, 1.0, 0.0).astype(_BF16)
        from_strand_ref[...] = jnp.where(
            c == seg + (rr % steps) * SUBLANES + rr // steps, 1.0, 0.0).astype(_BF16)
        t = lax.broadcasted_iota(jnp.int32, (lq, KEY_WIN), 0)
        j = lax.broadcasted_iota(jnp.int32, (lq, KEY_WIN), 1)
        dist = jnp.abs(t + WINDOW - j).astype(_F32)
        qc = t // CHUNK
        kc = j // CHUNK
        base = jnp.where(kc >= qc, jnp.where(kc <= qc + N_BACK, 0.0, NEG_INF), NEG_INF)
        for h in range(N_KV_HEADS):
            for g in range(GROUP):
                slope = 2.0 ** (-8.0 * (h * GROUP + g + 1) / N_HEADS)
                bias_ref[h, g % 2, (g // 2) * lq:(g // 2 + 1) * lq, :] = LOG2E * (base - slope * dist)

    def publish_state():
        ks_ref[...] = kst_ref[...]
        vs_ref[...] = vst_ref[...]
        cs_ref[...] = cst_ref[...]

    def norm_input():
        xn = _rms(xa_ref[...])
        for s in range(ns):
            sl = slice(s * ls, (s + 1) * ls)
            gain = v1024_ref[0:1, :] * (1.0 + moda_ref[s, MOD_SCALE_MIX:MOD_SCALE_MIX + 1, :])
            u_ref[sl, :] = (xn[sl] * gain + moda_ref[s, MOD_SHIFT_MIX:MOD_SHIFT_MIX + 1, :]).astype(_BF16)
        us_ref[...] = _dot(to_strand_ref[...], u_ref[...]).astype(_BF16)

    proj = []
    off = 0
    for src, ref, width, free_after in (
            (u_ref, q_ref, q_w, lambda c: 2 * (c // MXU_COLS) + 1),
            (u_ref, k_ref, KV_W, lambda c: -1), (u_ref, v_ref, KV_W, lambda c: -1),
            (us_ref, xr_ref, d_rnn, lambda c: -1),
            (us_ref, yr_ref, d_rnn, lambda c: (c + MXU_COLS - 1) // RNN_BLOCK),
            (u_ref, ga_ref, d_model, lambda c: -1), (u_ref, gr_ref, d_model, lambda c: -1)):
        for c in range(0, width, MXU_COLS):
            proj.append((free_after(c), src, ref, c, off + c))
        off += width
    proj.sort(key=lambda p: p[0])

    def emit_proj(slot, slots_left):
        ready = [p for p in proj if p[0] <= slot]
        quota = -(-len(proj) // slots_left) if slots_left else len(proj)
        for p in ready[:quota]:
            proj.remove(p)
            _, src, ref, c, wc = p
            ref[:, c:c + MXU_COLS] = _dot(src[...], w_in_ref[wc // MXU_COLS]).astype(ref.dtype)

    lo_sel = lax.broadcasted_iota(jnp.int32, (lq, LANES), 1) < HEAD_DIM
    if carry:
        kcol = lax.broadcasted_iota(jnp.int32, (1, KEY_WIN), 1)
        hist_bias = jnp.where(kcol < WINDOW, NEG_INF, 0.0).astype(_F32) * (1.0 - keep_mix)

    def attend(s, w, h):
        r0 = s * ls + w * lq
        kw = slice(w * lq, w * lq + KEY_WIN)
        c0 = h * GROUP * HEAD_DIM
        qab = jnp.concatenate([q_ref[r0:r0 + lq, c0:c0 + LANES],
                               q_ref[r0:r0 + lq, c0 + LANES:c0 + 2 * LANES]], axis=0)
        ps, inv = [[], []], [None] * GROUP
        for half in range(2):
            sc = _dot_nt(qab, kwin_ref[s, h, half, kw, :]) + bias_ref[h, half]
            if carry and w == 0:
                sc = sc + hist_bias
            for pair in range(2):
                g = 2 * pair + half
                sg = sc[pair * lq:(pair + 1) * lq]
                sink = LOG2E * sinks_ref[h * GROUP + g]
                mg = jnp.maximum(jnp.max(sg, axis=-1, keepdims=True), sink)
                pg = jnp.exp2(sg - mg)
                inv[g] = 1.0 / (jnp.sum(pg, axis=-1, keepdims=True) + jnp.exp2(sink - mg))
                ps[half].append(pg.astype(_BF16))
        o = (_dot(jnp.concatenate(ps[0], axis=0), vwin_ref[s, h, 0, kw, :])
             + _dot(jnp.concatenate(ps[1], axis=0), vwin_ref[s, h, 1, kw, :]))
        for pair in range(2):
            norm = jnp.where(lo_sel, inv[2 * pair], inv[2 * pair + 1])
            attn_ref[r0:r0 + lq, c0 + pair * LANES:c0 + (pair + 1) * LANES] = (
                o[pair * lq:(pair + 1) * lq] * norm).astype(_BF16)

    units = [(s, w, h) for h in range(N_KV_HEADS) for s in range(ns) for w in range(n_win)]
    units_per_slot = len(units) // (2 * N_KV_HEADS)

    sub = lax.broadcasted_iota(jnp.int32, (SUBLANES, RNN_BLOCK), 0)

    def recur(n):
        cs_ = slice(n * RNN_BLOCK, (n + 1) * RNN_BLOCK)
        half_xc = 0.5 * xc_ref[:, cs_]
        half_gates = _dot(half_xc.astype(_BF16), w_rg_ref[n])
        half_c = (-0.5 * LRU_C) * _softplus(-v1280_ref[ROW_LAMBDA:ROW_LAMBDA + 1, cs_])
        half_ba = 0.5 * v1280_ref[ROW_B_A:ROW_B_A + 1, cs_]
        half_bx = 0.5 * v1280_ref[ROW_B_X:ROW_B_X + 1, cs_]
        log_a = half_c * jnp.tanh(half_gates[:, :RNN_BLOCK] + half_ba) + half_c
        gated_x = jnp.tanh(half_gates[:, RNN_BLOCK:] + half_bx) * half_xc + half_xc
        a = jnp.exp(log_a)
        b = _sqrt_nonneg(_neg_expm1_twice(log_a)) * gated_x
        hs = []
        for s in range(ns):
            sl = slice(s * ls, (s + 1) * ls)
            a3 = a[sl].reshape(steps, SUBLANES, RNN_BLOCK)
            b3 = b[sl].reshape(steps, SUBLANES, RNN_BLOCK)
            hz, ap = [b3[0]], [a3[0]]
            for jb in range(1, steps):
                hz.append(a3[jb] * hz[-1] + b3[jb])
                ap.append(a3[jb] * ap[-1])
            ae, he = ap[-1], hz[-1]
            d = 1
            while d < SUBLANES:
                keep = sub >= d
                he = he + ae * jnp.where(keep, pltpu.roll(he, d, axis=0), 0.0)
                ae = ae * jnp.where(keep, pltpu.roll(ae, d, axis=0), 1.0)
                d *= 2
            hprev = hcar_ref[0:1, cs_] * keep_mix if carry else h0_ref[s, 0:1, cs_]
            after = he + ae * hprev
            before = jnp.where(sub == 0, hprev, pltpu.roll(after, 1, axis=0))
            hs.extend(hz[jb] + ap[jb] * before for jb in range(steps))
            hlast = after[SUBLANES - 1:SUBLANES, :]
            hs_ref[s, 0:1, cs_] = hlast
            if carry:
                hcar_ref[0:1, cs_] = hlast
        hfull = jnp.concatenate(hs, axis=0)
        rnn_ref[:, cs_] = (hfull * _gelu_tanh(yr_ref[:, cs_])).astype(_BF16)

    def main_loop(with_proj):
        for n in range(n_blk):
            recur(n)
            for _ in range(units_per_slot):
                if units:
                    attend(*units.pop(0))
            if with_proj:
                emit_proj(n, n_blk - n)
        assert not units

    n_slab = d_model // MXU_COLS

    def branch_projections():
        rnn = _dot(from_strand_ref[...], rnn_ref[...]).astype(_BF16)
        return ([_dot(attn_ref[...], w_ao_ref[j]) for j in range(n_slab)],
                [_dot(rnn, w_ro_ref[j]) for j in range(n_slab)])

    def prepare_window_and_conv():
        qk_scale = LOG2E * HEAD_DIM ** -0.5
        if carry:
            kwin_ref[0, :, :, 0:WINDOW, :] = kwin_ref[0, :, :, ls:ls + WINDOW, :]
            vwin_ref[0, :, :, 0:WINDOW, :] = vwin_ref[0, :, :, ls:ls + WINDOW, :]
        for s in range(ns):
            sl = slice(s * ls, (s + 1) * ls)
            kd = _split_heads(k_ref[sl, :], qk_scale)
            vd = _split_heads(v_ref[sl, :])
            for h in range(N_KV_HEADS):
                for half in range(2):
                    kwin_ref[s, h, half, WINDOW:WINDOW + ls, :] = kd[h][half]
                    vwin_ref[s, h, half, WINDOW:WINDOW + ls, :] = vd[h][half]
            if carry:
                kst_ref[0] = k_ref[rows - WINDOW:rows, :]
                vst_ref[0] = v_ref[rows - WINDOW:rows, :]
            else:
                hk = _split_heads(ck_ref[s], qk_scale)
                hv = _split_heads(cv_ref[s])
                pad = KEY_WIN - WINDOW - ls
                for h in range(N_KV_HEADS):
                    for half in range(2):
                        kwin_ref[s, h, half, 0:WINDOW, :] = hk[h][half]
                        vwin_ref[s, h, half, 0:WINDOW, :] = hv[h][half]
                        kwin_ref[s, h, half, WINDOW + ls:KEY_WIN, :] = jnp.zeros((pad, LANES), _BF16)
                        vwin_ref[s, h, half, WINDOW + ls:KEY_WIN, :] = jnp.zeros((pad, LANES), _BF16)
                kst_ref[s, 0:WINDOW - ls, :] = ck_ref[s, ls:WINDOW, :]
                kst_ref[s, WINDOW - ls:WINDOW, :] = k_ref[sl, :]
                vst_ref[s, 0:WINDOW - ls, :] = cv_ref[s, ls:WINDOW, :]
                vst_ref[s, WINDOW - ls:WINDOW, :] = v_ref[sl, :]

            hist = c0_ref[s] if not carry else hist_ref[0] * keep_prep
            first = lax.broadcasted_iota(jnp.int32, (SUBLANES, LANES), 0) == 0
            for col in range(0, d_rnn, LANES):
                lanes = slice(col, col + LANES)
                taps = [jnp.broadcast_to(v1280_ref[r:r + 1, lanes], (SUBLANES, LANES))
                        for r in range(CONV_W + 1)]
                blocks = [xr_ref[s * ls + jb * SUBLANES:s * ls + (jb + 1) * SUBLANES, lanes]
                          for jb in range(steps)]

                def earlier(jb, d):
                    if jb >= d:
                        return blocks[jb - d]
                    row = CONV_W - 1 + jb - d
                    prev_strand = pltpu.roll(blocks[steps + jb - d], 1, axis=0)
                    return jnp.where(first, hist[row:row + 1, lanes], prev_strand)

                for jb in range(steps):
                    acc = taps[ROW_B_CONV] + taps[CONV_W - 1] * blocks[jb]
                    for d in range(1, CONV_W):
                        acc = acc + taps[CONV_W - 1 - d] * earlier(jb, d)
                    xc_ref[s * ls + jb * SUBLANES:s * ls + (jb + 1) * SUBLANES, lanes] = acc
            tail = jnp.concatenate(
                [xr_ref[s * ls + (steps - d + 1) * SUBLANES - 1:s * ls + (steps - d + 1) * SUBLANES, :]
                 for d in range(CONV_W - 1, 0, -1)], axis=0)
            cst_ref[s] = tail
            if carry:
                hist_ref[s] = tail

    def output_projection(ao, ro):
        merged = jnp.concatenate(
            [(sga_ref[:, j * MXU_COLS:(j + 1) * MXU_COLS] * ao[j]
              + sgr_ref[:, j * MXU_COLS:(j + 1) * MXU_COLS] * ro[j]).astype(_BF16)
             for j in range(n_slab)], axis=1)
        return jnp.concatenate([_dot(merged, w_out_ref[j]) for j in range(n_slab)], axis=1)

    def gate_sigmoids():
        sga_ref[...] = _sigmoid(ga_ref[...])
        sgr_ref[...] = _sigmoid(gr_ref[...])

    def residual(mo):
        yn = _rms(mo)
        for s in range(ns):
            sl = slice(s * ls, (s + 1) * ls)
            gain = v1024_ref[1:2, :] * modb_ref[s, MOD_GATE_MIX:MOD_GATE_MIX + 1, :]
            y_ref[sl, :] = xb_ref[sl, :] + gain * yn[sl]

    if skew:
        publish_state()
        norm_input()
        main_loop(with_proj=True)
        ao, ro = branch_projections()
        prepare_window_and_conv()
        mo = output_projection(ao, ro)
        gate_sigmoids()
        residual(mo)
        emit_proj(n_blk, 0)
    else:
        norm_input()
        emit_proj(n_blk, 0)
        prepare_window_and_conv()
        gate_sigmoids()
        main_loop(with_proj=False)
        residual(output_projection(*branch_projections()))
        publish_state()
    assert not proj


def _whole(memory_space=pltpu.VMEM):
    return pl.BlockSpec(memory_space=memory_space)


def _nbytes(shape, dtype):
    return math.prod(shape) * jnp.dtype(dtype).itemsize


def _vmem_limit(resident, windows, scratch, value_tiles, rows, d_model):
    total = sum(_nbytes(a.shape, a.dtype) for a in resident)
    total += 2 * sum(_nbytes(shape, dtype) for shape, dtype in windows)
    total += sum(_nbytes(s.shape, s.dtype) for s in scratch)
    return total + value_tiles * _nbytes((rows, d_model), _F32)


def _mixer_call(x, mod, hist, sinks, v1024, v1280, w_in, w_rg, w_ao, w_ro, w_out, *, ns, ls, skew):
    n_seq, seq_len, d_model = x.shape
    d_rnn = v1280.shape[1]
    carry = hist is None
    rows = ns * ls
    if carry:
        assert ns == 1 and seq_len % ls == 0 and ls % (2 * CHUNK) == 0
        tiles_per_seq = seq_len // ls
    else:
        assert seq_len == ls == CHUNK and n_seq % ns == 0
        tiles_per_seq = 1
    n_tiles = n_seq * seq_len // rows
    lq = min(ls, 2 * CHUNK)
    xf = x.reshape(n_seq * seq_len, d_model)

    if skew:
        tile_a = lambda i: jnp.minimum(i, n_tiles - 1)
        tile_b = lambda i: jnp.maximum(i - 1, 0)
    else:
        tile_a = tile_b = lambda i: i
    seq_a = lambda i: (tile_a(i) // tiles_per_seq, 0, 0)
    seq_b = lambda i: (tile_b(i) // tiles_per_seq, 0, 0)

    in_specs = [pl.BlockSpec((rows, d_model), lambda i: (tile_a(i), 0)),
                pl.BlockSpec((ns, 6, d_model), seq_a)]
    args = [xf, mod]
    if skew:
        in_specs = [in_specs[0], pl.BlockSpec((rows, d_model), lambda i: (tile_b(i), 0)),
                    in_specs[1], pl.BlockSpec((ns, 6, d_model), seq_b)]
        args = [xf, xf, mod, mod]
    if not carry:
        ck, cv, c0, h0 = hist
        in_specs += [pl.BlockSpec((ns, WINDOW, KV_W), seq_a),
                     pl.BlockSpec((ns, WINDOW, KV_W), seq_a),
                     pl.BlockSpec((ns, CONV_W - 1, d_rnn), seq_a),
                     pl.BlockSpec((ns, 1, d_rnn), seq_b)]
        args += [ck, cv, c0, h0]
    in_specs += [_whole(pltpu.SMEM)] + [_whole()] * 7
    args += [sinks, v1024, v1280, w_in, w_rg, w_ao, w_ro, w_out]

    out_shape = (jax.ShapeDtypeStruct((n_seq * seq_len, d_model), _F32),
                 jax.ShapeDtypeStruct((n_seq, WINDOW, KV_W), _F32),
                 jax.ShapeDtypeStruct((n_seq, WINDOW, KV_W), _F32),
                 jax.ShapeDtypeStruct((n_seq, CONV_W - 1, d_rnn), _F32),
                 jax.ShapeDtypeStruct((n_seq, 1, d_rnn), _F32))
    out_specs = (pl.BlockSpec((rows, d_model), lambda i: (tile_b(i), 0)),
                 pl.BlockSpec((ns, WINDOW, KV_W), seq_b),
                 pl.BlockSpec((ns, WINDOW, KV_W), seq_b),
                 pl.BlockSpec((ns, CONV_W - 1, d_rnn), seq_b),
                 pl.BlockSpec((ns, 1, d_rnn), seq_b))
    win_rows = WINDOW + max(ls, 2 * CHUNK)
    scratch = [pltpu.VMEM((rows, d_model), _BF16),
               pltpu.VMEM((rows, d_model), _BF16),
               pltpu.VMEM((rows, N_HEADS * HEAD_DIM), _BF16),
               pltpu.VMEM((rows, KV_W), _F32),
               pltpu.VMEM((rows, KV_W), _F32),
               pltpu.VMEM((rows, d_rnn), _F32),
               pltpu.VMEM((rows, d_rnn), _F32),
               pltpu.VMEM((rows, d_model), _F32),
               pltpu.VMEM((rows, d_model), _F32),
               pltpu.VMEM((rows, d_rnn), _F32),
               pltpu.VMEM((rows, d_model), _F32),
               pltpu.VMEM((rows, d_model), _F32),
               pltpu.VMEM((ns, N_KV_HEADS, 2, win_rows, LANES), _BF16),
               pltpu.VMEM((ns, N_KV_HEADS, 2, win_rows, LANES), _BF16),
               pltpu.VMEM((ns, WINDOW, KV_W), _F32),
               pltpu.VMEM((ns, WINDOW, KV_W), _F32),
               pltpu.VMEM((ns, CONV_W - 1, d_rnn), _F32),
               pltpu.VMEM((N_KV_HEADS, 2, 2 * lq, KEY_WIN), _F32),
               pltpu.VMEM((ns, CONV_W - 1, d_rnn), _F32),
               pltpu.VMEM((1, d_rnn), _F32),
               pltpu.VMEM((rows, N_HEADS * HEAD_DIM), _BF16),
               pltpu.VMEM((rows, d_rnn), _BF16),
               pltpu.VMEM((rows, rows), _BF16),
               pltpu.VMEM((rows, rows), _BF16)]
    kern = functools.partial(_mixer_kernel, skew, carry, ns, ls, tiles_per_seq, d_model, d_rnn)
    windows = [(spec.block_shape, _F32) for spec in (*in_specs, *out_specs)
               if spec.block_shape is not None]
    vmem_limit = _vmem_limit((v1024, v1280, w_in, w_rg, w_ao, w_ro, w_out), windows, scratch,
                             MIXER_VALUE_TILES, rows, d_model)
    return pl.pallas_call(
        kern,
        grid=(n_tiles + (1 if skew else 0),),
        in_specs=in_specs,
        out_specs=out_specs,
        out_shape=out_shape,
        scratch_shapes=scratch,
        compiler_params=pltpu.CompilerParams(dimension_semantics=("arbitrary",),
                                             vmem_limit_bytes=vmem_limit),
        name="mixer_stream" if carry else "mixer_step",
    )(*args)


def _ffn_kernel(ns, ls, x_ref, mod_ref, v1024_ref, wg_ref, wu_ref, wd_ref, y_ref, u_ref):
    x = x_ref[...]
    xn = _rms(x)
    for s in range(ns):
        sl = slice(s * ls, (s + 1) * ls)
        gain = v1024_ref[0:1, :] * (1.0 + mod_ref[s, MOD_SCALE_FFN:MOD_SCALE_FFN + 1, :])
        u_ref[sl, :] = (xn[sl] * gain + mod_ref[s, MOD_SHIFT_FFN:MOD_SHIFT_FFN + 1, :]).astype(_BF16)
    u = u_ref[...]
    g = _dot(u, wg_ref[...])
    up = _dot(u, wu_ref[...])
    hmid = (_silu(g) * up).astype(_BF16)
    yn = _rms(_dot(hmid, wd_ref[...]))
    for s in range(ns):
        sl = slice(s * ls, (s + 1) * ls)
        gain = v1024_ref[1:2, :] * mod_ref[s, MOD_GATE_FFN:MOD_GATE_FFN + 1, :]
        y_ref[sl, :] = x[sl] + gain * yn[sl]


def _ffn_call(xf, mod, v1024, wg, wu, wd, *, ns, ls, seq_len):
    n_rows, d_model = xf.shape
    rows = ns * ls
    if ns == 1:
        tiles_per_seq = seq_len // ls
        seq_of = lambda i: (i // tiles_per_seq, 0, 0)
    else:
        assert ls == seq_len
        seq_of = lambda i: (i, 0, 0)
    scratch = [pltpu.VMEM((rows, d_model), _BF16)]
    windows = [((rows, d_model), _F32), ((ns, 6, d_model), _F32), ((rows, d_model), _F32)]
    value_tiles = FFN_VALUE_TILES + FFN_WIDE_VALUES * pl.cdiv(wg.shape[1], d_model)
    return pl.pallas_call(
        functools.partial(_ffn_kernel, ns, ls),
        grid=(n_rows // rows,),
        in_specs=[pl.BlockSpec((rows, d_model), lambda i: (i, 0)),
                  pl.BlockSpec((ns, 6, d_model), seq_of),
                  _whole(), _whole(), _whole(), _whole()],
        out_specs=pl.BlockSpec((rows, d_model), lambda i: (i, 0)),
        out_shape=jax.ShapeDtypeStruct((n_rows, d_model), _F32),
        scratch_shapes=scratch,
        compiler_params=pltpu.CompilerParams(
            dimension_semantics=("arbitrary",),
            vmem_limit_bytes=_vmem_limit((v1024, wg, wu, wd), windows, scratch, value_tiles,
                                         rows, d_model)),
        name="ffn",
    )(xf, mod, v1024, wg, wu, wd)


MIX_TILE = 256
MIX_SEQS = 4
SKEW_PROMPT = True
SKEW_SAMPLE = False
FFN_TILE = 512
FFN_SEQS = 8


def kernel(x_prompt, x_sample, c_prompt, c_sample, cache_k, cache_v, state_conv, state_h, w_ada, b_ada, g_pre_mix, g_post_mix, w_in, attn_sinks, w_conv, b_conv, w_rg_a, b_rg_a, w_rg_x, b_rg_x, rg_lambda, w_attn_o, w_rnn_o, w_out, g_pre_ffn, g_post_ffn, w_ffn_gate, w_ffn_up, w_ffn_down):
    depth = w_in.shape[0]
    assert depth == 1
    bp, sp, d_model = x_prompt.shape
    bs, ss, _ = x_sample.shape
    d_rnn = w_conv.shape[-1]
    l = 0

    n_c = bp + bs
    pad = (-n_c) % 16
    c_all = jnp.concatenate([c_prompt, c_sample, jnp.zeros((pad, d_model), _F32)], axis=0)
    mod = _ada_call(c_all, w_ada[l], b_ada[l])[:n_c].reshape(n_c, 6, d_model)
    mod_p, mod_s = mod[:bp], mod[bp:]

    bf = lambda w: w.astype(_BF16)
    v1024_mix = jnp.stack([g_pre_mix[l], g_post_mix[l]])
    v1024_ffn = jnp.stack([g_pre_ffn[l], g_post_ffn[l]])
    v1280 = jnp.concatenate([w_conv[l], b_conv[l][None], b_rg_a[l][None], b_rg_x[l][None],
                             rg_lambda[l][None]], axis=0)
    w_rg = bf(jnp.concatenate([w_rg_a[l], w_rg_x[l]], axis=-1))
    mix_w = (attn_sinks[l], v1024_mix, v1280, _slabs(w_in[l]), w_rg, _slabs(w_attn_o[l]),
             _slabs(w_rnn_o[l]), _slabs(w_out[l]))
    ffn_w = (v1024_ffn, bf(w_ffn_gate[l]), bf(w_ffn_up[l]), bf(w_ffn_down[l]))

    xp1, kp, vp, cp, hp = _mixer_call(x_prompt, mod_p, None, *mix_w, ns=1, ls=MIX_TILE,
                                      skew=SKEW_PROMPT)
    hist = (cache_k[l].reshape(bs, WINDOW, KV_W), cache_v[l].reshape(bs, WINDOW, KV_W),
            state_conv[l], state_h[l].reshape(bs, 1, d_rnn))
    xs1, ks, vs, cs, hs = _mixer_call(x_sample, mod_s, hist, *mix_w, ns=MIX_SEQS, ls=ss,
                                      skew=SKEW_SAMPLE)

    yp = _ffn_call(xp1, mod_p, *ffn_w, ns=1, ls=FFN_TILE, seq_len=sp)
    ys = _ffn_call(xs1, mod_s, *ffn_w, ns=FFN_SEQS, ls=ss, seq_len=ss)

    kv_shape = lambda n: (1, n, WINDOW, N_KV_HEADS, HEAD_DIM)
    return (yp.reshape(bp, sp, d_model), ys.reshape(bs, ss, d_model),
            kp.reshape(kv_shape(bp)), vp.reshape(kv_shape(bp)), cp[None], hp.reshape(1, bp, d_rnn),
            ks.reshape(kv_shape(bs)), vs.reshape(kv_shape(bs)), cs[None], hs.reshape(1, bs, d_rnn))
```

```python
import functools
import math

import jax
import jax.numpy as jnp
from jax import lax
from jax.experimental import pallas as pl
from jax.experimental.pallas import tpu as pltpu

CHUNK = 64
N_HEADS = 16
N_KV_HEADS = 4
HEAD_DIM = 64
GROUP = N_HEADS // N_KV_HEADS
WINDOW = 128
N_BACK = WINDOW // CHUNK
KV_W = N_KV_HEADS * HEAD_DIM
RNN_BLOCK = 128
CONV_W = 4
LRU_C = 8.0
EPS = 1e-6
NEG_INF = -1e30

LANES = 128
SUBLANES = 8
MXU_COLS = 256
SLABS_PER_CAST_STEP = 4
KEY_WIN = WINDOW + 2 * CHUNK
MIXER_VALUE_TILES = 14
FFN_VALUE_TILES = 2
FFN_WIDE_VALUES = 3
LOG2E = math.log2(math.e)
ROW_B_CONV, ROW_B_A, ROW_B_X, ROW_LAMBDA = CONV_W, CONV_W + 1, CONV_W + 2, CONV_W + 3
MOD_SHIFT_MIX, MOD_SCALE_MIX, MOD_GATE_MIX, MOD_SHIFT_FFN, MOD_SCALE_FFN, MOD_GATE_FFN = range(6)

_F32 = jnp.float32
_BF16 = jnp.bfloat16


def _dot(a, b):
    return lax.dot_general(a, b, (((1,), (0,)), ((), ())), preferred_element_type=_F32)


def _dot_nt(a, b):
    return lax.dot_general(a, b, (((1,), (1,)), ((), ())), preferred_element_type=_F32)


def _rms(x):
    return x * lax.rsqrt(jnp.mean(x * x, axis=-1, keepdims=True) + EPS)


def _softplus(x):
    return jnp.maximum(x, 0.0) + jnp.log1p(jnp.exp(-jnp.abs(x)))


def _sigmoid(x):
    return 0.5 * jnp.tanh(0.5 * x) + 0.5


def _silu(x):
    h = 0.5 * x
    return h * jnp.tanh(h) + h


def _sqrt_nonneg(x):
    return jnp.exp2((0.5 * LOG2E) * jnp.log(x))


def _neg_expm1_twice(x):
    t = jnp.tanh(x)
    return (-2.0 * t) / (1.0 - t)


def _slab_cast_kernel(n_slabs, w_ref, o_ref):
    for j in range(n_slabs):
        o_ref[j] = w_ref[:, j * MXU_COLS:(j + 1) * MXU_COLS].astype(_BF16)


def _slabs(w):
    k, n = w.shape
    total = n // MXU_COLS
    per_step = math.gcd(total, SLABS_PER_CAST_STEP)
    return pl.pallas_call(
        functools.partial(_slab_cast_kernel, per_step),
        grid=(total // per_step,),
        in_specs=[pl.BlockSpec((k, per_step * MXU_COLS), lambda j: (0, j))],
        out_specs=pl.BlockSpec((per_step, k, MXU_COLS), lambda j: (j, 0, 0)),
        out_shape=jax.ShapeDtypeStruct((total, k, MXU_COLS), _BF16),
        compiler_params=pltpu.CompilerParams(dimension_semantics=("arbitrary",)),
        name="slab_cast",
    )(w)


def _gelu_tanh(x):
    c = math.sqrt(2.0 / math.pi)
    hx = 0.5 * x
    return hx + hx * jnp.tanh(x * (c + (c * 0.044715) * (x * x)))


def _split_heads(kv, scale=None):
    n = kv.shape[0]
    lo = lax.broadcasted_iota(jnp.int32, (n, LANES), 1) < HEAD_DIM
    outs = []
    for blk in range(KV_W // LANES):
        b = kv[:, blk * LANES:(blk + 1) * LANES]
        if scale is not None:
            b = b * scale
        r = pltpu.roll(b, HEAD_DIM, axis=1)
        outs.append((jnp.where(lo, b, 0.0).astype(_BF16), jnp.where(lo, 0.0, r).astype(_BF16)))
        outs.append((jnp.where(lo, r, 0.0).astype(_BF16), jnp.where(lo, 0.0, b).astype(_BF16)))
    return outs


def _ada_kernel(c_ref, w_ref, b_ref, o_ref):
    c = c_ref[...]
    a = _silu(c).astype(_BF16)
    o_ref[...] = _dot(a, w_ref[...].astype(_BF16)) + b_ref[...]


def _ada_call(c, w_ada, b_ada):
    rows, d = c.shape
    n = w_ada.shape[1]
    bn = d
    return pl.pallas_call(
        _ada_kernel,
        grid=(n // bn,),
        in_specs=[
            pl.BlockSpec((rows, d), lambda j: (0, 0)),
            pl.BlockSpec((d, bn), lambda j: (0, j)),
            pl.BlockSpec((1, bn), lambda j: (0, j)),
        ],
        out_specs=pl.BlockSpec((rows, bn), lambda j: (0, j)),
        out_shape=jax.ShapeDtypeStruct((rows, n), _F32),
        compiler_params=pltpu.CompilerParams(dimension_semantics=("arbitrary",)),
        name="ada",
    )(c, w_ada, b_ada.reshape(1, n))


def _mixer_kernel(skew, carry, ns, ls, tiles_per_seq, d_model, d_rnn, *refs):
    if skew:
        xa_ref, xb_ref, moda_ref, modb_ref = refs[:4]
        refs = refs[4:]
    else:
        xa_ref, moda_ref = xb_ref, modb_ref = refs[:2]
        refs = refs[2:]
    n_in = 0 if carry else 4
    if not carry:
        ck_ref, cv_ref, c0_ref, h0_ref = refs[:4]
    (sinks_ref, v1024_ref, v1280_ref, w_in_ref, w_rg_ref, w_ao_ref, w_ro_ref, w_out_ref,
     y_ref, ks_ref, vs_ref, cs_ref, hs_ref,
     u_ref, us_ref, q_ref, k_ref, v_ref, xr_ref, yr_ref, ga_ref, gr_ref, xc_ref, sga_ref, sgr_ref,
     kwin_ref, vwin_ref, kst_ref, vst_ref, cst_ref, bias_ref, hist_ref, hcar_ref,
     attn_ref, rnn_ref, to_strand_ref, from_strand_ref) = refs[n_in:]

    i = pl.program_id(0)
    rows = ns * ls
    lq = min(ls, 2 * CHUNK)
    n_win = ls // lq
    q_w = N_HEADS * HEAD_DIM
    n_blk = d_rnn // RNN_BLOCK
    steps = ls // SUBLANES
    if carry:
        mixed = i + tiles_per_seq - 1 if skew else i
        keep_mix = jnp.where(lax.rem(mixed, tiles_per_seq) == 0, 0.0, 1.0)
        keep_prep = jnp.where(lax.rem(i, tiles_per_seq) == 0, 0.0, 1.0)

    @pl.when(i == 0)
    def _():
        for ref in (q_ref, yr_ref, xc_ref, sga_ref, sgr_ref, kwin_ref, vwin_ref, kst_ref,
                    vst_ref, cst_ref, hist_ref, hcar_ref):
            ref[...] = jnp.zeros(ref.shape, ref.dtype)
        r = lax.broadcasted_iota(jnp.int32, (rows, rows), 0)
        c = lax.broadcasted_iota(jnp.int32, (rows, rows), 1)
        seg, rr = (r // ls) * ls, r % ls
        to_strand_ref[...] = jnp.where(
            c == seg + (rr % SUBLANES) * steps + rr // SUBLANES, 1.0, 0.0).astype(_BF16)
        from_strand_ref[...] = jnp.where(
            c == seg + (rr % steps) * SUBLANES + rr // steps, 1.0, 0.0).astype(_BF16)
        t = lax.broadcasted_iota(jnp.int32, (lq, KEY_WIN), 0)
        j = lax.broadcasted_iota(jnp.int32, (lq, KEY_WIN), 1)
        dist = jnp.abs(t + WINDOW - j).astype(_F32)
        qc = t // CHUNK
        kc = j // CHUNK
        base = jnp.where(kc >= qc, jnp.where(kc <= qc + N_BACK, 0.0, NEG_INF), NEG_INF)
        for h in range(N_KV_HEADS):
            for g in range(GROUP):
                slope = 2.0 ** (-8.0 * (h * GROUP + g + 1) / N_HEADS)
                bias_ref[h, g % 2, (g // 2) * lq:(g // 2 + 1) * lq, :] = LOG2E * (base - slope * dist)

    def publish_state():
        ks_ref[...] = kst_ref[...]
        vs_ref[...] = vst_ref[...]
        cs_ref[...] = cst_ref[...]

    def norm_input():
        xn = _rms(xa_ref[...])
        for s in range(ns):
            sl = slice(s * ls, (s + 1) * ls)
            gain = v1024_ref[0:1, :] * (1.0 + moda_ref[s, MOD_SCALE_MIX:MOD_SCALE_MIX + 1, :])
            u_ref[sl, :] = (xn[sl] * gain + moda_ref[s, MOD_SHIFT_MIX:MOD_SHIFT_MIX + 1, :]).astype(_BF16)
        us_ref[...] = _dot(to_strand_ref[...], u_ref[...]).astype(_BF16)

    proj = []
    off = 0
    for src, ref, width, free_after in (
            (u_ref, q_ref, q_w, lambda c: 2 * (c // MXU_COLS) + 1),
            (u_ref, k_ref, KV_W, lambda c: -1), (u_ref, v_ref, KV_W, lambda c: -1),
            (us_ref, xr_ref, d_rnn, lambda c: -1),
            (us_ref, yr_ref, d_rnn, lambda c: (c + MXU_COLS - 1) // RNN_BLOCK),
            (u_ref, ga_ref, d_model, lambda c: -1), (u_ref, gr_ref, d_model, lambda c: -1)):
        for c in range(0, width, MXU_COLS):
            proj.append((free_after(c), src, ref, c, off + c))
        off += width
    proj.sort(key=lambda p: p[0])

    def emit_proj(slot, slots_left):
        ready = [p for p in proj if p[0] <= slot]
        quota = -(-len(proj) // slots_left) if slots_left else len(proj)
        for p in ready[:quota]:
            proj.remove(p)
            _, src, ref, c, wc = p
            ref[:, c:c + MXU_COLS] = _dot(src[...], w_in_ref[wc // MXU_COLS]).astype(ref.dtype)

    lo_sel = lax.broadcasted_iota(jnp.int32, (lq, LANES), 1) < HEAD_DIM
    if carry:
        kcol = lax.broadcasted_iota(jnp.int32, (1, KEY_WIN), 1)
        hist_bias = jnp.where(kcol < WINDOW, NEG_INF, 0.0).astype(_F32) * (1.0 - keep_mix)

    def attend_scores(s, w, h):
        r0 = s * ls + w * lq
        kw = slice(w * lq, w * lq + KEY_WIN)
        c0 = h * GROUP * HEAD_DIM
        qab = jnp.concatenate([q_ref[r0:r0 + lq, c0:c0 + LANES],
                               q_ref[r0:r0 + lq, c0 + LANES:c0 + 2 * LANES]], axis=0)
        ps, inv = [[], []], [None] * GROUP
        for half in range(2):
            sc = _dot_nt(qab, kwin_ref[s, h, half, kw, :]) + bias_ref[h, half]
            if carry and w == 0:
                sc = sc + hist_bias
            for pair in range(2):
                g = 2 * pair + half
                sg = sc[pair * lq:(pair + 1) * lq]
                sink = LOG2E * sinks_ref[h * GROUP + g]
                mg = jnp.maximum(jnp.max(sg, axis=-1, keepdims=True), sink)
                pg = jnp.exp2(sg - mg)
                inv[g] = 1.0 / (jnp.sum(pg, axis=-1, keepdims=True) + jnp.exp2(sink - mg))
                ps[half].append(pg.astype(_BF16))
        return s, h, r0, kw, c0, ps, inv

    def attend_values(s, h, r0, kw, c0, ps, inv):
        o = (_dot(jnp.concatenate(ps[0], axis=0), vwin_ref[s, h, 0, kw, :])
             + _dot(jnp.concatenate(ps[1], axis=0), vwin_ref[s, h, 1, kw, :]))
        for pair in range(2):
            norm = jnp.where(lo_sel, inv[2 * pair], inv[2 * pair + 1])
            attn_ref[r0:r0 + lq, c0 + pair * LANES:c0 + (pair + 1) * LANES] = (
                o[pair * lq:(pair + 1) * lq] * norm).astype(_BF16)

    units = [(s, w, h) for h in range(N_KV_HEADS) for s in range(ns) for w in range(n_win)]
    units_per_slot = len(units) // (2 * N_KV_HEADS)

    sub = lax.broadcasted_iota(jnp.int32, (SUBLANES, RNN_BLOCK), 0)

    def recur_gates(n):
        half_xc = 0.5 * xc_ref[:, n * RNN_BLOCK:(n + 1) * RNN_BLOCK]
        return half_xc, _dot(half_xc.astype(_BF16), w_rg_ref[n])

    def recur(n, half_xc, half_gates):
        cs_ = slice(n * RNN_BLOCK, (n + 1) * RNN_BLOCK)
        half_c = (-0.5 * LRU_C) * _softplus(-v1280_ref[ROW_LAMBDA:ROW_LAMBDA + 1, cs_])
        half_ba = 0.5 * v1280_ref[ROW_B_A:ROW_B_A + 1, cs_]
        half_bx = 0.5 * v1280_ref[ROW_B_X:ROW_B_X + 1, cs_]
        log_a = half_c * jnp.tanh(half_gates[:, :RNN_BLOCK] + half_ba) + half_c
        gated_x = jnp.tanh(half_gates[:, RNN_BLOCK:] + half_bx) * half_xc + half_xc
        a = jnp.exp(log_a)
        b = _sqrt_nonneg(_neg_expm1_twice(log_a)) * gated_x
        hs = []
        for s in range(ns):
            sl = slice(s * ls, (s + 1) * ls)
            a3 = a[sl].reshape(steps, SUBLANES, RNN_BLOCK)
            b3 = b[sl].reshape(steps, SUBLANES, RNN_BLOCK)
            hz, ap = [b3[0]], [a3[0]]
            for jb in range(1, steps):
                hz.append(a3[jb] * hz[-1] + b3[jb])
                ap.append(a3[jb] * ap[-1])
            ae, he = ap[-1], hz[-1]
            d = 1
            while d < SUBLANES:
                keep = sub >= d
                he = he + ae * jnp.where(keep, pltpu.roll(he, d, axis=0), 0.0)
                ae = ae * jnp.where(keep, pltpu.roll(ae, d, axis=0), 1.0)
                d *= 2
            hprev = hcar_ref[0:1, cs_] * keep_mix if carry else h0_ref[s, 0:1, cs_]
            after = he + ae * hprev
            before = jnp.where(sub == 0, hprev, pltpu.roll(after, 1, axis=0))
            hs.extend(hz[jb] + ap[jb] * before for jb in range(steps))
            hlast = after[SUBLANES - 1:SUBLANES, :]
            hs_ref[s, 0:1, cs_] = hlast
            if carry:
                hcar_ref[0:1, cs_] = hlast
        hfull = jnp.concatenate(hs, axis=0)
        rnn_ref[:, cs_] = (hfull * _gelu_tanh(yr_ref[:, cs_])).astype(_BF16)

    def next_scores():
        return [attend_scores(*units.pop(0)) for _ in range(min(units_per_slot, len(units)))]

    def main_loop(with_proj):
        scored = next_scores()
        gates = recur_gates(0)
        for n in range(n_blk):
            scored_next = next_scores()
            if with_proj:
                emit_proj(n - 1, n_blk - n)
            gates_next = recur_gates(n + 1) if n + 1 < n_blk else None
            recur(n, *gates)
            for unit in scored:
                attend_values(*unit)
            scored, gates = scored_next, gates_next
        for unit in scored:
            attend_values(*unit)
        if with_proj:
            emit_proj(n_blk - 1, 1)
        assert not units

    n_slab = d_model // MXU_COLS

    def branch_projections():
        rnn = _dot(from_strand_ref[...], rnn_ref[...]).astype(_BF16)
        return ([_dot(attn_ref[...], w_ao_ref[j]) for j in range(n_slab)],
                [_dot(rnn, w_ro_ref[j]) for j in range(n_slab)])

    def prepare_window_and_conv():
        qk_scale = LOG2E * HEAD_DIM ** -0.5
        if carry:
            kwin_ref[0, :, :, 0:WINDOW, :] = kwin_ref[0, :, :, ls:ls + WINDOW, :]
            vwin_ref[0, :, :, 0:WINDOW, :] = vwin_ref[0, :, :, ls:ls + WINDOW, :]
        for s in range(ns):
            sl = slice(s * ls, (s + 1) * ls)
            kd = _split_heads(k_ref[sl, :], qk_scale)
            vd = _split_heads(v_ref[sl, :])
            for h in range(N_KV_HEADS):
                for half in range(2):
                    kwin_ref[s, h, half, WINDOW:WINDOW + ls, :] = kd[h][half]
                    vwin_ref[s, h, half, WINDOW:WINDOW + ls, :] = vd[h][half]
            if carry:
                kst_ref[0] = k_ref[rows - WINDOW:rows, :]
                vst_ref[0] = v_ref[rows - WINDOW:rows, :]
            else:
                hk = _split_heads(ck_ref[s], qk_scale)
                hv = _split_heads(cv_ref[s])
                pad = KEY_WIN - WINDOW - ls
                for h in range(N_KV_HEADS):
                    for half in range(2):
                        kwin_ref[s, h, half, 0:WINDOW, :] = hk[h][half]
                        vwin_ref[s, h, half, 0:WINDOW, :] = hv[h][half]
                        kwin_ref[s, h, half, WINDOW + ls:KEY_WIN, :] = jnp.zeros((pad, LANES), _BF16)
                        vwin_ref[s, h, half, WINDOW + ls:KEY_WIN, :] = jnp.zeros((pad, LANES), _BF16)
                kst_ref[s, 0:WINDOW - ls, :] = ck_ref[s, ls:WINDOW, :]
                kst_ref[s, WINDOW - ls:WINDOW, :] = k_ref[sl, :]
                vst_ref[s, 0:WINDOW - ls, :] = cv_ref[s, ls:WINDOW, :]
                vst_ref[s, WINDOW - ls:WINDOW, :] = v_ref[sl, :]

            hist = c0_ref[s] if not carry else hist_ref[0] * keep_prep
            first = lax.broadcasted_iota(jnp.int32, (SUBLANES, LANES), 0) == 0
            for col in range(0, d_rnn, LANES):
                lanes = slice(col, col + LANES)
                taps = [jnp.broadcast_to(v1280_ref[r:r + 1, lanes], (SUBLANES, LANES))
                        for r in range(CONV_W + 1)]
                blocks = [xr_ref[s * ls + jb * SUBLANES:s * ls + (jb + 1) * SUBLANES, lanes]
                          for jb in range(steps)]

                def earlier(jb, d):
                    if jb >= d:
                        return blocks[jb - d]
                    row = CONV_W - 1 + jb - d
                    prev_strand = pltpu.roll(blocks[steps + jb - d], 1, axis=0)
                    return jnp.where(first, hist[row:row + 1, lanes], prev_strand)

                for jb in range(steps):
                    acc = taps[ROW_B_CONV] + taps[CONV_W - 1] * blocks[jb]
                    for d in range(1, CONV_W):
                        acc = acc + taps[CONV_W - 1 - d] * earlier(jb, d)
                    xc_ref[s * ls + jb * SUBLANES:s * ls + (jb + 1) * SUBLANES, lanes] = acc
            tail = jnp.concatenate(
                [xr_ref[s * ls + (steps - d + 1) * SUBLANES - 1:s * ls + (steps - d + 1) * SUBLANES, :]
                 for d in range(CONV_W - 1, 0, -1)], axis=0)
            cst_ref[s] = tail
            if carry:
                hist_ref[s] = tail

    def output_projection(ao, ro):
        merged = jnp.concatenate(
            [(sga_ref[:, j * MXU_COLS:(j + 1) * MXU_COLS] * ao[j]
              + sgr_ref[:, j * MXU_COLS:(j + 1) * MXU_COLS] * ro[j]).astype(_BF16)
             for j in range(n_slab)], axis=1)
        return jnp.concatenate([_dot(merged, w_out_ref[j]) for j in range(n_slab)], axis=1)

    def gate_sigmoids():
        sga_ref[...] = _sigmoid(ga_ref[...])
        sgr_ref[...] = _sigmoid(gr_ref[...])

    def residual(mo):
        yn = _rms(mo)
        for s in range(ns):
            sl = slice(s * ls, (s + 1) * ls)
            gain = v1024_ref[1:2, :] * modb_ref[s, MOD_GATE_MIX:MOD_GATE_MIX + 1, :]
            y_ref[sl, :] = xb_ref[sl, :] + gain * yn[sl]

    if skew:
        publish_state()
        norm_input()
        main_loop(with_proj=True)
        ao, ro = branch_projections()
        prepare_window_and_conv()
        mo = output_projection(ao, ro)
        gate_sigmoids()
        residual(mo)
        emit_proj(n_blk, 0)
    else:
        norm_input()
        emit_proj(n_blk, 0)
        prepare_window_and_conv()
        gate_sigmoids()
        main_loop(with_proj=False)
        residual(output_projection(*branch_projections()))
        publish_state()
    assert not proj


def _whole(memory_space=pltpu.VMEM):
    return pl.BlockSpec(memory_space=memory_space)


def _nbytes(shape, dtype):
    return math.prod(shape) * jnp.dtype(dtype).itemsize


def _vmem_limit(resident, windows, scratch, value_tiles, rows, d_model):
    total = sum(_nbytes(a.shape, a.dtype) for a in resident)
    total += 2 * sum(_nbytes(shape, dtype) for shape, dtype in windows)
    total += sum(_nbytes(s.shape, s.dtype) for s in scratch)
    return total + value_tiles * _nbytes((rows, d_model), _F32)


def _mixer_call(x, mod, hist, sinks, v1024, v1280, w_in, w_rg, w_ao, w_ro, w_out, *, ns, ls, skew):
    n_seq, seq_len, d_model = x.shape
    d_rnn = v1280.shape[1]
    carry = hist is None
    rows = ns * ls
    if carry:
        assert ns == 1 and seq_len % ls == 0 and ls % (2 * CHUNK) == 0
        tiles_per_seq = seq_len // ls
    else:
        assert seq_len == ls == CHUNK and n_seq % ns == 0
        tiles_per_seq = 1
    n_tiles = n_seq * seq_len // rows
    lq = min(ls, 2 * CHUNK)
    xf = x.reshape(n_seq * seq_len, d_model)

    if skew:
        tile_a = lambda i: jnp.minimum(i, n_tiles - 1)
        tile_b = lambda i: jnp.maximum(i - 1, 0)
    else:
        tile_a = tile_b = lambda i: i
    seq_a = lambda i: (tile_a(i) // tiles_per_seq, 0, 0)
    seq_b = lambda i: (tile_b(i) // tiles_per_seq, 0, 0)

    in_specs = [pl.BlockSpec((rows, d_model), lambda i: (tile_a(i), 0)),
                pl.BlockSpec((ns, 6, d_model), seq_a)]
    args = [xf, mod]
    if skew:
        in_specs = [in_specs[0], pl.BlockSpec((rows, d_model), lambda i: (tile_b(i), 0)),
                    in_specs[1], pl.BlockSpec((ns, 6, d_model), seq_b)]
        args = [xf, xf, mod, mod]
    if not carry:
        ck, cv, c0, h0 = hist
        in_specs += [pl.BlockSpec((ns, WINDOW, KV_W), seq_a),
                     pl.BlockSpec((ns, WINDOW, KV_W), seq_a),
                     pl.BlockSpec((ns, CONV_W - 1, d_rnn), seq_a),
                     pl.BlockSpec((ns, 1, d_rnn), seq_b)]
        args += [ck, cv, c0, h0]
    in_specs += [_whole(pltpu.SMEM)] + [_whole()] * 7
    args += [sinks, v1024, v1280, w_in, w_rg, w_ao, w_ro, w_out]

    out_shape = (jax.ShapeDtypeStruct((n_seq * seq_len, d_model), _F32),
                 jax.ShapeDtypeStruct((n_seq, WINDOW, KV_W), _F32),
                 jax.ShapeDtypeStruct((n_seq, WINDOW, KV_W), _F32),
                 jax.ShapeDtypeStruct((n_seq, CONV_W - 1, d_rnn), _F32),
                 jax.ShapeDtypeStruct((n_seq, 1, d_rnn), _F32))
    out_specs = (pl.BlockSpec((rows, d_model), lambda i: (tile_b(i), 0)),
                 pl.BlockSpec((ns, WINDOW, KV_W), seq_b),
                 pl.BlockSpec((ns, WINDOW, KV_W), seq_b),
                 pl.BlockSpec((ns, CONV_W - 1, d_rnn), seq_b),
                 pl.BlockSpec((ns, 1, d_rnn), seq_b))
    win_rows = WINDOW + max(ls, 2 * CHUNK)
    scratch = [pltpu.VMEM((rows, d_model), _BF16),
               pltpu.VMEM((rows, d_model), _BF16),
               pltpu.VMEM((rows, N_HEADS * HEAD_DIM), _BF16),
               pltpu.VMEM((rows, KV_W), _F32),
               pltpu.VMEM((rows, KV_W), _F32),
               pltpu.VMEM((rows, d_rnn), _F32),
               pltpu.VMEM((rows, d_rnn), _F32),
               pltpu.VMEM((rows, d_model), _F32),
               pltpu.VMEM((rows, d_model), _F32),
               pltpu.VMEM((rows, d_rnn), _F32),
               pltpu.VMEM((rows, d_model), _F32),
               pltpu.VMEM((rows, d_model), _F32),
               pltpu.VMEM((ns, N_KV_HEADS, 2, win_rows, LANES), _BF16),
               pltpu.VMEM((ns, N_KV_HEADS, 2, win_rows, LANES), _BF16),
               pltpu.VMEM((ns, WINDOW, KV_W), _F32),
               pltpu.VMEM((ns, WINDOW, KV_W), _F32),
               pltpu.VMEM((ns, CONV_W - 1, d_rnn), _F32),
               pltpu.VMEM((N_KV_HEADS, 2, 2 * lq, KEY_WIN), _F32),
               pltpu.VMEM((ns, CONV_W - 1, d_rnn), _F32),
               pltpu.VMEM((1, d_rnn), _F32),
               pltpu.VMEM((rows, N_HEADS * HEAD_DIM), _BF16),
               pltpu.VMEM((rows, d_rnn), _BF16),
               pltpu.VMEM((rows, rows), _BF16),
               pltpu.VMEM((rows, rows), _BF16)]
    kern = functools.partial(_mixer_kernel, skew, carry, ns, ls, tiles_per_seq, d_model, d_rnn)
    windows = [(spec.block_shape, _F32) for spec in (*in_specs, *out_specs)
               if spec.block_shape is not None]
    vmem_limit = _vmem_limit((v1024, v1280, w_in, w_rg, w_ao, w_ro, w_out), windows, scratch,
                             MIXER_VALUE_TILES, rows, d_model)
    return pl.pallas_call(
        kern,
        grid=(n_tiles + (1 if skew else 0),),
        in_specs=in_specs,
        out_specs=out_specs,
        out_shape=out_shape,
        scratch_shapes=scratch,
        compiler_params=pltpu.CompilerParams(dimension_semantics=("arbitrary",),
                                             vmem_limit_bytes=vmem_limit),
        name="mixer_stream" if carry else "mixer_step",
    )(*args)


def _ffn_kernel(ns, ls, x_ref, mod_ref, v1024_ref, wg_ref, wu_ref, wd_ref, y_ref, u_ref):
    x = x_ref[...]
    xn = _rms(x)
    for s in range(ns):
        sl = slice(s * ls, (s + 1) * ls)
        gain = v1024_ref[0:1, :] * (1.0 + mod_ref[s, MOD_SCALE_FFN:MOD_SCALE_FFN + 1, :])
        u_ref[sl, :] = (xn[sl] * gain + mod_ref[s, MOD_SHIFT_FFN:MOD_SHIFT_FFN + 1, :]).astype(_BF16)
    u = u_ref[...]
    g = _dot(u, wg_ref[...])
    up = _dot(u, wu_ref[...])
    hmid = (_silu(g) * up).astype(_BF16)
    yn = _rms(_dot(hmid, wd_ref[...]))
    for s in range(ns):
        sl = slice(s * ls, (s + 1) * ls)
        gain = v1024_ref[1:2, :] * mod_ref[s, MOD_GATE_FFN:MOD_GATE_FFN + 1, :]
        y_ref[sl, :] = x[sl] + gain * yn[sl]


def _ffn_call(xf, mod, v1024, wg, wu, wd, *, ns, ls, seq_len):
    n_rows, d_model = xf.shape
    rows = ns * ls
    if ns == 1:
        tiles_per_seq = seq_len // ls
        seq_of = lambda i: (i // tiles_per_seq, 0, 0)
    else:
        assert ls == seq_len
        seq_of = lambda i: (i, 0, 0)
    scratch = [pltpu.VMEM((rows, d_model), _BF16)]
    windows = [((rows, d_model), _F32), ((ns, 6, d_model), _F32), ((rows, d_model), _F32)]
    value_tiles = FFN_VALUE_TILES + FFN_WIDE_VALUES * pl.cdiv(wg.shape[1], d_model)
    return pl.pallas_call(
        functools.partial(_ffn_kernel, ns, ls),
        grid=(n_rows // rows,),
        in_specs=[pl.BlockSpec((rows, d_model), lambda i: (i, 0)),
                  pl.BlockSpec((ns, 6, d_model), seq_of),
                  _whole(), _whole(), _whole(), _whole()],
        out_specs=pl.BlockSpec((rows, d_model), lambda i: (i, 0)),
        out_shape=jax.ShapeDtypeStruct((n_rows, d_model), _F32),
        scratch_shapes=scratch,
        compiler_params=pltpu.CompilerParams(
            dimension_semantics=("arbitrary",),
            vmem_limit_bytes=_vmem_limit((v1024, wg, wu, wd), windows, scratch, value_tiles,
                                         rows, d_model)),
        name="ffn",
    )(xf, mod, v1024, wg, wu, wd)


MIX_TILE = 256
MIX_SEQS = 4
SKEW_PROMPT = True
SKEW_SAMPLE = False
FFN_TILE = 512
FFN_SEQS = 8


def kernel(x_prompt, x_sample, c_prompt, c_sample, cache_k, cache_v, state_conv, state_h, w_ada, b_ada, g_pre_mix, g_post_mix, w_in, attn_sinks, w_conv, b_conv, w_rg_a, b_rg_a, w_rg_x, b_rg_x, rg_lambda, w_attn_o, w_rnn_o, w_out, g_pre_ffn, g_post_ffn, w_ffn_gate, w_ffn_up, w_ffn_down):
    depth = w_in.shape[0]
    assert depth == 1
    bp, sp, d_model = x_prompt.shape
    bs, ss, _ = x_sample.shape
    d_rnn = w_conv.shape[-1]
    l = 0

    n_c = bp + bs
    pad = (-n_c) % 16
    c_all = jnp.concatenate([c_prompt, c_sample, jnp.zeros((pad, d_model), _F32)], axis=0)
    mod = _ada_call(c_all, w_ada[l], b_ada[l])[:n_c].reshape(n_c, 6, d_model)
    mod_p, mod_s = mod[:bp], mod[bp:]

    bf = lambda w: w.astype(_BF16)
    v1024_mix = jnp.stack([g_pre_mix[l], g_post_mix[l]])
    v1024_ffn = jnp.stack([g_pre_ffn[l], g_post_ffn[l]])
    v1280 = jnp.concatenate([w_conv[l], b_conv[l][None], b_rg_a[l][None], b_rg_x[l][None],
                             rg_lambda[l][None]], axis=0)
    w_rg = bf(jnp.concatenate([w_rg_a[l], w_rg_x[l]], axis=-1))
    mix_w = (attn_sinks[l], v1024_mix, v1280, _slabs(w_in[l]), w_rg, _slabs(w_attn_o[l]),
             _slabs(w_rnn_o[l]), _slabs(w_out[l]))
    ffn_w = (v1024_ffn, bf(w_ffn_gate[l]), bf(w_ffn_up[l]), bf(w_ffn_down[l]))

    xp1, kp, vp, cp, hp = _mixer_call(x_prompt, mod_p, None, *mix_w, ns=1, ls=MIX_TILE,
                                      skew=SKEW_PROMPT)
    hist = (cache_k[l].reshape(bs, WINDOW, KV_W), cache_v[l].reshape(bs, WINDOW, KV_W),
            state_conv[l], state_h[l].reshape(bs, 1, d_rnn))
    xs1, ks, vs, cs, hs = _mixer_call(x_sample, mod_s, hist, *mix_w, ns=MIX_SEQS, ls=ss,
                                      skew=SKEW_SAMPLE)

    yp = _ffn_call(xp1, mod_p, *ffn_w, ns=1, ls=FFN_TILE, seq_len=sp)
    ys = _ffn_call(xs1, mod_s, *ffn_w, ns=FFN_SEQS, ls=ss, seq_len=ss)

    kv_shape = lambda n: (1, n, WINDOW, N_KV_HEADS, HEAD_DIM)
    return (yp.reshape(bp, sp, d_model), ys.reshape(bs, ss, d_model),
            kp.reshape(kv_shape(bp)), vp.reshape(kv_shape(bp)), cp[None], hp.reshape(1, bp, d_rnn),
            ks.reshape(kv_shape(bs)), vs.reshape(kv_shape(bs)), cs[None], hs.reshape(1, bs, d_rnn))
```

```python
import functools
import math

import jax
import jax.numpy as jnp
from jax import lax
from jax.experimental import pallas as pl
from jax.experimental.pallas import tpu as pltpu

CHUNK = 64
N_HEADS = 16
N_KV_HEADS = 4
HEAD_DIM = 64
GROUP = N_HEADS // N_KV_HEADS
WINDOW = 128
N_BACK = WINDOW // CHUNK
KV_W = N_KV_HEADS * HEAD_DIM
RNN_BLOCK = 128
CONV_W = 4
LRU_C = 8.0
EPS = 1e-6
NEG_INF = -1e30

LANES = 128
SUBLANES = 8
MXU_COLS = 256
SLABS_PER_CAST_STEP = 4
KEY_WIN = WINDOW + 2 * CHUNK
MIXER_VALUE_TILES = 14
FFN_VALUE_TILES = 2
FFN_WIDE_VALUES = 3
LOG2E = math.log2(math.e)
ROW_B_CONV, ROW_B_A, ROW_B_X, ROW_LAMBDA = CONV_W, CONV_W + 1, CONV_W + 2, CONV_W + 3
MOD_SHIFT_MIX, MOD_SCALE_MIX, MOD_GATE_MIX, MOD_SHIFT_FFN, MOD_SCALE_FFN, MOD_GATE_FFN = range(6)

_F32 = jnp.float32
_BF16 = jnp.bfloat16


def _dot(a, b):
    return lax.dot_general(a, b, (((1,), (0,)), ((), ())), preferred_element_type=_F32)


def _dot_nt(a, b):
    return lax.dot_general(a, b, (((1,), (1,)), ((), ())), preferred_element_type=_F32)


def _rms(x):
    return x * lax.rsqrt(jnp.mean(x * x, axis=-1, keepdims=True) + EPS)


def _softplus(x):
    return jnp.maximum(x, 0.0) + jnp.log1p(jnp.exp(-jnp.abs(x)))


def _sigmoid(x):
    return 0.5 * jnp.tanh(0.5 * x) + 0.5


def _silu(x):
    h = 0.5 * x
    return h * jnp.tanh(h) + h


def _sqrt_nonneg(x):
    return jnp.exp2((0.5 * LOG2E) * jnp.log(x))


def _neg_expm1_twice(x):
    t = jnp.tanh(x)
    return (-2.0 * t) / (1.0 - t)


def _slab_cast_kernel(n_slabs, w_ref, o_ref):
    for j in range(n_slabs):
        o_ref[j] = w_ref[:, j * MXU_COLS:(j + 1) * MXU_COLS].astype(_BF16)


def _slabs(w):
    k, n = w.shape
    total = n // MXU_COLS
    per_step = math.gcd(total, SLABS_PER_CAST_STEP)
    return pl.pallas_call(
        functools.partial(_slab_cast_kernel, per_step),
        grid=(total // per_step,),
        in_specs=[pl.BlockSpec((k, per_step * MXU_COLS), lambda j: (0, j))],
        out_specs=pl.BlockSpec((per_step, k, MXU_COLS), lambda j: (j, 0, 0)),
        out_shape=jax.ShapeDtypeStruct((total, k, MXU_COLS), _BF16),
        compiler_params=pltpu.CompilerParams(dimension_semantics=("arbitrary",)),
        name="slab_cast",
    )(w)


def _gelu_tanh(x):
    c = math.sqrt(2.0 / math.pi)
    hx = 0.5 * x
    return hx + hx * jnp.tanh(x * (c + (c * 0.044715) * (x * x)))


def _split_heads(kv, scale=None):
    n = kv.shape[0]
    lo = lax.broadcasted_iota(jnp.int32, (n, LANES), 1) < HEAD_DIM
    outs = []
    for blk in range(KV_W // LANES):
        b = kv[:, blk * LANES:(blk + 1) * LANES]
        if scale is not None:
            b = b * scale
        r = pltpu.roll(b, HEAD_DIM, axis=1)
        outs.append((jnp.where(lo, b, 0.0).astype(_BF16), jnp.where(lo, 0.0, r).astype(_BF16)))
        outs.append((jnp.where(lo, r, 0.0).astype(_BF16), jnp.where(lo, 0.0, b).astype(_BF16)))
    return outs


def _ada_kernel(c_ref, w_ref, b_ref, o_ref):
    c = c_ref[...]
    a = _silu(c).astype(_BF16)
    o_ref[...] = _dot(a, w_ref[...].astype(_BF16)) + b_ref[...]


def _ada_call(c, w_ada, b_ada):
    rows, d = c.shape
    n = w_ada.shape[1]
    bn = d
    return pl.pallas_call(
        _ada_kernel,
        grid=(n // bn,),
        in_specs=[
            pl.BlockSpec((rows, d), lambda j: (0, 0)),
            pl.BlockSpec((d, bn), lambda j: (0, j)),
            pl.BlockSpec((1, bn), lambda j: (0, j)),
        ],
        out_specs=pl.BlockSpec((rows, bn), lambda j: (0, j)),
        out_shape=jax.ShapeDtypeStruct((rows, n), _F32),
        compiler_params=pltpu.CompilerParams(dimension_semantics=("arbitrary",)),
        name="ada",
    )(c, w_ada, b_ada.reshape(1, n))


def _mixer_kernel(skew, carry, ns, ls, tiles_per_seq, d_model, d_rnn, *refs):
    if skew:
        xa_ref, xb_ref, moda_ref, modb_ref = refs[:4]
        refs = refs[4:]
    else:
        xa_ref, moda_ref = xb_ref, modb_ref = refs[:2]
        refs = refs[2:]
    n_in = 0 if carry else 4
    if not carry:
        ck_ref, cv_ref, c0_ref, h0_ref = refs[:4]
    (sinks_ref, v1024_ref, v1280_ref, w_in_ref, w_rg_ref, w_ao_ref, w_ro_ref, w_out_ref,
     y_ref, ks_ref, vs_ref, cs_ref, hs_ref,
     u_ref, us_ref, q_ref, k_ref, v_ref, xr_ref, yr_ref, ga_ref, gr_ref, xc_ref, sga_ref, sgr_ref,
     kwin_ref, vwin_ref, kst_ref, vst_ref, cst_ref, bias_ref, hist_ref, hcar_ref,
     attn_ref, rnn_ref, to_strand_ref, from_strand_ref) = refs[n_in:]

    i = pl.program_id(0)
    rows = ns * ls
    lq = min(ls, 2 * CHUNK)
    n_win = ls // lq
    q_w = N_HEADS * HEAD_DIM
    n_blk = d_rnn // RNN_BLOCK
    steps = ls // SUBLANES
    if carry:
        mixed = i + tiles_per_seq - 1 if skew else i
        keep_mix = jnp.where(lax.rem(mixed, tiles_per_seq) == 0, 0.0, 1.0)
        keep_prep = jnp.where(lax.rem(i, tiles_per_seq) == 0, 0.0, 1.0)

    @pl.when(i == 0)
    def _():
        for ref in (q_ref, yr_ref, xc_ref, sga_ref, sgr_ref, kwin_ref, vwin_ref, kst_ref,
                    vst_ref, cst_ref, hist_ref, hcar_ref):
            ref[...] = jnp.zeros(ref.shape, ref.dtype)
        r = lax.broadcasted_iota(jnp.int32, (rows, rows), 0)
        c = lax.broadcasted_iota(jnp.int32, (rows, rows), 1)
        seg, rr = (r // ls) * ls, r % ls
        to_strand_ref[...] = jnp.where(
            c == seg + (rr % SUBLANES) * steps + rr // SUBLANES, 1.0, 0.0).astype(_BF16)
        from_strand_ref[...] = jnp.where(
            c == seg + (rr % steps) * SUBLANES + rr // steps, 1.0, 0.0).astype(_BF16)
        t = lax.broadcasted_iota(jnp.int32, (lq, KEY_WIN), 0)
        j = lax.broadcasted_iota(jnp.int32, (lq, KEY_WIN), 1)
        dist = jnp.abs(t + WINDOW - j).astype(_F32)
        qc = t // CHUNK
        kc = j // CHUNK
        base = jnp.where(kc >= qc, jnp.where(kc <= qc + N_BACK, 0.0, NEG_INF), NEG_INF)
        for h in range(N_KV_HEADS):
            for g in range(GROUP):
                slope = 2.0 ** (-8.0 * (h * GROUP + g + 1) / N_HEADS)
                bias_ref[h, g % 2, (g // 2) * lq:(g // 2 + 1) * lq, :] = LOG2E * (base - slope * dist)

    def publish_state():
        ks_ref[...] = kst_ref[...]
        vs_ref[...] = vst_ref[...]
        cs_ref[...] = cst_ref[...]

    def norm_input():
        xn = _rms(xa_ref[...])
        for s in range(ns):
            sl = slice(s * ls, (s + 1) * ls)
            gain = v1024_ref[0:1, :] * (1.0 + moda_ref[s, MOD_SCALE_MIX:MOD_SCALE_MIX + 1, :])
            u_ref[sl, :] = (xn[sl] * gain + moda_ref[s, MOD_SHIFT_MIX:MOD_SHIFT_MIX + 1, :]).astype(_BF16)
        us_ref[...] = _dot(to_strand_ref[...], u_ref[...]).astype(_BF16)

    proj = []
    off = 0
    for src, ref, width, free_after in (
            (u_ref, q_ref, q_w, lambda c: 2 * (c // MXU_COLS) + 1),
            (u_ref, k_ref, KV_W, lambda c: -1), (u_ref, v_ref, KV_W, lambda c: -1),
            (us_ref, xr_ref, d_rnn, lambda c: -1),
            (us_ref, yr_ref, d_rnn, lambda c: (c + MXU_COLS - 1) // RNN_BLOCK),
            (u_ref, ga_ref, d_model, lambda c: -1), (u_ref, gr_ref, d_model, lambda c: -1)):
        for c in range(0, width, MXU_COLS):
            proj.append((free_after(c), src, ref, c, off + c))
        off += width
    proj.sort(key=lambda p: p[0])

    def emit_proj(slot, slots_left):
        ready = [p for p in proj if p[0] <= slot]
        quota = -(-len(proj) // slots_left) if slots_left else len(proj)
        for p in ready[:quota]:
            proj.remove(p)
            _, src, ref, c, wc = p
            ref[:, c:c + MXU_COLS] = _dot(src[...], w_in_ref[wc // MXU_COLS]).astype(ref.dtype)

    lo_sel = lax.broadcasted_iota(jnp.int32, (lq, LANES), 1) < HEAD_DIM
    if carry:
        kcol = lax.broadcasted_iota(jnp.int32, (1, KEY_WIN), 1)
        hist_bias = jnp.where(kcol < WINDOW, NEG_INF, 0.0).astype(_F32) * (1.0 - keep_mix)

    def attend_scores(s, w, h):
        r0 = s * ls + w * lq
        kw = slice(w * lq, w * lq + KEY_WIN)
        c0 = h * GROUP * HEAD_DIM
        qab = jnp.concatenate([q_ref[r0:r0 + lq, c0:c0 + LANES],
                               q_ref[r0:r0 + lq, c0 + LANES:c0 + 2 * LANES]], axis=0)
        ps, inv = [[], []], [None] * GROUP
        for half in range(2):
            sc = _dot_nt(qab, kwin_ref[s, h, half, kw, :]) + bias_ref[h, half]
            if carry and w == 0:
                sc = sc + hist_bias
            for pair in range(2):
                g = 2 * pair + half
                sg = sc[pair * lq:(pair + 1) * lq]
                sink = LOG2E * sinks_ref[h * GROUP + g]
                mg = jnp.maximum(jnp.max(sg, axis=-1, keepdims=True), sink)
                pg = jnp.exp2(sg - mg)
                inv[g] = 1.0 / (jnp.sum(pg, axis=-1, keepdims=True) + jnp.exp2(sink - mg))
                ps[half].append(pg.astype(_BF16))
        return s, h, r0, kw, c0, ps, inv

    def attend_values(s, h, r0, kw, c0, ps, inv):
        o = (_dot(jnp.concatenate(ps[0], axis=0), vwin_ref[s, h, 0, kw, :])
             + _dot(jnp.concatenate(ps[1], axis=0), vwin_ref[s, h, 1, kw, :]))
        for pair in range(2):
            norm = jnp.where(lo_sel, inv[2 * pair], inv[2 * pair + 1])
            attn_ref[r0:r0 + lq, c0 + pair * LANES:c0 + (pair + 1) * LANES] = (
                o[pair * lq:(pair + 1) * lq] * norm).astype(_BF16)

    units = [(s, w, h) for h in range(N_KV_HEADS) for s in range(ns) for w in range(n_win)]
    units_per_slot = len(units) // (2 * N_KV_HEADS)

    sub = lax.broadcasted_iota(jnp.int32, (SUBLANES, RNN_BLOCK), 0)

    def recur_gates(n):
        half_xc = 0.5 * xc_ref[:, n * RNN_BLOCK:(n + 1) * RNN_BLOCK]
        return half_xc, _dot(half_xc.astype(_BF16), w_rg_ref[n])

    def recur(n, half_xc, half_gates):
        cs_ = slice(n * RNN_BLOCK, (n + 1) * RNN_BLOCK)
        half_c = (-0.5 * LRU_C) * _softplus(-v1280_ref[ROW_LAMBDA:ROW_LAMBDA + 1, cs_])
        half_ba = 0.5 * v1280_ref[ROW_B_A:ROW_B_A + 1, cs_]
        half_bx = 0.5 * v1280_ref[ROW_B_X:ROW_B_X + 1, cs_]
        log_a = half_c * jnp.tanh(half_gates[:, :RNN_BLOCK] + half_ba) + half_c
        gated_x = jnp.tanh(half_gates[:, RNN_BLOCK:] + half_bx) * half_xc + half_xc
        a = jnp.exp(log_a)
        b = _sqrt_nonneg(_neg_expm1_twice(log_a)) * gated_x
        hs = []
        for s in range(ns):
            sl = slice(s * ls, (s + 1) * ls)
            a3 = a[sl].reshape(steps, SUBLANES, RNN_BLOCK)
            b3 = b[sl].reshape(steps, SUBLANES, RNN_BLOCK)
            hz, ap = [b3[0]], [a3[0]]
            for jb in range(1, steps):
                hz.append(a3[jb] * hz[-1] + b3[jb])
                ap.append(a3[jb] * ap[-1])
            ae, he = ap[-1], hz[-1]
            d = 1
            while d < SUBLANES:
                keep = sub >= d
                he = he + ae * jnp.where(keep, pltpu.roll(he, d, axis=0), 0.0)
                ae = ae * jnp.where(keep, pltpu.roll(ae, d, axis=0), 1.0)
                d *= 2
            hprev = hcar_ref[0:1, cs_] * keep_mix if carry else h0_ref[s, 0:1, cs_]
            after = he + ae * hprev
            before = jnp.where(sub == 0, hprev, pltpu.roll(after, 1, axis=0))
            hs.extend(hz[jb] + ap[jb] * before for jb in range(steps))
            hlast = after[SUBLANES - 1:SUBLANES, :]
            hs_ref[s, 0:1, cs_] = hlast
            if carry:
                hcar_ref[0:1, cs_] = hlast
        hfull = jnp.concatenate(hs, axis=0)
        rnn_ref[:, cs_] = (hfull * _gelu_tanh(yr_ref[:, cs_])).astype(_BF16)

    def next_scores():
        return [attend_scores(*units.pop(0)) for _ in range(min(units_per_slot, len(units)))]

    def main_loop(with_proj):
        scored = next_scores()
        gates = recur_gates(0)
        for n in range(n_blk):
            scored_next = next_scores()
            if with_proj:
                emit_proj(n - 1, n_blk - n)
            gates_next = recur_gates(n + 1) if n + 1 < n_blk else None
            recur(n, *gates)
            for unit in scored:
                attend_values(*unit)
            scored, gates = scored_next, gates_next
        for unit in scored:
            attend_values(*unit)
        if with_proj:
            emit_proj(n_blk - 1, 1)
        assert not units

    n_slab = d_model // MXU_COLS

    def branch_projections():
        rnn = _dot(from_strand_ref[...], rnn_ref[...]).astype(_BF16)
        return ([_dot(attn_ref[...], w_ao_ref[j]) for j in range(n_slab)],
                [_dot(rnn, w_ro_ref[j]) for j in range(n_slab)])

    def prepare_window():
        qk_scale = LOG2E * HEAD_DIM ** -0.5
        if carry:
            kwin_ref[0, :, :, 0:WINDOW, :] = kwin_ref[0, :, :, ls:ls + WINDOW, :]
            vwin_ref[0, :, :, 0:WINDOW, :] = vwin_ref[0, :, :, ls:ls + WINDOW, :]
        for s in range(ns):
            sl = slice(s * ls, (s + 1) * ls)
            kd = _split_heads(k_ref[sl, :], qk_scale)
            vd = _split_heads(v_ref[sl, :])
            for h in range(N_KV_HEADS):
                for half in range(2):
                    kwin_ref[s, h, half, WINDOW:WINDOW + ls, :] = kd[h][half]
                    vwin_ref[s, h, half, WINDOW:WINDOW + ls, :] = vd[h][half]
            if carry:
                kst_ref[0] = k_ref[rows - WINDOW:rows, :]
                vst_ref[0] = v_ref[rows - WINDOW:rows, :]
            else:
                hk = _split_heads(ck_ref[s], qk_scale)
                hv = _split_heads(cv_ref[s])
                pad = KEY_WIN - WINDOW - ls
                for h in range(N_KV_HEADS):
                    for half in range(2):
                        kwin_ref[s, h, half, 0:WINDOW, :] = hk[h][half]
                        vwin_ref[s, h, half, 0:WINDOW, :] = hv[h][half]
                        kwin_ref[s, h, half, WINDOW + ls:KEY_WIN, :] = jnp.zeros((pad, LANES), _BF16)
                        vwin_ref[s, h, half, WINDOW + ls:KEY_WIN, :] = jnp.zeros((pad, LANES), _BF16)
                kst_ref[s, 0:WINDOW - ls, :] = ck_ref[s, ls:WINDOW, :]
                kst_ref[s, WINDOW - ls:WINDOW, :] = k_ref[sl, :]
                vst_ref[s, 0:WINDOW - ls, :] = cv_ref[s, ls:WINDOW, :]
                vst_ref[s, WINDOW - ls:WINDOW, :] = v_ref[sl, :]

    first_strand = lax.broadcasted_iota(jnp.int32, (SUBLANES, LANES), 0) == 0

    def conv_column(col):
        lanes = slice(col * LANES, (col + 1) * LANES)
        taps = [jnp.broadcast_to(v1280_ref[r:r + 1, lanes], (SUBLANES, LANES))
                for r in range(CONV_W + 1)]
        for s in range(ns):
            hist = (c0_ref[s, :, lanes] if not carry
                    else hist_ref[0, :, lanes] * keep_prep)
            blocks = [xr_ref[s * ls + jb * SUBLANES:s * ls + (jb + 1) * SUBLANES, lanes]
                      for jb in range(steps)]

            def earlier(jb, d):
                if jb >= d:
                    return blocks[jb - d]
                row = CONV_W - 1 + jb - d
                prev_strand = pltpu.roll(blocks[steps + jb - d], 1, axis=0)
                return jnp.where(first_strand, hist[row:row + 1, :], prev_strand)

            for jb in range(steps):
                acc = taps[ROW_B_CONV] + taps[CONV_W - 1] * blocks[jb]
                for d in range(1, CONV_W):
                    acc = acc + taps[CONV_W - 1 - d] * earlier(jb, d)
                xc_ref[s * ls + jb * SUBLANES:s * ls + (jb + 1) * SUBLANES, lanes] = acc
            tail = jnp.concatenate([blocks[steps - d][SUBLANES - 1:SUBLANES, :]
                                    for d in range(CONV_W - 1, 0, -1)], axis=0)
            cst_ref[s, :, lanes] = tail
            if carry:
                hist_ref[s, :, lanes] = tail

    def conv():
        for col in range(d_rnn // LANES):
            conv_column(col)

    def output_projection(ao, ro):
        merged = jnp.concatenate(
            [(sga_ref[:, j * MXU_COLS:(j + 1) * MXU_COLS] * ao[j]
              + sgr_ref[:, j * MXU_COLS:(j + 1) * MXU_COLS] * ro[j]).astype(_BF16)
             for j in range(n_slab)], axis=1)
        return jnp.concatenate([_dot(merged, w_out_ref[j]) for j in range(n_slab)], axis=1)

    def gate_sigmoids():
        sga_ref[...] = _sigmoid(ga_ref[...])
        sgr_ref[...] = _sigmoid(gr_ref[...])

    def residual(mo):
        yn = _rms(mo)
        for s in range(ns):
            sl = slice(s * ls, (s + 1) * ls)
            gain = v1024_ref[1:2, :] * modb_ref[s, MOD_GATE_MIX:MOD_GATE_MIX + 1, :]
            y_ref[sl, :] = xb_ref[sl, :] + gain * yn[sl]

    if skew:
        publish_state()
        norm_input()
        main_loop(with_proj=True)
        ao, ro = branch_projections()
        prepare_window()
        conv()
        mo = output_projection(ao, ro)
        gate_sigmoids()
        residual(mo)
        emit_proj(n_blk, 0)
    else:
        norm_input()
        emit_proj(n_blk, 0)
        prepare_window()
        conv()
        gate_sigmoids()
        main_loop(with_proj=False)
        residual(output_projection(*branch_projections()))
        publish_state()
    assert not proj


def _whole(memory_space=pltpu.VMEM):
    return pl.BlockSpec(memory_space=memory_space)


def _nbytes(shape, dtype):
    return math.prod(shape) * jnp.dtype(dtype).itemsize


def _vmem_limit(resident, windows, scratch, value_tiles, rows, d_model):
    total = sum(_nbytes(a.shape, a.dtype) for a in resident)
    total += 2 * sum(_nbytes(shape, dtype) for shape, dtype in windows)
    total += sum(_nbytes(s.shape, s.dtype) for s in scratch)
    return total + value_tiles * _nbytes((rows, d_model), _F32)


def _mixer_call(x, mod, hist, sinks, v1024, v1280, w_in, w_rg, w_ao, w_ro, w_out, *, ns, ls, skew):
    n_seq, seq_len, d_model = x.shape
    d_rnn = v1280.shape[1]
    carry = hist is None
    rows = ns * ls
    if carry:
        assert ns == 1 and seq_len % ls == 0 and ls % (2 * CHUNK) == 0
        tiles_per_seq = seq_len // ls
    else:
        assert seq_len == ls == CHUNK and n_seq % ns == 0
        tiles_per_seq = 1
    n_tiles = n_seq * seq_len // rows
    lq = min(ls, 2 * CHUNK)
    xf = x.reshape(n_seq * seq_len, d_model)

    if skew:
        tile_a = lambda i: jnp.minimum(i, n_tiles - 1)
        tile_b = lambda i: jnp.maximum(i - 1, 0)
    else:
        tile_a = tile_b = lambda i: i
    seq_a = lambda i: (tile_a(i) // tiles_per_seq, 0, 0)
    seq_b = lambda i: (tile_b(i) // tiles_per_seq, 0, 0)

    in_specs = [pl.BlockSpec((rows, d_model), lambda i: (tile_a(i), 0)),
                pl.BlockSpec((ns, 6, d_model), seq_a)]
    args = [xf, mod]
    if skew:
        in_specs = [in_specs[0], pl.BlockSpec((rows, d_model), lambda i: (tile_b(i), 0)),
                    in_specs[1], pl.BlockSpec((ns, 6, d_model), seq_b)]
        args = [xf, xf, mod, mod]
    if not carry:
        ck, cv, c0, h0 = hist
        in_specs += [pl.BlockSpec((ns, WINDOW, KV_W), seq_a),
                     pl.BlockSpec((ns, WINDOW, KV_W), seq_a),
                     pl.BlockSpec((ns, CONV_W - 1, d_rnn), seq_a),
                     pl.BlockSpec((ns, 1, d_rnn), seq_b)]
        args += [ck, cv, c0, h0]
    in_specs += [_whole(pltpu.SMEM)] + [_whole()] * 7
    args += [sinks, v1024, v1280, w_in, w_rg, w_ao, w_ro, w_out]

    out_shape = (jax.ShapeDtypeStruct((n_seq * seq_len, d_model), _F32),
                 jax.ShapeDtypeStruct((n_seq, WINDOW, KV_W), _F32),
                 jax.ShapeDtypeStruct((n_seq, WINDOW, KV_W), _F32),
                 jax.ShapeDtypeStruct((n_seq, CONV_W - 1, d_rnn), _F32),
                 jax.ShapeDtypeStruct((n_seq, 1, d_rnn), _F32))
    out_specs = (pl.BlockSpec((rows, d_model), lambda i: (tile_b(i), 0)),
                 pl.BlockSpec((ns, WINDOW, KV_W), seq_b),
                 pl.BlockSpec((ns, WINDOW, KV_W), seq_b),
                 pl.BlockSpec((ns, CONV_W - 1, d_rnn), seq_b),
                 pl.BlockSpec((ns, 1, d_rnn), seq_b))
    win_rows = WINDOW + max(ls, 2 * CHUNK)
    scratch = [pltpu.VMEM((rows, d_model), _BF16),
               pltpu.VMEM((rows, d_model), _BF16),
               pltpu.VMEM((rows, N_HEADS * HEAD_DIM), _BF16),
               pltpu.VMEM((rows, KV_W), _F32),
               pltpu.VMEM((rows, KV_W), _F32),
               pltpu.VMEM((rows, d_rnn), _F32),
               pltpu.VMEM((rows, d_rnn), _F32),
               pltpu.VMEM((rows, d_model), _F32),
               pltpu.VMEM((rows, d_model), _F32),
               pltpu.VMEM((rows, d_rnn), _F32),
               pltpu.VMEM((rows, d_model), _F32),
               pltpu.VMEM((rows, d_model), _F32),
               pltpu.VMEM((ns, N_KV_HEADS, 2, win_rows, LANES), _BF16),
               pltpu.VMEM((ns, N_KV_HEADS, 2, win_rows, LANES), _BF16),
               pltpu.VMEM((ns, WINDOW, KV_W), _F32),
               pltpu.VMEM((ns, WINDOW, KV_W), _F32),
               pltpu.VMEM((ns, CONV_W - 1, d_rnn), _F32),
               pltpu.VMEM((N_KV_HEADS, 2, 2 * lq, KEY_WIN), _F32),
               pltpu.VMEM((ns, CONV_W - 1, d_rnn), _F32),
               pltpu.VMEM((1, d_rnn), _F32),
               pltpu.VMEM((rows, N_HEADS * HEAD_DIM), _BF16),
               pltpu.VMEM((rows, d_rnn), _BF16),
               pltpu.VMEM((rows, rows), _BF16),
               pltpu.VMEM((rows, rows), _BF16)]
    kern = functools.partial(_mixer_kernel, skew, carry, ns, ls, tiles_per_seq, d_model, d_rnn)
    windows = [(spec.block_shape, _F32) for spec in (*in_specs, *out_specs)
               if spec.block_shape is not None]
    vmem_limit = _vmem_limit((v1024, v1280, w_in, w_rg, w_ao, w_ro, w_out), windows, scratch,
                             MIXER_VALUE_TILES, rows, d_model)
    return pl.pallas_call(
        kern,
        grid=(n_tiles + (1 if skew else 0),),
        in_specs=in_specs,
        out_specs=out_specs,
        out_shape=out_shape,
        scratch_shapes=scratch,
        compiler_params=pltpu.CompilerParams(dimension_semantics=("arbitrary",),
                                             vmem_limit_bytes=vmem_limit),
        name="mixer_stream" if carry else "mixer_step",
    )(*args)


def _ffn_kernel(ns, ls, parts, x_ref, mod_ref, v1024_ref, wg_ref, wu_ref, wd_ref, y_ref, u_ref):
    part_rows = ns * ls // parts

    def segments(p):
        if ns == 1:
            return [(0, slice(p * part_rows, (p + 1) * part_rows))]
        per_part = ns // parts
        return [(s, slice(s * ls, (s + 1) * ls)) for s in range(p * per_part, (p + 1) * per_part)]

    def norm(p):
        for s, sl in segments(p):
            gain = v1024_ref[0:1, :] * (1.0 + mod_ref[s, MOD_SCALE_FFN:MOD_SCALE_FFN + 1, :])
            u_ref[sl, :] = (_rms(x_ref[sl, :]) * gain
                            + mod_ref[s, MOD_SHIFT_FFN:MOD_SHIFT_FFN + 1, :]).astype(_BF16)

    def gate_up(p):
        u = u_ref[p * part_rows:(p + 1) * part_rows, :]
        return (_silu(_dot(u, wg_ref[...])) * _dot(u, wu_ref[...])).astype(_BF16)

    def residual(p, f):
        yn = _rms(f)
        for s, sl in segments(p):
            gain = v1024_ref[1:2, :] * mod_ref[s, MOD_GATE_FFN:MOD_GATE_FFN + 1, :]
            lo = sl.start - p * part_rows
            y_ref[sl, :] = x_ref[sl, :] + gain * yn[lo:lo + sl.stop - sl.start]

    norm(0)
    hmid = gate_up(0)
    for p in range(parts):
        if p + 1 < parts:
            norm(p + 1)
        f = _dot(hmid, wd_ref[...])
        if p + 1 < parts:
            hmid = gate_up(p + 1)
        residual(p, f)


def _ffn_call(xf, mod, v1024, wg, wu, wd, *, ns, ls, seq_len, parts):
    n_rows, d_model = xf.shape
    rows = ns * ls
    assert rows % parts == 0 and (ns == 1 or ns % parts == 0)
    if ns == 1:
        tiles_per_seq = seq_len // ls
        seq_of = lambda i: (i // tiles_per_seq, 0, 0)
    else:
        assert ls == seq_len
        seq_of = lambda i: (i, 0, 0)
    scratch = [pltpu.VMEM((rows, d_model), _BF16)]
    windows = [((rows, d_model), _F32), ((ns, 6, d_model), _F32), ((rows, d_model), _F32)]
    value_tiles = FFN_VALUE_TILES + FFN_WIDE_VALUES * pl.cdiv(wg.shape[1], d_model)
    return pl.pallas_call(
        functools.partial(_ffn_kernel, ns, ls, parts),
        grid=(n_rows // rows,),
        in_specs=[pl.BlockSpec((rows, d_model), lambda i: (i, 0)),
                  pl.BlockSpec((ns, 6, d_model), seq_of),
                  _whole(), _whole(), _whole(), _whole()],
        out_specs=pl.BlockSpec((rows, d_model), lambda i: (i, 0)),
        out_shape=jax.ShapeDtypeStruct((n_rows, d_model), _F32),
        scratch_shapes=scratch,
        compiler_params=pltpu.CompilerParams(
            dimension_semantics=("arbitrary",),
            vmem_limit_bytes=_vmem_limit((v1024, wg, wu, wd), windows, scratch, value_tiles,
                                         rows // parts, d_model)),
        name="ffn",
    )(xf, mod, v1024, wg, wu, wd)


MIX_TILE = 256
MIX_SEQS = 4
SKEW_PROMPT = True
SKEW_SAMPLE = False
FFN_TILE = 1024
FFN_SEQS = 16
FFN_PARTS = 4


def kernel(x_prompt, x_sample, c_prompt, c_sample, cache_k, cache_v, state_conv, state_h, w_ada, b_ada, g_pre_mix, g_post_mix, w_in, attn_sinks, w_conv, b_conv, w_rg_a, b_rg_a, w_rg_x, b_rg_x, rg_lambda, w_attn_o, w_rnn_o, w_out, g_pre_ffn, g_post_ffn, w_ffn_gate, w_ffn_up, w_ffn_down):
    depth = w_in.shape[0]
    assert depth == 1
    bp, sp, d_model = x_prompt.shape
    bs, ss, _ = x_sample.shape
    d_rnn = w_conv.shape[-1]
    l = 0

    n_c = bp + bs
    pad = (-n_c) % 16
    c_all = jnp.concatenate([c_prompt, c_sample, jnp.zeros((pad, d_model), _F32)], axis=0)
    mod = _ada_call(c_all, w_ada[l], b_ada[l])[:n_c].reshape(n_c, 6, d_model)
    mod_p, mod_s = mod[:bp], mod[bp:]

    bf = lambda w: w.astype(_BF16)
    v1024_mix = jnp.stack([g_pre_mix[l], g_post_mix[l]])
    v1024_ffn = jnp.stack([g_pre_ffn[l], g_post_ffn[l]])
    v1280 = jnp.concatenate([w_conv[l], b_conv[l][None], b_rg_a[l][None], b_rg_x[l][None],
                             rg_lambda[l][None]], axis=0)
    w_rg = bf(jnp.concatenate([w_rg_a[l], w_rg_x[l]], axis=-1))
    mix_w = (attn_sinks[l], v1024_mix, v1280, _slabs(w_in[l]), w_rg, _slabs(w_attn_o[l]),
             _slabs(w_rnn_o[l]), _slabs(w_out[l]))
    ffn_w = (v1024_ffn, bf(w_ffn_gate[l]), bf(w_ffn_up[l]), bf(w_ffn_down[l]))

    xp1, kp, vp, cp, hp = _mixer_call(x_prompt, mod_p, None, *mix_w, ns=1, ls=MIX_TILE,
                                      skew=SKEW_PROMPT)
    hist = (cache_k[l].reshape(bs, WINDOW, KV_W), cache_v[l].reshape(bs, WINDOW, KV_W),
            state_conv[l], state_h[l].reshape(bs, 1, d_rnn))
    xs1, ks, vs, cs, hs = _mixer_call(x_sample, mod_s, hist, *mix_w, ns=MIX_SEQS, ls=ss,
                                      skew=SKEW_SAMPLE)

    yp = _ffn_call(xp1, mod_p, *ffn_w, ns=1, ls=FFN_TILE, seq_len=sp, parts=FFN_PARTS)
    ys = _ffn_call(xs1, mod_s, *ffn_w, ns=FFN_SEQS, ls=ss, seq_len=ss, parts=FFN_PARTS)

    kv_shape = lambda n: (1, n, WINDOW, N_KV_HEADS, HEAD_DIM)
    return (yp.reshape(bp, sp, d_model), ys.reshape(bs, ss, d_model),
            kp.reshape(kv_shape(bp)), vp.reshape(kv_shape(bp)), cp[None], hp.reshape(1, bp, d_rnn),
            ks.reshape(kv_shape(bs)), vs.reshape(kv_shape(bs)), cs[None], hs.reshape(1, bs, d_rnn))
```

```python
import functools
import math

import jax
import jax.numpy as jnp
from jax import lax
from jax.experimental import pallas as pl
from jax.experimental.pallas import tpu as pltpu

CHUNK = 64
N_HEADS = 16
N_KV_HEADS = 4
HEAD_DIM = 64
GROUP = N_HEADS // N_KV_HEADS
WINDOW = 128
N_BACK = WINDOW // CHUNK
KV_W = N_KV_HEADS * HEAD_DIM
RNN_BLOCK = 128
CONV_W = 4
LRU_C = 8.0
EPS = 1e-6
NEG_INF = -1e30

LANES = 128
SUBLANES = 8
MXU_COLS = 256
SLABS_PER_CAST_STEP = 4
KEY_WIN = WINDOW + 2 * CHUNK
MIXER_VALUE_TILES = 14
FFN_VALUE_TILES = 2
FFN_WIDE_VALUES = 3
LOG2E = math.log2(math.e)
ROW_B_CONV, ROW_B_A, ROW_B_X, ROW_LAMBDA = CONV_W, CONV_W + 1, CONV_W + 2, CONV_W + 3
MOD_SHIFT_MIX, MOD_SCALE_MIX, MOD_GATE_MIX, MOD_SHIFT_FFN, MOD_SCALE_FFN, MOD_GATE_FFN = range(6)

_F32 = jnp.float32
_BF16 = jnp.bfloat16


def _dot(a, b):
    return lax.dot_general(a, b, (((1,), (0,)), ((), ())), preferred_element_type=_F32)


def _dot_nt(a, b):
    return lax.dot_general(a, b, (((1,), (1,)), ((), ())), preferred_element_type=_F32)


def _rms(x):
    return x * lax.rsqrt(jnp.mean(x * x, axis=-1, keepdims=True) + EPS)


def _softplus(x):
    return jnp.maximum(x, 0.0) + jnp.log1p(jnp.exp(-jnp.abs(x)))


def _sigmoid(x):
    return 0.5 * jnp.tanh(0.5 * x) + 0.5


def _silu(x):
    h = 0.5 * x
    return h * jnp.tanh(h) + h


def _sqrt_nonneg(x):
    return jnp.exp2((0.5 * LOG2E) * jnp.log(x))


def _neg_expm1_twice(x):
    t = jnp.tanh(x)
    return (-2.0 * t) / (1.0 - t)


def _slab_cast_kernel(n_slabs, w_ref, o_ref):
    for j in range(n_slabs):
        o_ref[j] = w_ref[:, j * MXU_COLS:(j + 1) * MXU_COLS].astype(_BF16)


def _slabs(w):
    k, n = w.shape
    total = n // MXU_COLS
    per_step = math.gcd(total, SLABS_PER_CAST_STEP)
    return pl.pallas_call(
        functools.partial(_slab_cast_kernel, per_step),
        grid=(total // per_step,),
        in_specs=[pl.BlockSpec((k, per_step * MXU_COLS), lambda j: (0, j))],
        out_specs=pl.BlockSpec((per_step, k, MXU_COLS), lambda j: (j, 0, 0)),
        out_shape=jax.ShapeDtypeStruct((total, k, MXU_COLS), _BF16),
        compiler_params=pltpu.CompilerParams(dimension_semantics=("arbitrary",)),
        name="slab_cast",
    )(w)


def _gelu_tanh(x):
    c = math.sqrt(2.0 / math.pi)
    hx = 0.5 * x
    return hx + hx * jnp.tanh(x * (c + (c * 0.044715) * (x * x)))


def _split_heads(kv, scale=None):
    n = kv.shape[0]
    lo = lax.broadcasted_iota(jnp.int32, (n, LANES), 1) < HEAD_DIM
    outs = []
    for blk in range(KV_W // LANES):
        b = kv[:, blk * LANES:(blk + 1) * LANES]
        if scale is not None:
            b = b * scale
        r = pltpu.roll(b, HEAD_DIM, axis=1)
        outs.append((jnp.where(lo, b, 0.0).astype(_BF16), jnp.where(lo, 0.0, r).astype(_BF16)))
        outs.append((jnp.where(lo, r, 0.0).astype(_BF16), jnp.where(lo, 0.0, b).astype(_BF16)))
    return outs


def _ada_kernel(c_ref, w_ref, b_ref, o_ref):
    c = c_ref[...]
    a = _silu(c).astype(_BF16)
    o_ref[...] = _dot(a, w_ref[...].astype(_BF16)) + b_ref[...]


def _ada_call(c, w_ada, b_ada):
    rows, d = c.shape
    n = w_ada.shape[1]
    bn = d
    return pl.pallas_call(
        _ada_kernel,
        grid=(n // bn,),
        in_specs=[
            pl.BlockSpec((rows, d), lambda j: (0, 0)),
            pl.BlockSpec((d, bn), lambda j: (0, j)),
            pl.BlockSpec((1, bn), lambda j: (0, j)),
        ],
        out_specs=pl.BlockSpec((rows, bn), lambda j: (0, j)),
        out_shape=jax.ShapeDtypeStruct((rows, n), _F32),
        compiler_params=pltpu.CompilerParams(dimension_semantics=("arbitrary",)),
        name="ada",
    )(c, w_ada, b_ada.reshape(1, n))


def _mixer_kernel(skew, carry, ns, ls, tiles_per_seq, d_model, d_rnn, *refs):
    if skew:
        xa_ref, xb_ref, moda_ref, modb_ref = refs[:4]
        refs = refs[4:]
    else:
        xa_ref, moda_ref = xb_ref, modb_ref = refs[:2]
        refs = refs[2:]
    n_in = 0 if carry else 4
    if not carry:
        ck_ref, cv_ref, c0_ref, h0_ref = refs[:4]
    (sinks_ref, v1024_ref, v1280_ref, w_in_ref, w_rg_ref, w_ao_ref, w_ro_ref, w_out_ref,
     y_ref, ks_ref, vs_ref, cs_ref, hs_ref,
     u_ref, us_ref, q_ref, k_ref, v_ref, xr_ref, yr_ref, ga_ref, gr_ref, xc_ref, sga_ref, sgr_ref,
     kwin_ref, vwin_ref, kst_ref, vst_ref, cst_ref, bias_ref, hist_ref, hcar_ref,
     attn_ref, rnn_ref, to_strand_ref, from_strand_ref) = refs[n_in:]

    i = pl.program_id(0)
    rows = ns * ls
    lq = min(ls, 2 * CHUNK)
    n_win = ls // lq
    q_w = N_HEADS * HEAD_DIM
    n_blk = d_rnn // RNN_BLOCK
    steps = ls // SUBLANES
    if carry:
        mixed = i + tiles_per_seq - 1 if skew else i
        keep_mix = jnp.where(lax.rem(mixed, tiles_per_seq) == 0, 0.0, 1.0)
        keep_prep = jnp.where(lax.rem(i, tiles_per_seq) == 0, 0.0, 1.0)

    @pl.when(i == 0)
    def _():
        for ref in (q_ref, yr_ref, xc_ref, sga_ref, sgr_ref, kwin_ref, vwin_ref, kst_ref,
                    vst_ref, cst_ref, hist_ref, hcar_ref):
            ref[...] = jnp.zeros(ref.shape, ref.dtype)
        r = lax.broadcasted_iota(jnp.int32, (rows, rows), 0)
        c = lax.broadcasted_iota(jnp.int32, (rows, rows), 1)
        seg, rr = (r // ls) * ls, r % ls
        to_strand_ref[...] = jnp.where(
            c == seg + (rr % SUBLANES) * steps + rr // SUBLANES, 1.0, 0.0).astype(_BF16)
        from_strand_ref[...] = jnp.where(
            c == seg + (rr % steps) * SUBLANES + rr // steps, 1.0, 0.0).astype(_BF16)
        t = lax.broadcasted_iota(jnp.int32, (lq, KEY_WIN), 0)
        j = lax.broadcasted_iota(jnp.int32, (lq, KEY_WIN), 1)
        dist = jnp.abs(t + WINDOW - j).astype(_F32)
        qc = t // CHUNK
        kc = j // CHUNK
        base = jnp.where(kc >= qc, jnp.where(kc <= qc + N_BACK, 0.0, NEG_INF), NEG_INF)
        for h in range(N_KV_HEADS):
            for g in range(GROUP):
                slope = 2.0 ** (-8.0 * (h * GROUP + g + 1) / N_HEADS)
                bias_ref[h, g % 2, (g // 2) * lq:(g // 2 + 1) * lq, :] = LOG2E * (base - slope * dist)

    def publish_state():
        ks_ref[...] = kst_ref[...]
        vs_ref[...] = vst_ref[...]
        cs_ref[...] = cst_ref[...]

    def norm_input():
        xn = _rms(xa_ref[...])
        for s in range(ns):
            sl = slice(s * ls, (s + 1) * ls)
            gain = v1024_ref[0:1, :] * (1.0 + moda_ref[s, MOD_SCALE_MIX:MOD_SCALE_MIX + 1, :])
            u_ref[sl, :] = (xn[sl] * gain + moda_ref[s, MOD_SHIFT_MIX:MOD_SHIFT_MIX + 1, :]).astype(_BF16)
        us_ref[...] = _dot(to_strand_ref[...], u_ref[...]).astype(_BF16)

    proj = []
    off = 0
    for src, ref, width, free_after in (
            (u_ref, q_ref, q_w, lambda c: 2 * (c // MXU_COLS) + 1),
            (u_ref, k_ref, KV_W, lambda c: -1), (u_ref, v_ref, KV_W, lambda c: -1),
            (us_ref, xr_ref, d_rnn, lambda c: -1),
            (us_ref, yr_ref, d_rnn, lambda c: (c + MXU_COLS - 1) // RNN_BLOCK),
            (u_ref, ga_ref, d_model, lambda c: -1), (u_ref, gr_ref, d_model, lambda c: -1)):
        for c in range(0, width, MXU_COLS):
            proj.append((free_after(c), src, ref, c, off + c))
        off += width
    proj.sort(key=lambda p: p[0])

    def emit_proj(slot, slots_left):
        ready = [p for p in proj if p[0] <= slot]
        quota = -(-len(proj) // slots_left) if slots_left else len(proj)
        for p in ready[:quota]:
            proj.remove(p)
            _, src, ref, c, wc = p
            ref[:, c:c + MXU_COLS] = _dot(src[...], w_in_ref[wc // MXU_COLS]).astype(ref.dtype)

    lo_sel = lax.broadcasted_iota(jnp.int32, (lq, LANES), 1) < HEAD_DIM
    if carry:
        kcol = lax.broadcasted_iota(jnp.int32, (1, KEY_WIN), 1)
        hist_bias = jnp.where(kcol < WINDOW, NEG_INF, 0.0).astype(_F32) * (1.0 - keep_mix)

    def attend_scores(s, w, h):
        r0 = s * ls + w * lq
        kw = slice(w * lq, w * lq + KEY_WIN)
        c0 = h * GROUP * HEAD_DIM
        qab = jnp.concatenate([q_ref[r0:r0 + lq, c0:c0 + LANES],
                               q_ref[r0:r0 + lq, c0 + LANES:c0 + 2 * LANES]], axis=0)
        ps, inv = [[], []], [None] * GROUP
        for half in range(2):
            sc = _dot_nt(qab, kwin_ref[s, h, half, kw, :]) + bias_ref[h, half]
            if carry and w == 0:
                sc = sc + hist_bias
            for pair in range(2):
                g = 2 * pair + half
                sg = sc[pair * lq:(pair + 1) * lq]
                sink = LOG2E * sinks_ref[h * GROUP + g]
                mg = jnp.maximum(jnp.max(sg, axis=-1, keepdims=True), sink)
                pg = jnp.exp2(sg - mg)
                inv[g] = 1.0 / (jnp.sum(pg, axis=-1, keepdims=True) + jnp.exp2(sink - mg))
                ps[half].append(pg.astype(_BF16))
        return s, h, r0, kw, c0, ps, inv

    def attend_values(s, h, r0, kw, c0, ps, inv):
        o = (_dot(jnp.concatenate(ps[0], axis=0), vwin_ref[s, h, 0, kw, :])
             + _dot(jnp.concatenate(ps[1], axis=0), vwin_ref[s, h, 1, kw, :]))
        for pair in range(2):
            norm = jnp.where(lo_sel, inv[2 * pair], inv[2 * pair + 1])
            attn_ref[r0:r0 + lq, c0 + pair * LANES:c0 + (pair + 1) * LANES] = (
                o[pair * lq:(pair + 1) * lq] * norm).astype(_BF16)

    units = [(s, w, h) for h in range(N_KV_HEADS) for s in range(ns) for w in range(n_win)]
    units_per_slot = len(units) // (2 * N_KV_HEADS)

    sub = lax.broadcasted_iota(jnp.int32, (SUBLANES, RNN_BLOCK), 0)

    def recur_gates(n):
        half_xc = 0.5 * xc_ref[:, n * RNN_BLOCK:(n + 1) * RNN_BLOCK]
        return half_xc, _dot(half_xc.astype(_BF16), w_rg_ref[n])

    def recur(n, half_xc, half_gates):
        cs_ = slice(n * RNN_BLOCK, (n + 1) * RNN_BLOCK)
        half_c = (-0.5 * LRU_C) * _softplus(-v1280_ref[ROW_LAMBDA:ROW_LAMBDA + 1, cs_])
        half_ba = 0.5 * v1280_ref[ROW_B_A:ROW_B_A + 1, cs_]
        half_bx = 0.5 * v1280_ref[ROW_B_X:ROW_B_X + 1, cs_]
        log_a = half_c * jnp.tanh(half_gates[:, :RNN_BLOCK] + half_ba) + half_c
        gated_x = jnp.tanh(half_gates[:, RNN_BLOCK:] + half_bx) * half_xc + half_xc
        a = jnp.exp(log_a)
        b = _sqrt_nonneg(_neg_expm1_twice(log_a)) * gated_x
        hs = []
        for s in range(ns):
            sl = slice(s * ls, (s + 1) * ls)
            a3 = a[sl].reshape(steps, SUBLANES, RNN_BLOCK)
            b3 = b[sl].reshape(steps, SUBLANES, RNN_BLOCK)
            hz, ap = [b3[0]], [a3[0]]
            for jb in range(1, steps):
                hz.append(a3[jb] * hz[-1] + b3[jb])
                ap.append(a3[jb] * ap[-1])
            ae, he = ap[-1], hz[-1]
            d = 1
            while d < SUBLANES:
                keep = sub >= d
                he = he + ae * jnp.where(keep, pltpu.roll(he, d, axis=0), 0.0)
                ae = ae * jnp.where(keep, pltpu.roll(ae, d, axis=0), 1.0)
                d *= 2
            hprev = hcar_ref[0:1, cs_] * keep_mix if carry else h0_ref[s, 0:1, cs_]
            after = he + ae * hprev
            before = jnp.where(sub == 0, hprev, pltpu.roll(after, 1, axis=0))
            hs.extend(hz[jb] + ap[jb] * before for jb in range(steps))
            hlast = after[SUBLANES - 1:SUBLANES, :]
            hs_ref[s, 0:1, cs_] = hlast
            if carry:
                hcar_ref[0:1, cs_] = hlast
        hfull = jnp.concatenate(hs, axis=0)
        rnn_ref[:, cs_] = (hfull * _gelu_tanh(yr_ref[:, cs_])).astype(_BF16)

    def next_scores():
        return [attend_scores(*units.pop(0)) for _ in range(min(units_per_slot, len(units)))]

    def main_loop(with_proj):
        scored = next_scores()
        gates = recur_gates(0)
        ao = None
        for n in range(n_blk):
            scored_next = next_scores()
            if with_proj:
                emit_proj(n - 1, n_blk - n)
            gates_next = recur_gates(n + 1) if n + 1 < n_blk else None
            recur(n, *gates)
            for unit in scored:
                attend_values(*unit)
            if ao is None and not (scored or scored_next or units):
                ao = attention_projection()
            scored, gates = scored_next, gates_next
        for unit in scored:
            attend_values(*unit)
        if with_proj:
            emit_proj(n_blk - 1, 1)
        assert not units
        return ao if ao is not None else attention_projection()

    n_slab = d_model // MXU_COLS

    def attention_projection():
        return [_dot(attn_ref[...], w_ao_ref[j]) for j in range(n_slab)]

    def recurrent_projection():
        rnn = _dot(from_strand_ref[...], rnn_ref[...]).astype(_BF16)
        return [_dot(rnn, w_ro_ref[j]) for j in range(n_slab)]

    def prepare_window_and_conv():
        qk_scale = LOG2E * HEAD_DIM ** -0.5
        if carry:
            kwin_ref[0, :, :, 0:WINDOW, :] = kwin_ref[0, :, :, ls:ls + WINDOW, :]
            vwin_ref[0, :, :, 0:WINDOW, :] = vwin_ref[0, :, :, ls:ls + WINDOW, :]
        for s in range(ns):
            sl = slice(s * ls, (s + 1) * ls)
            kd = _split_heads(k_ref[sl, :], qk_scale)
            vd = _split_heads(v_ref[sl, :])
            for h in range(N_KV_HEADS):
                for half in range(2):
                    kwin_ref[s, h, half, WINDOW:WINDOW + ls, :] = kd[h][half]
                    vwin_ref[s, h, half, WINDOW:WINDOW + ls, :] = vd[h][half]
            if carry:
                kst_ref[0] = k_ref[rows - WINDOW:rows, :]
                vst_ref[0] = v_ref[rows - WINDOW:rows, :]
            else:
                hk = _split_heads(ck_ref[s], qk_scale)
                hv = _split_heads(cv_ref[s])
                pad = KEY_WIN - WINDOW - ls
                for h in range(N_KV_HEADS):
                    for half in range(2):
                        kwin_ref[s, h, half, 0:WINDOW, :] = hk[h][half]
                        vwin_ref[s, h, half, 0:WINDOW, :] = hv[h][half]
                        kwin_ref[s, h, half, WINDOW + ls:KEY_WIN, :] = jnp.zeros((pad, LANES), _BF16)
                        vwin_ref[s, h, half, WINDOW + ls:KEY_WIN, :] = jnp.zeros((pad, LANES), _BF16)
                kst_ref[s, 0:WINDOW - ls, :] = ck_ref[s, ls:WINDOW, :]
                kst_ref[s, WINDOW - ls:WINDOW, :] = k_ref[sl, :]
                vst_ref[s, 0:WINDOW - ls, :] = cv_ref[s, ls:WINDOW, :]
                vst_ref[s, WINDOW - ls:WINDOW, :] = v_ref[sl, :]

            hist = c0_ref[s] if not carry else hist_ref[0] * keep_prep
            first = lax.broadcasted_iota(jnp.int32, (SUBLANES, LANES), 0) == 0
            for col in range(0, d_rnn, LANES):
                lanes = slice(col, col + LANES)
                taps = [jnp.broadcast_to(v1280_ref[r:r + 1, lanes], (SUBLANES, LANES))
                        for r in range(CONV_W + 1)]
                blocks = [xr_ref[s * ls + jb * SUBLANES:s * ls + (jb + 1) * SUBLANES, lanes]
                          for jb in range(steps)]

                def earlier(jb, d):
                    if jb >= d:
                        return blocks[jb - d]
                    row = CONV_W - 1 + jb - d
                    prev_strand = pltpu.roll(blocks[steps + jb - d], 1, axis=0)
                    return jnp.where(first, hist[row:row + 1, lanes], prev_strand)

                for jb in range(steps):
                    acc = taps[ROW_B_CONV] + taps[CONV_W - 1] * blocks[jb]
                    for d in range(1, CONV_W):
                        acc = acc + taps[CONV_W - 1 - d] * earlier(jb, d)
                    xc_ref[s * ls + jb * SUBLANES:s * ls + (jb + 1) * SUBLANES, lanes] = acc
            tail = jnp.concatenate(
                [xr_ref[s * ls + (steps - d + 1) * SUBLANES - 1:s * ls + (steps - d + 1) * SUBLANES, :]
                 for d in range(CONV_W - 1, 0, -1)], axis=0)
            cst_ref[s] = tail
            if carry:
                hist_ref[s] = tail

    def output_projection(ao, ro):
        merged = jnp.concatenate(
            [(sga_ref[:, j * MXU_COLS:(j + 1) * MXU_COLS] * ao[j]
              + sgr_ref[:, j * MXU_COLS:(j + 1) * MXU_COLS] * ro[j]).astype(_BF16)
             for j in range(n_slab)], axis=1)
        return jnp.concatenate([_dot(merged, w_out_ref[j]) for j in range(n_slab)], axis=1)

    def gate_sigmoids():
        sga_ref[...] = _sigmoid(ga_ref[...])
        sgr_ref[...] = _sigmoid(gr_ref[...])

    def residual(mo):
        yn = _rms(mo)
        for s in range(ns):
            sl = slice(s * ls, (s + 1) * ls)
            gain = v1024_ref[1:2, :] * modb_ref[s, MOD_GATE_MIX:MOD_GATE_MIX + 1, :]
            y_ref[sl, :] = xb_ref[sl, :] + gain * yn[sl]

    if skew:
        publish_state()
        norm_input()
        ao = main_loop(with_proj=True)
        ro = recurrent_projection()
        prepare_window_and_conv()
        mo = output_projection(ao, ro)
        gate_sigmoids()
        residual(mo)
        emit_proj(n_blk, 0)
    else:
        norm_input()
        emit_proj(n_blk, 0)
        prepare_window_and_conv()
        gate_sigmoids()
        ao = main_loop(with_proj=False)
        residual(output_projection(ao, recurrent_projection()))
        publish_state()
    assert not proj


def _whole(memory_space=pltpu.VMEM):
    return pl.BlockSpec(memory_space=memory_space)


def _nbytes(shape, dtype):
    return math.prod(shape) * jnp.dtype(dtype).itemsize


def _vmem_limit(resident, windows, scratch, value_tiles, rows, d_model):
    total = sum(_nbytes(a.shape, a.dtype) for a in resident)
    total += 2 * sum(_nbytes(shape, dtype) for shape, dtype in windows)
    total += sum(_nbytes(s.shape, s.dtype) for s in scratch)
    return total + value_tiles * _nbytes((rows, d_model), _F32)


def _mixer_call(x, mod, hist, sinks, v1024, v1280, w_in, w_rg, w_ao, w_ro, w_out, *, ns, ls, skew):
    n_seq, seq_len, d_model = x.shape
    d_rnn = v1280.shape[1]
    carry = hist is None
    rows = ns * ls
    if carry:
        assert ns == 1 and seq_len % ls == 0 and ls % (2 * CHUNK) == 0
        tiles_per_seq = seq_len // ls
    else:
        assert seq_len == ls == CHUNK and n_seq % ns == 0
        tiles_per_seq = 1
    n_tiles = n_seq * seq_len // rows
    lq = min(ls, 2 * CHUNK)
    xf = x.reshape(n_seq * seq_len, d_model)

    if skew:
        tile_a = lambda i: jnp.minimum(i, n_tiles - 1)
        tile_b = lambda i: jnp.maximum(i - 1, 0)
    else:
        tile_a = tile_b = lambda i: i
    seq_a = lambda i: (tile_a(i) // tiles_per_seq, 0, 0)
    seq_b = lambda i: (tile_b(i) // tiles_per_seq, 0, 0)

    in_specs = [pl.BlockSpec((rows, d_model), lambda i: (tile_a(i), 0)),
                pl.BlockSpec((ns, 6, d_model), seq_a)]
    args = [xf, mod]
    if skew:
        in_specs = [in_specs[0], pl.BlockSpec((rows, d_model), lambda i: (tile_b(i), 0)),
                    in_specs[1], pl.BlockSpec((ns, 6, d_model), seq_b)]
        args = [xf, xf, mod, mod]
    if not carry:
        ck, cv, c0, h0 = hist
        in_specs += [pl.BlockSpec((ns, WINDOW, KV_W), seq_a),
                     pl.BlockSpec((ns, WINDOW, KV_W), seq_a),
                     pl.BlockSpec((ns, CONV_W - 1, d_rnn), seq_a),
                     pl.BlockSpec((ns, 1, d_rnn), seq_b)]
        args += [ck, cv, c0, h0]
    in_specs += [_whole(pltpu.SMEM)] + [_whole()] * 7
    args += [sinks, v1024, v1280, w_in, w_rg, w_ao, w_ro, w_out]

    out_shape = (jax.ShapeDtypeStruct((n_seq * seq_len, d_model), _F32),
                 jax.ShapeDtypeStruct((n_seq, WINDOW, KV_W), _F32),
                 jax.ShapeDtypeStruct((n_seq, WINDOW, KV_W), _F32),
                 jax.ShapeDtypeStruct((n_seq, CONV_W - 1, d_rnn), _F32),
                 jax.ShapeDtypeStruct((n_seq, 1, d_rnn), _F32))
    out_specs = (pl.BlockSpec((rows, d_model), lambda i: (tile_b(i), 0)),
                 pl.BlockSpec((ns, WINDOW, KV_W), seq_b),
                 pl.BlockSpec((ns, WINDOW, KV_W), seq_b),
                 pl.BlockSpec((ns, CONV_W - 1, d_rnn), seq_b),
                 pl.BlockSpec((ns, 1, d_rnn), seq_b))
    win_rows = WINDOW + max(ls, 2 * CHUNK)
    scratch = [pltpu.VMEM((rows, d_model), _BF16),
               pltpu.VMEM((rows, d_model), _BF16),
               pltpu.VMEM((rows, N_HEADS * HEAD_DIM), _BF16),
               pltpu.VMEM((rows, KV_W), _F32),
               pltpu.VMEM((rows, KV_W), _F32),
               pltpu.VMEM((rows, d_rnn), _F32),
               pltpu.VMEM((rows, d_rnn), _F32),
               pltpu.VMEM((rows, d_model), _F32),
               pltpu.VMEM((rows, d_model), _F32),
               pltpu.VMEM((rows, d_rnn), _F32),
               pltpu.VMEM((rows, d_model), _F32),
               pltpu.VMEM((rows, d_model), _F32),
               pltpu.VMEM((ns, N_KV_HEADS, 2, win_rows, LANES), _BF16),
               pltpu.VMEM((ns, N_KV_HEADS, 2, win_rows, LANES), _BF16),
               pltpu.VMEM((ns, WINDOW, KV_W), _F32),
               pltpu.VMEM((ns, WINDOW, KV_W), _F32),
               pltpu.VMEM((ns, CONV_W - 1, d_rnn), _F32),
               pltpu.VMEM((N_KV_HEADS, 2, 2 * lq, KEY_WIN), _F32),
               pltpu.VMEM((ns, CONV_W - 1, d_rnn), _F32),
               pltpu.VMEM((1, d_rnn), _F32),
               pltpu.VMEM((rows, N_HEADS * HEAD_DIM), _BF16),
               pltpu.VMEM((rows, d_rnn), _BF16),
               pltpu.VMEM((rows, rows), _BF16),
               pltpu.VMEM((rows, rows), _BF16)]
    kern = functools.partial(_mixer_kernel, skew, carry, ns, ls, tiles_per_seq, d_model, d_rnn)
    windows = [(spec.block_shape, _F32) for spec in (*in_specs, *out_specs)
               if spec.block_shape is not None]
    vmem_limit = _vmem_limit((v1024, v1280, w_in, w_rg, w_ao, w_ro, w_out), windows, scratch,
                             MIXER_VALUE_TILES, rows, d_model)
    return pl.pallas_call(
        kern,
        grid=(n_tiles + (1 if skew else 0),),
        in_specs=in_specs,
        out_specs=out_specs,
        out_shape=out_shape,
        scratch_shapes=scratch,
        compiler_params=pltpu.CompilerParams(dimension_semantics=("arbitrary",),
                                             vmem_limit_bytes=vmem_limit),
        name="mixer_stream" if carry else "mixer_step",
    )(*args)


def _ffn_kernel(ns, ls, x_ref, mod_ref, v1024_ref, wg_ref, wu_ref, wd_ref, y_ref, u_ref):
    x = x_ref[...]
    xn = _rms(x)
    for s in range(ns):
        sl = slice(s * ls, (s + 1) * ls)
        gain = v1024_ref[0:1, :] * (1.0 + mod_ref[s, MOD_SCALE_FFN:MOD_SCALE_FFN + 1, :])
        u_ref[sl, :] = (xn[sl] * gain + mod_ref[s, MOD_SHIFT_FFN:MOD_SHIFT_FFN + 1, :]).astype(_BF16)
    u = u_ref[...]
    g = _dot(u, wg_ref[...])
    up = _dot(u, wu_ref[...])
    hmid = (_silu(g) * up).astype(_BF16)
    yn = _rms(_dot(hmid, wd_ref[...]))
    for s in range(ns):
        sl = slice(s * ls, (s + 1) * ls)
        gain = v1024_ref[1:2, :] * mod_ref[s, MOD_GATE_FFN:MOD_GATE_FFN + 1, :]
        y_ref[sl, :] = x[sl] + gain * yn[sl]


def _ffn_call(xf, mod, v1024, wg, wu, wd, *, ns, ls, seq_len):
    n_rows, d_model = xf.shape
    rows = ns * ls
    if ns == 1:
        tiles_per_seq = seq_len // ls
        seq_of = lambda i: (i // tiles_per_seq, 0, 0)
    else:
        assert ls == seq_len
        seq_of = lambda i: (i, 0, 0)
    scratch = [pltpu.VMEM((rows, d_model), _BF16)]
    windows = [((rows, d_model), _F32), ((ns, 6, d_model), _F32), ((rows, d_model), _F32)]
    value_tiles = FFN_VALUE_TILES + FFN_WIDE_VALUES * pl.cdiv(wg.shape[1], d_model)
    return pl.pallas_call(
        functools.partial(_ffn_kernel, ns, ls),
        grid=(n_rows // rows,),
        in_specs=[pl.BlockSpec((rows, d_model), lambda i: (i, 0)),
                  pl.BlockSpec((ns, 6, d_model), seq_of),
                  _whole(), _whole(), _whole(), _whole()],
        out_specs=pl.BlockSpec((rows, d_model), lambda i: (i, 0)),
        out_shape=jax.ShapeDtypeStruct((n_rows, d_model), _F32),
        scratch_shapes=scratch,
        compiler_params=pltpu.CompilerParams(
            dimension_semantics=("arbitrary",),
            vmem_limit_bytes=_vmem_limit((v1024, wg, wu, wd), windows, scratch, value_tiles,
                                         rows, d_model)),
        name="ffn",
    )(xf, mod, v1024, wg, wu, wd)


MIX_TILE = 256
MIX_SEQS = 4
SKEW_PROMPT = True
SKEW_SAMPLE = False
FFN_TILE = 512
FFN_SEQS = 8


def kernel(x_prompt, x_sample, c_prompt, c_sample, cache_k, cache_v, state_conv, state_h, w_ada, b_ada, g_pre_mix, g_post_mix, w_in, attn_sinks, w_conv, b_conv, w_rg_a, b_rg_a, w_rg_x, b_rg_x, rg_lambda, w_attn_o, w_rnn_o, w_out, g_pre_ffn, g_post_ffn, w_ffn_gate, w_ffn_up, w_ffn_down):
    depth = w_in.shape[0]
    assert depth == 1
    bp, sp, d_model = x_prompt.shape
    bs, ss, _ = x_sample.shape
    d_rnn = w_conv.shape[-1]
    l = 0

    n_c = bp + bs
    pad = (-n_c) % 16
    c_all = jnp.concatenate([c_prompt, c_sample, jnp.zeros((pad, d_model), _F32)], axis=0)
    mod = _ada_call(c_all, w_ada[l], b_ada[l])[:n_c].reshape(n_c, 6, d_model)
    mod_p, mod_s = mod[:bp], mod[bp:]

    bf = lambda w: w.astype(_BF16)
    v1024_mix = jnp.stack([g_pre_mix[l], g_post_mix[l]])
    v1024_ffn = jnp.stack([g_pre_ffn[l], g_post_ffn[l]])
    v1280 = jnp.concatenate([w_conv[l], b_conv[l][None], b_rg_a[l][None], b_rg_x[l][None],
                             rg_lambda[l][None]], axis=0)
    w_rg = bf(jnp.concatenate([w_rg_a[l], w_rg_x[l]], axis=-1))
    mix_w = (attn_sinks[l], v1024_mix, v1280, _slabs(w_in[l]), w_rg, _slabs(w_attn_o[l]),
             _slabs(w_rnn_o[l]), _slabs(w_out[l]))
    ffn_w = (v1024_ffn, bf(w_ffn_gate[l]), bf(w_ffn_up[l]), bf(w_ffn_down[l]))

    xp1, kp, vp, cp, hp = _mixer_call(x_prompt, mod_p, None, *mix_w, ns=1, ls=MIX_TILE,
                                      skew=SKEW_PROMPT)
    hist = (cache_k[l].reshape(bs, WINDOW, KV_W), cache_v[l].reshape(bs, WINDOW, KV_W),
            state_conv[l], state_h[l].reshape(bs, 1, d_rnn))
    xs1, ks, vs, cs, hs = _mixer_call(x_sample, mod_s, hist, *mix_w, ns=MIX_SEQS, ls=ss,
                                      skew=SKEW_SAMPLE)

    yp = _ffn_call(xp1, mod_p, *ffn_w, ns=1, ls=FFN_TILE, seq_len=sp)
    ys = _ffn_call(xs1, mod_s, *ffn_w, ns=FFN_SEQS, ls=ss, seq_len=ss)

    kv_shape = lambda n: (1, n, WINDOW, N_KV_HEADS, HEAD_DIM)
    return (yp.reshape(bp, sp, d_model), ys.reshape(bs, ss, d_model),
            kp.reshape(kv_shape(bp)), vp.reshape(kv_shape(bp)), cp[None], hp.reshape(1, bp, d_rnn),
            ks.reshape(kv_shape(bs)), vs.reshape(kv_shape(bs)), cs[None], hs.reshape(1, bs, d_rnn))
```

```python
import functools
import math

import jax
import jax.numpy as jnp
from jax import lax
from jax.experimental import pallas as pl
from jax.experimental.pallas import tpu as pltpu

CHUNK = 64
N_HEADS = 16
N_KV_HEADS = 4
HEAD_DIM = 64
GROUP = N_HEADS // N_KV_HEADS
WINDOW = 128
N_BACK = WINDOW // CHUNK
KV_W = N_KV_HEADS * HEAD_DIM
RNN_BLOCK = 128
CONV_W = 4
LRU_C = 8.0
EPS = 1e-6
NEG_INF = -1e30

LANES = 128
SUBLANES = 8
MXU_COLS = 256
SLABS_PER_CAST_STEP = 4
KEY_WIN = WINDOW + 2 * CHUNK
MIXER_VALUE_TILES = 14
FFN_VALUE_TILES = 2
FFN_WIDE_VALUES = 1
LOG2E = math.log2(math.e)
ROW_B_CONV, ROW_B_A, ROW_B_X, ROW_LAMBDA = CONV_W, CONV_W + 1, CONV_W + 2, CONV_W + 3
MOD_SHIFT_MIX, MOD_SCALE_MIX, MOD_GATE_MIX, MOD_SHIFT_FFN, MOD_SCALE_FFN, MOD_GATE_FFN = range(6)

_F32 = jnp.float32
_BF16 = jnp.bfloat16


def _dot(a, b):
    return lax.dot_general(a, b, (((1,), (0,)), ((), ())), preferred_element_type=_F32)


def _dot_nt(a, b):
    return lax.dot_general(a, b, (((1,), (1,)), ((), ())), preferred_element_type=_F32)


def _rms(x):
    return x * lax.rsqrt(jnp.mean(x * x, axis=-1, keepdims=True) + EPS)


def _softplus(x):
    return jnp.maximum(x, 0.0) + jnp.log1p(jnp.exp(-jnp.abs(x)))


def _sigmoid(x):
    return 0.5 * jnp.tanh(0.5 * x) + 0.5


def _silu(x):
    h = 0.5 * x
    return h * jnp.tanh(h) + h


def _sqrt_nonneg(x):
    return jnp.exp2((0.5 * LOG2E) * jnp.log(x))


def _neg_expm1_twice(x):
    t = jnp.tanh(x)
    return (-2.0 * t) / (1.0 - t)


def _slab_cast_kernel(n_slabs, w_ref, o_ref):
    for j in range(n_slabs):
        o_ref[j] = w_ref[:, j * MXU_COLS:(j + 1) * MXU_COLS].astype(_BF16)


def _slabs(w):
    k, n = w.shape
    total = n // MXU_COLS
    per_step = math.gcd(total, SLABS_PER_CAST_STEP)
    return pl.pallas_call(
        functools.partial(_slab_cast_kernel, per_step),
        grid=(total // per_step,),
        in_specs=[pl.BlockSpec((k, per_step * MXU_COLS), lambda j: (0, j))],
        out_specs=pl.BlockSpec((per_step, k, MXU_COLS), lambda j: (j, 0, 0)),
        out_shape=jax.ShapeDtypeStruct((total, k, MXU_COLS), _BF16),
        compiler_params=pltpu.CompilerParams(dimension_semantics=("arbitrary",)),
        name="slab_cast",
    )(w)


def _gelu_tanh(x):
    c = math.sqrt(2.0 / math.pi)
    hx = 0.5 * x
    return hx + hx * jnp.tanh(x * (c + (c * 0.044715) * (x * x)))


def _split_heads(kv, scale=None):
    n = kv.shape[0]
    lo = lax.broadcasted_iota(jnp.int32, (n, LANES), 1) < HEAD_DIM
    outs = []
    for blk in range(KV_W // LANES):
        b = kv[:, blk * LANES:(blk + 1) * LANES]
        if scale is not None:
            b = b * scale
        r = pltpu.roll(b, HEAD_DIM, axis=1)
        outs.append((jnp.where(lo, b, 0.0).astype(_BF16), jnp.where(lo, 0.0, r).astype(_BF16)))
        outs.append((jnp.where(lo, r, 0.0).astype(_BF16), jnp.where(lo, 0.0, b).astype(_BF16)))
    return outs


def _ada_kernel(c_ref, w_ref, b_ref, o_ref):
    c = c_ref[...]
    a = _silu(c).astype(_BF16)
    o_ref[...] = _dot(a, w_ref[...].astype(_BF16)) + b_ref[...]


def _ada_call(c, w_ada, b_ada):
    rows, d = c.shape
    n = w_ada.shape[1]
    bn = d
    return pl.pallas_call(
        _ada_kernel,
        grid=(n // bn,),
        in_specs=[
            pl.BlockSpec((rows, d), lambda j: (0, 0)),
            pl.BlockSpec((d, bn), lambda j: (0, j)),
            pl.BlockSpec((1, bn), lambda j: (0, j)),
        ],
        out_specs=pl.BlockSpec((rows, bn), lambda j: (0, j)),
        out_shape=jax.ShapeDtypeStruct((rows, n), _F32),
        compiler_params=pltpu.CompilerParams(dimension_semantics=("arbitrary",)),
        name="ada",
    )(c, w_ada, b_ada.reshape(1, n))


def _mixer_kernel(skew, carry, ns, ls, tiles_per_seq, d_model, d_rnn, *refs):
    if skew:
        xa_ref, xb_ref, moda_ref, modb_ref = refs[:4]
        refs = refs[4:]
    else:
        xa_ref, moda_ref = xb_ref, modb_ref = refs[:2]
        refs = refs[2:]
    n_in = 0 if carry else 4
    if not carry:
        ck_ref, cv_ref, c0_ref, h0_ref = refs[:4]
    (sinks_ref, v1024_ref, v1280_ref, w_in_ref, w_rg_ref, w_ao_ref, w_ro_ref, w_out_ref,
     y_ref, ks_ref, vs_ref, cs_ref, hs_ref,
     u_ref, us_ref, q_ref, k_ref, v_ref, xr_ref, yr_ref, ga_ref, gr_ref, xc_ref, sga_ref, sgr_ref,
     kwin_ref, vwin_ref, kst_ref, vst_ref, cst_ref, bias_ref, hist_ref, hcar_ref,
     attn_ref, rnn_ref, to_strand_ref, from_strand_ref) = refs[n_in:]

    i = pl.program_id(0)
    rows = ns * ls
    lq = min(ls, 2 * CHUNK)
    n_win = ls // lq
    q_w = N_HEADS * HEAD_DIM
    n_blk = d_rnn // RNN_BLOCK
    steps = ls // SUBLANES
    if carry:
        mixed = i + tiles_per_seq - 1 if skew else i
        keep_mix = jnp.where(lax.rem(mixed, tiles_per_seq) == 0, 0.0, 1.0)
        keep_prep = jnp.where(lax.rem(i, tiles_per_seq) == 0, 0.0, 1.0)

    @pl.when(i == 0)
    def _():
        for ref in (q_ref, yr_ref, xc_ref, sga_ref, sgr_ref, kwin_ref, vwin_ref, kst_ref,
                    vst_ref, cst_ref, hist_ref, hcar_ref):
            ref[...] = jnp.zeros(ref.shape, ref.dtype)
        r = lax.broadcasted_iota(jnp.int32, (rows, rows), 0)
        c = lax.broadcasted_iota(jnp.int32, (rows, rows), 1)
        seg, rr = (r // ls) * ls, r % ls
        to_strand_ref[...] = jnp.where(
            c == seg + (rr % SUBLANES) * steps + rr // SUBLANES, 1.0, 0.0).astype(_BF16)
        from_strand_ref[...] = jnp.where(
            c == seg + (rr % steps) * SUBLANES + rr // steps, 1.0, 0.0).astype(_BF16)
        t = lax.broadcasted_iota(jnp.int32, (lq, KEY_WIN), 0)
        j = lax.broadcasted_iota(jnp.int32, (lq, KEY_WIN), 1)
        dist = jnp.abs(t + WINDOW - j).astype(_F32)
        qc = t // CHUNK
        kc = j // CHUNK
        base = jnp.where(kc >= qc, jnp.where(kc <= qc + N_BACK, 0.0, NEG_INF), NEG_INF)
        for h in range(N_KV_HEADS):
            for g in range(GROUP):
                slope = 2.0 ** (-8.0 * (h * GROUP + g + 1) / N_HEADS)
                bias_ref[h, g % 2, (g // 2) * lq:(g // 2 + 1) * lq, :] = LOG2E * (base - slope * dist)

    def publish_state():
        ks_ref[...] = kst_ref[...]
        vs_ref[...] = vst_ref[...]
        cs_ref[...] = cst_ref[...]

    def norm_input():
        xn = _rms(xa_ref[...])
        for s in range(ns):
            sl = slice(s * ls, (s + 1) * ls)
            gain = v1024_ref[0:1, :] * (1.0 + moda_ref[s, MOD_SCALE_MIX:MOD_SCALE_MIX + 1, :])
            u_ref[sl, :] = (xn[sl] * gain + moda_ref[s, MOD_SHIFT_MIX:MOD_SHIFT_MIX + 1, :]).astype(_BF16)
        us_ref[...] = _dot(to_strand_ref[...], u_ref[...]).astype(_BF16)

    proj = []
    off = 0
    for src, ref, width, free_after in (
            (u_ref, q_ref, q_w, lambda c: 2 * (c // MXU_COLS) + 1),
            (u_ref, k_ref, KV_W, lambda c: -1), (u_ref, v_ref, KV_W, lambda c: -1),
            (us_ref, xr_ref, d_rnn, lambda c: -1),
            (us_ref, yr_ref, d_rnn, lambda c: (c + MXU_COLS - 1) // RNN_BLOCK),
            (u_ref, ga_ref, d_model, lambda c: -1), (u_ref, gr_ref, d_model, lambda c: -1)):
        for c in range(0, width, MXU_COLS):
            proj.append((free_after(c), src, ref, c, off + c))
        off += width
    proj.sort(key=lambda p: p[0])

    def emit_proj(slot, slots_left):
        ready = [p for p in proj if p[0] <= slot]
        quota = -(-len(proj) // slots_left) if slots_left else len(proj)
        for p in ready[:quota]:
            proj.remove(p)
            _, src, ref, c, wc = p
            ref[:, c:c + MXU_COLS] = _dot(src[...], w_in_ref[wc // MXU_COLS]).astype(ref.dtype)

    lo_sel = lax.broadcasted_iota(jnp.int32, (lq, LANES), 1) < HEAD_DIM
    if carry:
        kcol = lax.broadcasted_iota(jnp.int32, (1, KEY_WIN), 1)
        hist_bias = jnp.where(kcol < WINDOW, NEG_INF, 0.0).astype(_F32) * (1.0 - keep_mix)

    def attend_scores(s, w, h):
        r0 = s * ls + w * lq
        kw = slice(w * lq, w * lq + KEY_WIN)
        c0 = h * GROUP * HEAD_DIM
        qab = jnp.concatenate([q_ref[r0:r0 + lq, c0:c0 + LANES],
                               q_ref[r0:r0 + lq, c0 + LANES:c0 + 2 * LANES]], axis=0)
        ps, inv = [[], []], [None] * GROUP
        for half in range(2):
            sc = _dot_nt(qab, kwin_ref[s, h, half, kw, :]) + bias_ref[h, half]
            if carry and w == 0:
                sc = sc + hist_bias
            for pair in range(2):
                g = 2 * pair + half
                sg = sc[pair * lq:(pair + 1) * lq]
                sink = LOG2E * sinks_ref[h * GROUP + g]
                mg = jnp.maximum(jnp.max(sg, axis=-1, keepdims=True), sink)
                pg = jnp.exp2(sg - mg)
                inv[g] = 1.0 / (jnp.sum(pg, axis=-1, keepdims=True) + jnp.exp2(sink - mg))
                ps[half].append(pg.astype(_BF16))
        return s, h, r0, kw, c0, ps, inv

    def attend_values(s, h, r0, kw, c0, ps, inv):
        o = (_dot(jnp.concatenate(ps[0], axis=0), vwin_ref[s, h, 0, kw, :])
             + _dot(jnp.concatenate(ps[1], axis=0), vwin_ref[s, h, 1, kw, :]))
        for pair in range(2):
            norm = jnp.where(lo_sel, inv[2 * pair], inv[2 * pair + 1])
            attn_ref[r0:r0 + lq, c0 + pair * LANES:c0 + (pair + 1) * LANES] = (
                o[pair * lq:(pair + 1) * lq] * norm).astype(_BF16)

    units = [(s, w, h) for h in range(N_KV_HEADS) for s in range(ns) for w in range(n_win)]
    units_per_slot = len(units) // (2 * N_KV_HEADS)

    sub = lax.broadcasted_iota(jnp.int32, (SUBLANES, RNN_BLOCK), 0)

    def recur_gates(n):
        half_xc = 0.5 * xc_ref[:, n * RNN_BLOCK:(n + 1) * RNN_BLOCK]
        return half_xc, _dot(half_xc.astype(_BF16), w_rg_ref[n])

    def recur(n, half_xc, half_gates):
        cs_ = slice(n * RNN_BLOCK, (n + 1) * RNN_BLOCK)
        half_c = (-0.5 * LRU_C) * _softplus(-v1280_ref[ROW_LAMBDA:ROW_LAMBDA + 1, cs_])
        half_ba = 0.5 * v1280_ref[ROW_B_A:ROW_B_A + 1, cs_]
        half_bx = 0.5 * v1280_ref[ROW_B_X:ROW_B_X + 1, cs_]
        log_a = half_c * jnp.tanh(half_gates[:, :RNN_BLOCK] + half_ba) + half_c
        gated_x = jnp.tanh(half_gates[:, RNN_BLOCK:] + half_bx) * half_xc + half_xc
        a = jnp.exp(log_a)
        b = _sqrt_nonneg(_neg_expm1_twice(log_a)) * gated_x
        hs = []
        for s in range(ns):
            sl = slice(s * ls, (s + 1) * ls)
            a3 = a[sl].reshape(steps, SUBLANES, RNN_BLOCK)
            b3 = b[sl].reshape(steps, SUBLANES, RNN_BLOCK)
            hz, ap = [b3[0]], [a3[0]]
            for jb in range(1, steps):
                hz.append(a3[jb] * hz[-1] + b3[jb])
                ap.append(a3[jb] * ap[-1])
            ae, he = ap[-1], hz[-1]
            d = 1
            while d < SUBLANES:
                keep = sub >= d
                he = he + ae * jnp.where(keep, pltpu.roll(he, d, axis=0), 0.0)
                ae = ae * jnp.where(keep, pltpu.roll(ae, d, axis=0), 1.0)
                d *= 2
            hprev = hcar_ref[0:1, cs_] * keep_mix if carry else h0_ref[s, 0:1, cs_]
            after = he + ae * hprev
            before = jnp.where(sub == 0, hprev, pltpu.roll(after, 1, axis=0))
            hs.extend(hz[jb] + ap[jb] * before for jb in range(steps))
            hlast = after[SUBLANES - 1:SUBLANES, :]
            hs_ref[s, 0:1, cs_] = hlast
            if carry:
                hcar_ref[0:1, cs_] = hlast
        hfull = jnp.concatenate(hs, axis=0)
        rnn_ref[:, cs_] = (hfull * _gelu_tanh(yr_ref[:, cs_])).astype(_BF16)

    def next_scores():
        return [attend_scores(*units.pop(0)) for _ in range(min(units_per_slot, len(units)))]

    def main_loop(with_proj):
        scored = next_scores()
        gates = recur_gates(0)
        ao = None
        for n in range(n_blk):
            scored_next = next_scores()
            if with_proj:
                emit_proj(n - 1, n_blk - n)
            gates_next = recur_gates(n + 1) if n + 1 < n_blk else None
            recur(n, *gates)
            for unit in scored:
                attend_values(*unit)
            if ao is None and not (scored or scored_next or units):
                ao = attention_projection()
            scored, gates = scored_next, gates_next
        for unit in scored:
            attend_values(*unit)
        if with_proj:
            emit_proj(n_blk - 1, 1)
        assert not units
        return ao if ao is not None else attention_projection()

    n_slab = d_model // MXU_COLS

    def attention_projection():
        return [_dot(attn_ref[...], w_ao_ref[j]) for j in range(n_slab)]

    def recurrent_projection():
        rnn = _dot(from_strand_ref[...], rnn_ref[...]).astype(_BF16)
        return [_dot(rnn, w_ro_ref[j]) for j in range(n_slab)]

    def prepare_window_and_conv():
        qk_scale = LOG2E * HEAD_DIM ** -0.5
        if carry:
            kwin_ref[0, :, :, 0:WINDOW, :] = kwin_ref[0, :, :, ls:ls + WINDOW, :]
            vwin_ref[0, :, :, 0:WINDOW, :] = vwin_ref[0, :, :, ls:ls + WINDOW, :]
        for s in range(ns):
            sl = slice(s * ls, (s + 1) * ls)
            kd = _split_heads(k_ref[sl, :], qk_scale)
            vd = _split_heads(v_ref[sl, :])
            for h in range(N_KV_HEADS):
                for half in range(2):
                    kwin_ref[s, h, half, WINDOW:WINDOW + ls, :] = kd[h][half]
                    vwin_ref[s, h, half, WINDOW:WINDOW + ls, :] = vd[h][half]
            if carry:
                kst_ref[0] = k_ref[rows - WINDOW:rows, :]
                vst_ref[0] = v_ref[rows - WINDOW:rows, :]
            else:
                hk = _split_heads(ck_ref[s], qk_scale)
                hv = _split_heads(cv_ref[s])
                pad = KEY_WIN - WINDOW - ls
                for h in range(N_KV_HEADS):
                    for half in range(2):
                        kwin_ref[s, h, half, 0:WINDOW, :] = hk[h][half]
                        vwin_ref[s, h, half, 0:WINDOW, :] = hv[h][half]
                        kwin_ref[s, h, half, WINDOW + ls:KEY_WIN, :] = jnp.zeros((pad, LANES), _BF16)
                        vwin_ref[s, h, half, WINDOW + ls:KEY_WIN, :] = jnp.zeros((pad, LANES), _BF16)
                kst_ref[s, 0:WINDOW - ls, :] = ck_ref[s, ls:WINDOW, :]
                kst_ref[s, WINDOW - ls:WINDOW, :] = k_ref[sl, :]
                vst_ref[s, 0:WINDOW - ls, :] = cv_ref[s, ls:WINDOW, :]
                vst_ref[s, WINDOW - ls:WINDOW, :] = v_ref[sl, :]

            hist = c0_ref[s] if not carry else hist_ref[0] * keep_prep
            first = lax.broadcasted_iota(jnp.int32, (SUBLANES, LANES), 0) == 0
            for col in range(0, d_rnn, LANES):
                lanes = slice(col, col + LANES)
                taps = [jnp.broadcast_to(v1280_ref[r:r + 1, lanes], (SUBLANES, LANES))
                        for r in range(CONV_W + 1)]
                blocks = [xr_ref[s * ls + jb * SUBLANES:s * ls + (jb + 1) * SUBLANES, lanes]
                          for jb in range(steps)]

                def earlier(jb, d):
                    if jb >= d:
                        return blocks[jb - d]
                    row = CONV_W - 1 + jb - d
                    prev_strand = pltpu.roll(blocks[steps + jb - d], 1, axis=0)
                    return jnp.where(first, hist[row:row + 1, lanes], prev_strand)

                for jb in range(steps):
                    acc = taps[ROW_B_CONV] + taps[CONV_W - 1] * blocks[jb]
                    for d in range(1, CONV_W):
                        acc = acc + taps[CONV_W - 1 - d] * earlier(jb, d)
                    xc_ref[s * ls + jb * SUBLANES:s * ls + (jb + 1) * SUBLANES, lanes] = acc
            tail = jnp.concatenate(
                [xr_ref[s * ls + (steps - d + 1) * SUBLANES - 1:s * ls + (steps - d + 1) * SUBLANES, :]
                 for d in range(CONV_W - 1, 0, -1)], axis=0)
            cst_ref[s] = tail
            if carry:
                hist_ref[s] = tail

    def output_projection(ao, ro):
        merged = jnp.concatenate(
            [(sga_ref[:, j * MXU_COLS:(j + 1) * MXU_COLS] * ao[j]
              + sgr_ref[:, j * MXU_COLS:(j + 1) * MXU_COLS] * ro[j]).astype(_BF16)
             for j in range(n_slab)], axis=1)
        return jnp.concatenate([_dot(merged, w_out_ref[j]) for j in range(n_slab)], axis=1)

    def gate_sigmoids():
        sga_ref[...] = _sigmoid(ga_ref[...])
        sgr_ref[...] = _sigmoid(gr_ref[...])

    def residual(mo):
        yn = _rms(mo)
        for s in range(ns):
            sl = slice(s * ls, (s + 1) * ls)
            gain = v1024_ref[1:2, :] * modb_ref[s, MOD_GATE_MIX:MOD_GATE_MIX + 1, :]
            y_ref[sl, :] = xb_ref[sl, :] + gain * yn[sl]

    if skew:
        publish_state()
        norm_input()
        ao = main_loop(with_proj=True)
        ro = recurrent_projection()
        prepare_window_and_conv()
        mo = output_projection(ao, ro)
        gate_sigmoids()
        residual(mo)
        emit_proj(n_blk, 0)
    else:
        norm_input()
        emit_proj(n_blk, 0)
        prepare_window_and_conv()
        gate_sigmoids()
        ao = main_loop(with_proj=False)
        residual(output_projection(ao, recurrent_projection()))
        publish_state()
    assert not proj


def _whole(memory_space=pltpu.VMEM):
    return pl.BlockSpec(memory_space=memory_space)


def _nbytes(shape, dtype):
    return math.prod(shape) * jnp.dtype(dtype).itemsize


def _vmem_limit(resident, windows, scratch, value_tiles, rows, d_model):
    total = sum(_nbytes(a.shape, a.dtype) for a in resident)
    total += 2 * sum(_nbytes(shape, dtype) for shape, dtype in windows)
    total += sum(_nbytes(s.shape, s.dtype) for s in scratch)
    return total + value_tiles * _nbytes((rows, d_model), _F32)


def _mixer_call(x, mod, hist, sinks, v1024, v1280, w_in, w_rg, w_ao, w_ro, w_out, *, ns, ls, skew):
    n_seq, seq_len, d_model = x.shape
    d_rnn = v1280.shape[1]
    carry = hist is None
    rows = ns * ls
    if carry:
        assert ns == 1 and seq_len % ls == 0 and ls % (2 * CHUNK) == 0
        tiles_per_seq = seq_len // ls
    else:
        assert seq_len == ls == CHUNK and n_seq % ns == 0
        tiles_per_seq = 1
    n_tiles = n_seq * seq_len // rows
    lq = min(ls, 2 * CHUNK)
    xf = x.reshape(n_seq * seq_len, d_model)

    if skew:
        tile_a = lambda i: jnp.minimum(i, n_tiles - 1)
        tile_b = lambda i: jnp.maximum(i - 1, 0)
    else:
        tile_a = tile_b = lambda i: i
    seq_a = lambda i: (tile_a(i) // tiles_per_seq, 0, 0)
    seq_b = lambda i: (tile_b(i) // tiles_per_seq, 0, 0)

    in_specs = [pl.BlockSpec((rows, d_model), lambda i: (tile_a(i), 0)),
                pl.BlockSpec((ns, 6, d_model), seq_a)]
    args = [xf, mod]
    if skew:
        in_specs = [in_specs[0], pl.BlockSpec((rows, d_model), lambda i: (tile_b(i), 0)),
                    in_specs[1], pl.BlockSpec((ns, 6, d_model), seq_b)]
        args = [xf, xf, mod, mod]
    if not carry:
        ck, cv, c0, h0 = hist
        in_specs += [pl.BlockSpec((ns, WINDOW, KV_W), seq_a),
                     pl.BlockSpec((ns, WINDOW, KV_W), seq_a),
                     pl.BlockSpec((ns, CONV_W - 1, d_rnn), seq_a),
                     pl.BlockSpec((ns, 1, d_rnn), seq_b)]
        args += [ck, cv, c0, h0]
    in_specs += [_whole(pltpu.SMEM)] + [_whole()] * 7
    args += [sinks, v1024, v1280, w_in, w_rg, w_ao, w_ro, w_out]

    out_shape = (jax.ShapeDtypeStruct((n_seq * seq_len, d_model), _F32),
                 jax.ShapeDtypeStruct((n_seq, WINDOW, KV_W), _F32),
                 jax.ShapeDtypeStruct((n_seq, WINDOW, KV_W), _F32),
                 jax.ShapeDtypeStruct((n_seq, CONV_W - 1, d_rnn), _F32),
                 jax.ShapeDtypeStruct((n_seq, 1, d_rnn), _F32))
    out_specs = (pl.BlockSpec((rows, d_model), lambda i: (tile_b(i), 0)),
                 pl.BlockSpec((ns, WINDOW, KV_W), seq_b),
                 pl.BlockSpec((ns, WINDOW, KV_W), seq_b),
                 pl.BlockSpec((ns, CONV_W - 1, d_rnn), seq_b),
                 pl.BlockSpec((ns, 1, d_rnn), seq_b))
    win_rows = WINDOW + max(ls, 2 * CHUNK)
    scratch = [pltpu.VMEM((rows, d_model), _BF16),
               pltpu.VMEM((rows, d_model), _BF16),
               pltpu.VMEM((rows, N_HEADS * HEAD_DIM), _BF16),
               pltpu.VMEM((rows, KV_W), _F32),
               pltpu.VMEM((rows, KV_W), _F32),
               pltpu.VMEM((rows, d_rnn), _F32),
               pltpu.VMEM((rows, d_rnn), _F32),
               pltpu.VMEM((rows, d_model), _F32),
               pltpu.VMEM((rows, d_model), _F32),
               pltpu.VMEM((rows, d_rnn), _F32),
               pltpu.VMEM((rows, d_model), _F32),
               pltpu.VMEM((rows, d_model), _F32),
               pltpu.VMEM((ns, N_KV_HEADS, 2, win_rows, LANES), _BF16),
               pltpu.VMEM((ns, N_KV_HEADS, 2, win_rows, LANES), _BF16),
               pltpu.VMEM((ns, WINDOW, KV_W), _F32),
               pltpu.VMEM((ns, WINDOW, KV_W), _F32),
               pltpu.VMEM((ns, CONV_W - 1, d_rnn), _F32),
               pltpu.VMEM((N_KV_HEADS, 2, 2 * lq, KEY_WIN), _F32),
               pltpu.VMEM((ns, CONV_W - 1, d_rnn), _F32),
               pltpu.VMEM((1, d_rnn), _F32),
               pltpu.VMEM((rows, N_HEADS * HEAD_DIM), _BF16),
               pltpu.VMEM((rows, d_rnn), _BF16),
               pltpu.VMEM((rows, rows), _BF16),
               pltpu.VMEM((rows, rows), _BF16)]
    kern = functools.partial(_mixer_kernel, skew, carry, ns, ls, tiles_per_seq, d_model, d_rnn)
    windows = [(spec.block_shape, _F32) for spec in (*in_specs, *out_specs)
               if spec.block_shape is not None]
    vmem_limit = _vmem_limit((v1024, v1280, w_in, w_rg, w_ao, w_ro, w_out), windows, scratch,
                             MIXER_VALUE_TILES, rows, d_model)
    return pl.pallas_call(
        kern,
        grid=(n_tiles + (1 if skew else 0),),
        in_specs=in_specs,
        out_specs=out_specs,
        out_shape=out_shape,
        scratch_shapes=scratch,
        compiler_params=pltpu.CompilerParams(dimension_semantics=("arbitrary",),
                                             vmem_limit_bytes=vmem_limit),
        name="mixer_stream" if carry else "mixer_step",
    )(*args)


def _ffn_kernel(ns, ls, x_ref, mod_ref, v1024_ref, wg_ref, wu_ref, wd_ref, y_ref, u_ref):
    x = x_ref[...]
    xn = _rms(x)
    for s in range(ns):
        sl = slice(s * ls, (s + 1) * ls)
        gain = v1024_ref[0:1, :] * (1.0 + mod_ref[s, MOD_SCALE_FFN:MOD_SCALE_FFN + 1, :])
        u_ref[sl, :] = (xn[sl] * gain + mod_ref[s, MOD_SHIFT_FFN:MOD_SHIFT_FFN + 1, :]).astype(_BF16)
    u = u_ref[...]
    g = _dot(u, wg_ref[...].astype(_BF16))
    up = _dot(u, wu_ref[...].astype(_BF16))
    hmid = (_silu(g) * up).astype(_BF16)
    yn = _rms(_dot(hmid, wd_ref[...].astype(_BF16)))
    for s in range(ns):
        sl = slice(s * ls, (s + 1) * ls)
        gain = v1024_ref[1:2, :] * mod_ref[s, MOD_GATE_FFN:MOD_GATE_FFN + 1, :]
        y_ref[sl, :] = x[sl] + gain * yn[sl]


def _ffn_call(xf, mod, v1024, wg, wu, wd, *, ns, ls, seq_len):
    n_rows, d_model = xf.shape
    rows = ns * ls
    if ns == 1:
        tiles_per_seq = seq_len // ls
        seq_of = lambda i: (i // tiles_per_seq, 0, 0)
    else:
        assert ls == seq_len
        seq_of = lambda i: (i, 0, 0)
    scratch = [pltpu.VMEM((rows, d_model), _BF16)]
    windows = [((rows, d_model), _F32), ((ns, 6, d_model), _F32), ((rows, d_model), _F32)]
    value_tiles = FFN_VALUE_TILES + FFN_WIDE_VALUES * pl.cdiv(wg.shape[1], d_model)
    return pl.pallas_call(
        functools.partial(_ffn_kernel, ns, ls),
        grid=(n_rows // rows,),
        in_specs=[pl.BlockSpec((rows, d_model), lambda i: (i, 0)),
                  pl.BlockSpec((ns, 6, d_model), seq_of),
                  _whole(), _whole(), _whole(), _whole()],
        out_specs=pl.BlockSpec((rows, d_model), lambda i: (i, 0)),
        out_shape=jax.ShapeDtypeStruct((n_rows, d_model), _F32),
        scratch_shapes=scratch,
        compiler_params=pltpu.CompilerParams(
            dimension_semantics=("arbitrary",),
            vmem_limit_bytes=_vmem_limit((v1024, wg, wu, wd), windows, scratch, value_tiles,
                                         rows, d_model)),
        name="ffn",
    )(xf, mod, v1024, wg, wu, wd)


MIX_TILE = 256
MIX_SEQS = 4
SKEW_PROMPT = True
SKEW_SAMPLE = False
FFN_TILE = 512
FFN_SEQS = 8


def kernel(x_prompt, x_sample, c_prompt, c_sample, cache_k, cache_v, state_conv, state_h, w_ada, b_ada, g_pre_mix, g_post_mix, w_in, attn_sinks, w_conv, b_conv, w_rg_a, b_rg_a, w_rg_x, b_rg_x, rg_lambda, w_attn_o, w_rnn_o, w_out, g_pre_ffn, g_post_ffn, w_ffn_gate, w_ffn_up, w_ffn_down):
    depth = w_in.shape[0]
    assert depth == 1
    bp, sp, d_model = x_prompt.shape
    bs, ss, _ = x_sample.shape
    d_rnn = w_conv.shape[-1]
    l = 0

    n_c = bp + bs
    pad = (-n_c) % 16
    c_all = jnp.concatenate([c_prompt, c_sample, jnp.zeros((pad, d_model), _F32)], axis=0)
    mod = _ada_call(c_all, w_ada[l], b_ada[l])[:n_c].reshape(n_c, 6, d_model)
    mod_p, mod_s = mod[:bp], mod[bp:]

    bf = lambda w: w.astype(_BF16)
    v1024_mix = jnp.stack([g_pre_mix[l], g_post_mix[l]])
    v1024_ffn = jnp.stack([g_pre_ffn[l], g_post_ffn[l]])
    v1280 = jnp.concatenate([w_conv[l], b_conv[l][None], b_rg_a[l][None], b_rg_x[l][None],
                             rg_lambda[l][None]], axis=0)
    w_rg = bf(jnp.concatenate([w_rg_a[l], w_rg_x[l]], axis=-1))
    mix_w = (attn_sinks[l], v1024_mix, v1280, _slabs(w_in[l]), w_rg, _slabs(w_attn_o[l]),
             _slabs(w_rnn_o[l]), _slabs(w_out[l]))
    ffn_w = (v1024_ffn, w_ffn_gate[l], w_ffn_up[l], w_ffn_down[l])

    xp1, kp, vp, cp, hp = _mixer_call(x_prompt, mod_p, None, *mix_w, ns=1, ls=MIX_TILE,
                                      skew=SKEW_PROMPT)
    hist = (cache_k[l].reshape(bs, WINDOW, KV_W), cache_v[l].reshape(bs, WINDOW, KV_W),
            state_conv[l], state_h[l].reshape(bs, 1, d_rnn))
    xs1, ks, vs, cs, hs = _mixer_call(x_sample, mod_s, hist, *mix_w, ns=MIX_SEQS, ls=ss,
                                      skew=SKEW_SAMPLE)

    yp = _ffn_call(xp1, mod_p, *ffn_w, ns=1, ls=FFN_TILE, seq_len=sp)
    ys = _ffn_call(xs1, mod_s, *ffn_w, ns=FFN_SEQS, ls=ss, seq_len=ss)

    kv_shape = lambda n: (1, n, WINDOW, N_KV_HEADS, HEAD_DIM)
    return (yp.reshape(bp, sp, d_model), ys.reshape(bs, ss, d_model),
            kp.reshape(kv_shape(bp)), vp.reshape(kv_shape(bp)), cp[None], hp.reshape(1, bp, d_rnn),
            ks.reshape(kv_shape(bs)), vs.reshape(kv_shape(bs)), cs[None], hs.reshape(1, bs, d_rnn))
```

```python
import functools
import math

import jax
import jax.numpy as jnp
from jax import lax
from jax.experimental import pallas as pl
from jax.experimental.pallas import tpu as pltpu

CHUNK = 64
N_HEADS = 16
N_KV_HEADS = 4
HEAD_DIM = 64
GROUP = N_HEADS // N_KV_HEADS
WINDOW = 128
N_BACK = WINDOW // CHUNK
KV_W = N_KV_HEADS * HEAD_DIM
RNN_BLOCK = 128
CONV_W = 4
LRU_C = 8.0
EPS = 1e-6
NEG_INF = -1e30

LANES = 128
SUBLANES = 8
MXU_COLS = 256
SLABS_PER_CAST_STEP = 4
KEY_WIN = WINDOW + 2 * CHUNK
MIXER_VALUE_TILES = 14
FFN_VALUE_TILES = 2
FFN_WIDE_VALUES = 1
LOG2E = math.log2(math.e)
ROW_B_CONV, ROW_B_A, ROW_B_X, ROW_LAMBDA = CONV_W, CONV_W + 1, CONV_W + 2, CONV_W + 3
MOD_SHIFT_MIX, MOD_SCALE_MIX, MOD_GATE_MIX, MOD_SHIFT_FFN, MOD_SCALE_FFN, MOD_GATE_FFN = range(6)

_F32 = jnp.float32
_BF16 = jnp.bfloat16


def _dot(a, b):
    return lax.dot_general(a, b, (((1,), (0,)), ((), ())), preferred_element_type=_F32)


def _dot_nt(a, b):
    return lax.dot_general(a, b, (((1,), (1,)), ((), ())), preferred_element_type=_F32)


def _rms(x):
    return x * lax.rsqrt(jnp.mean(x * x, axis=-1, keepdims=True) + EPS)


def _softplus(x):
    return jnp.maximum(x, 0.0) + jnp.log1p(jnp.exp(-jnp.abs(x)))


def _sigmoid(x):
    return 0.5 * jnp.tanh(0.5 * x) + 0.5


def _silu(x):
    h = 0.5 * x
    return h * jnp.tanh(h) + h


def _sqrt_nonneg(x):
    return jnp.exp2((0.5 * LOG2E) * jnp.log(x))


def _neg_expm1_twice(x):
    t = jnp.tanh(x)
    return (-2.0 * t) / (1.0 - t)


def _slab_cast_kernel(n_slabs, w_ref, o_ref):
    for j in range(n_slabs):
        o_ref[j] = w_ref[:, j * MXU_COLS:(j + 1) * MXU_COLS].astype(_BF16)


def _slabs(w):
    k, n = w.shape
    total = n // MXU_COLS
    per_step = math.gcd(total, SLABS_PER_CAST_STEP)
    return pl.pallas_call(
        functools.partial(_slab_cast_kernel, per_step),
        grid=(total // per_step,),
        in_specs=[pl.BlockSpec((k, per_step * MXU_COLS), lambda j: (0, j))],
        out_specs=pl.BlockSpec((per_step, k, MXU_COLS), lambda j: (j, 0, 0)),
        out_shape=jax.ShapeDtypeStruct((total, k, MXU_COLS), _BF16),
        compiler_params=pltpu.CompilerParams(dimension_semantics=("arbitrary",)),
        name="slab_cast",
    )(w)


def _gelu_tanh(x):
    c = math.sqrt(2.0 / math.pi)
    hx = 0.5 * x
    return hx + hx * jnp.tanh(x * (c + (c * 0.044715) * (x * x)))


def _split_heads(kv, scale=None):
    n = kv.shape[0]
    lo = lax.broadcasted_iota(jnp.int32, (n, LANES), 1) < HEAD_DIM
    outs = []
    for blk in range(KV_W // LANES):
        b = kv[:, blk * LANES:(blk + 1) * LANES]
        if scale is not None:
            b = b * scale
        r = pltpu.roll(b, HEAD_DIM, axis=1)
        outs.append((jnp.where(lo, b, 0.0).astype(_BF16), jnp.where(lo, 0.0, r).astype(_BF16)))
        outs.append((jnp.where(lo, r, 0.0).astype(_BF16), jnp.where(lo, 0.0, b).astype(_BF16)))
    return outs


def _ada_kernel(c_ref, w_ref, b_ref, o_ref):
    c = c_ref[...]
    a = _silu(c).astype(_BF16)
    o_ref[...] = _dot(a, w_ref[...].astype(_BF16)) + b_ref[...]


def _ada_call(c, w_ada, b_ada):
    rows, d = c.shape
    n = w_ada.shape[1]
    bn = d
    return pl.pallas_call(
        _ada_kernel,
        grid=(n // bn,),
        in_specs=[
            pl.BlockSpec((rows, d), lambda j: (0, 0)),
            pl.BlockSpec((d, bn), lambda j: (0, j)),
            pl.BlockSpec((1, bn), lambda j: (0, j)),
        ],
        out_specs=pl.BlockSpec((rows, bn), lambda j: (0, j)),
        out_shape=jax.ShapeDtypeStruct((rows, n), _F32),
        compiler_params=pltpu.CompilerParams(dimension_semantics=("arbitrary",)),
        name="ada",
    )(c, w_ada, b_ada.reshape(1, n))


def _mixer_kernel(skew, carry, ns, ls, tiles_per_seq, d_model, d_rnn, *refs):
    if skew:
        xa_ref, xb_ref, moda_ref, modb_ref = refs[:4]
        refs = refs[4:]
    else:
        xa_ref, moda_ref = xb_ref, modb_ref = refs[:2]
        refs = refs[2:]
    n_in = 0 if carry else 4
    if not carry:
        ck_ref, cv_ref, c0_ref, h0_ref = refs[:4]
    (sinks_ref, v1024_ref, v1280_ref, w_in_ref, w_rg_ref, w_ao_ref, w_ro_ref, w_out_ref,
     y_ref, ks_ref, vs_ref, cs_ref, hs_ref,
     u_ref, us_ref, q_ref, k_ref, v_ref, xr_ref, yr_ref, ga_ref, gr_ref, xc_ref, sga_ref, sgr_ref,
     kwin_ref, vwin_ref, kst_ref, vst_ref, cst_ref, bias_ref, hist_ref, hcar_ref,
     attn_ref, rnn_ref, to_strand_ref, from_strand_ref) = refs[n_in:]

    i = pl.program_id(0)
    rows = ns * ls
    lq = min(ls, 2 * CHUNK)
    n_win = ls // lq
    q_w = N_HEADS * HEAD_DIM
    n_blk = d_rnn // RNN_BLOCK
    steps = ls // SUBLANES
    if carry:
        mixed = i + tiles_per_seq - 1 if skew else i
        keep_mix = jnp.where(lax.rem(mixed, tiles_per_seq) == 0, 0.0, 1.0)
        keep_prep = jnp.where(lax.rem(i, tiles_per_seq) == 0, 0.0, 1.0)

    @pl.when(i == 0)
    def _():
        for ref in (q_ref, yr_ref, xc_ref, sga_ref, sgr_ref, kwin_ref, vwin_ref, kst_ref,
                    vst_ref, cst_ref, hist_ref, hcar_ref):
            ref[...] = jnp.zeros(ref.shape, ref.dtype)
        r = lax.broadcasted_iota(jnp.int32, (rows, rows), 0)
        c = lax.broadcasted_iota(jnp.int32, (rows, rows), 1)
        seg, rr = (r // ls) * ls, r % ls
        to_strand_ref[...] = jnp.where(
            c == seg + (rr % SUBLANES) * steps + rr // SUBLANES, 1.0, 0.0).astype(_BF16)
        from_strand_ref[...] = jnp.where(
            c == seg + (rr % steps) * SUBLANES + rr // steps, 1.0, 0.0).astype(_BF16)
        t = lax.broadcasted_iota(jnp.int32, (lq, KEY_WIN), 0)
        j = lax.broadcasted_iota(jnp.int32, (lq, KEY_WIN), 1)
        dist = jnp.abs(t + WINDOW - j).astype(_F32)
        qc = t // CHUNK
        kc = j // CHUNK
        base = jnp.where(kc >= qc, jnp.where(kc <= qc + N_BACK, 0.0, NEG_INF), NEG_INF)
        for h in range(N_KV_HEADS):
            for g in range(GROUP):
                slope = 2.0 ** (-8.0 * (h * GROUP + g + 1) / N_HEADS)
                bias_ref[h, g % 2, (g // 2) * lq:(g // 2 + 1) * lq, :] = LOG2E * (base - slope * dist)

    def publish_state():
        ks_ref[...] = kst_ref[...]
        vs_ref[...] = vst_ref[...]
        cs_ref[...] = cst_ref[...]

    def norm_input():
        xn = _rms(xa_ref[...])
        for s in range(ns):
            sl = slice(s * ls, (s + 1) * ls)
            gain = v1024_ref[0:1, :] * (1.0 + moda_ref[s, MOD_SCALE_MIX:MOD_SCALE_MIX + 1, :])
            u_ref[sl, :] = (xn[sl] * gain + moda_ref[s, MOD_SHIFT_MIX:MOD_SHIFT_MIX + 1, :]).astype(_BF16)
        us_ref[...] = _dot(to_strand_ref[...], u_ref[...]).astype(_BF16)

    proj = []
    off = 0
    for src, ref, width, free_after in (
            (u_ref, q_ref, q_w, lambda c: 2 * (c // MXU_COLS) + 1),
            (u_ref, k_ref, KV_W, lambda c: -1), (u_ref, v_ref, KV_W, lambda c: -1),
            (us_ref, xr_ref, d_rnn, lambda c: -1),
            (us_ref, yr_ref, d_rnn, lambda c: (c + MXU_COLS - 1) // RNN_BLOCK),
            (u_ref, ga_ref, d_model, lambda c: -1), (u_ref, gr_ref, d_model, lambda c: -1)):
        for c in range(0, width, MXU_COLS):
            proj.append((free_after(c), src, ref, c, off + c))
        off += width
    proj.sort(key=lambda p: p[0])

    def emit_proj(slot, slots_left):
        ready = [p for p in proj if p[0] <= slot]
        quota = -(-len(proj) // slots_left) if slots_left else len(proj)
        for p in ready[:quota]:
            proj.remove(p)
            _, src, ref, c, wc = p
            ref[:, c:c + MXU_COLS] = _dot(src[...], w_in_ref[wc // MXU_COLS]).astype(ref.dtype)

    lo_sel = lax.broadcasted_iota(jnp.int32, (lq, LANES), 1) < HEAD_DIM
    if carry:
        kcol = lax.broadcasted_iota(jnp.int32, (1, KEY_WIN), 1)
        hist_bias = jnp.where(kcol < WINDOW, NEG_INF, 0.0).astype(_F32) * (1.0 - keep_mix)

    def attend_scores(s, w, h):
        r0 = s * ls + w * lq
        kw = slice(w * lq, w * lq + KEY_WIN)
        c0 = h * GROUP * HEAD_DIM
        qab = jnp.concatenate([q_ref[r0:r0 + lq, c0:c0 + LANES],
                               q_ref[r0:r0 + lq, c0 + LANES:c0 + 2 * LANES]], axis=0)
        ps, inv = [[], []], [None] * GROUP
        for half in range(2):
            sc = _dot_nt(qab, kwin_ref[s, h, half, kw, :]) + bias_ref[h, half]
            if carry and w == 0:
                sc = sc + hist_bias
            for pair in range(2):
                g = 2 * pair + half
                sg = sc[pair * lq:(pair + 1) * lq]
                sink = LOG2E * sinks_ref[h * GROUP + g]
                mg = jnp.maximum(jnp.max(sg, axis=-1, keepdims=True), sink)
                pg = jnp.exp2(sg - mg)
                inv[g] = 1.0 / (jnp.sum(pg, axis=-1, keepdims=True) + jnp.exp2(sink - mg))
                ps[half].append(pg.astype(_BF16))
        return s, h, r0, kw, c0, ps, inv

    def attend_values(s, h, r0, kw, c0, ps, inv):
        o = (_dot(jnp.concatenate(ps[0], axis=0), vwin_ref[s, h, 0, kw, :])
             + _dot(jnp.concatenate(ps[1], axis=0), vwin_ref[s, h, 1, kw, :]))
        for pair in range(2):
            norm = jnp.where(lo_sel, inv[2 * pair], inv[2 * pair + 1])
            attn_ref[r0:r0 + lq, c0 + pair * LANES:c0 + (pair + 1) * LANES] = (
                o[pair * lq:(pair + 1) * lq] * norm).astype(_BF16)

    units = [(s, w, h) for h in range(N_KV_HEADS) for s in range(ns) for w in range(n_win)]
    units_per_slot = len(units) // (2 * N_KV_HEADS)

    sub = lax.broadcasted_iota(jnp.int32, (SUBLANES, RNN_BLOCK), 0)

    def recur_gates(n):
        half_xc = 0.5 * xc_ref[:, n * RNN_BLOCK:(n + 1) * RNN_BLOCK]
        return half_xc, _dot(half_xc.astype(_BF16), w_rg_ref[n])

    def recur(n, half_xc, half_gates):
        cs_ = slice(n * RNN_BLOCK, (n + 1) * RNN_BLOCK)
        half_c = (-0.5 * LRU_C) * _softplus(-v1280_ref[ROW_LAMBDA:ROW_LAMBDA + 1, cs_])
        half_ba = 0.5 * v1280_ref[ROW_B_A:ROW_B_A + 1, cs_]
        half_bx = 0.5 * v1280_ref[ROW_B_X:ROW_B_X + 1, cs_]
        log_a = half_c * jnp.tanh(half_gates[:, :RNN_BLOCK] + half_ba) + half_c
        gated_x = jnp.tanh(half_gates[:, RNN_BLOCK:] + half_bx) * half_xc + half_xc
        a = jnp.exp(log_a)
        b = _sqrt_nonneg(_neg_expm1_twice(log_a)) * gated_x
        hs = []
        for s in range(ns):
            sl = slice(s * ls, (s + 1) * ls)
            a3 = a[sl].reshape(steps, SUBLANES, RNN_BLOCK)
            b3 = b[sl].reshape(steps, SUBLANES, RNN_BLOCK)
            hz, ap = [b3[0]], [a3[0]]
            for jb in range(1, steps):
                hz.append(a3[jb] * hz[-1] + b3[jb])
                ap.append(a3[jb] * ap[-1])
            ae, he = ap[-1], hz[-1]
            d = 1
            while d < SUBLANES:
                keep = sub >= d
                he = he + ae * jnp.where(keep, pltpu.roll(he, d, axis=0), 0.0)
                ae = ae * jnp.where(keep, pltpu.roll(ae, d, axis=0), 1.0)
                d *= 2
            hprev = hcar_ref[0:1, cs_] * keep_mix if carry else h0_ref[s, 0:1, cs_]
            after = he + ae * hprev
            before = jnp.where(sub == 0, hprev, pltpu.roll(after, 1, axis=0))
            hs.extend(hz[jb] + ap[jb] * before for jb in range(steps))
            hlast = after[SUBLANES - 1:SUBLANES, :]
            hs_ref[s, 0:1, cs_] = hlast
            if carry:
                hcar_ref[0:1, cs_] = hlast
        hfull = jnp.concatenate(hs, axis=0)
        rnn_ref[:, cs_] = (hfull * _gelu_tanh(yr_ref[:, cs_])).astype(_BF16)

    def next_scores():
        return [attend_scores(*units.pop(0)) for _ in range(min(units_per_slot, len(units)))]

    def main_loop(with_proj):
        scored = next_scores()
        gates = recur_gates(0)
        ao = None
        for n in range(n_blk):
            scored_next = next_scores()
            if with_proj:
                emit_proj(n - 1, n_blk - n)
            gates_next = recur_gates(n + 1) if n + 1 < n_blk else None
            recur(n, *gates)
            for unit in scored:
                attend_values(*unit)
            if ao is None and not (scored or scored_next or units):
                ao = attention_projection()
            scored, gates = scored_next, gates_next
        for unit in scored:
            attend_values(*unit)
        if with_proj:
            emit_proj(n_blk - 1, 1)
        assert not units
        return ao if ao is not None else attention_projection()

    n_slab = d_model // MXU_COLS

    def attention_projection():
        return [_dot(attn_ref[...], w_ao_ref[j]) for j in range(n_slab)]

    def recurrent_projection():
        rnn = _dot(from_strand_ref[...], rnn_ref[...]).astype(_BF16)
        return [_dot(rnn, w_ro_ref[j]) for j in range(n_slab)]

    def prepare_window_and_conv():
        qk_scale = LOG2E * HEAD_DIM ** -0.5
        if carry:
            kwin_ref[0, :, :, 0:WINDOW, :] = kwin_ref[0, :, :, ls:ls + WINDOW, :]
            vwin_ref[0, :, :, 0:WINDOW, :] = vwin_ref[0, :, :, ls:ls + WINDOW, :]
        for s in range(ns):
            sl = slice(s * ls, (s + 1) * ls)
            kd = _split_heads(k_ref[sl, :], qk_scale)
            vd = _split_heads(v_ref[sl, :])
            for h in range(N_KV_HEADS):
                for half in range(2):
                    kwin_ref[s, h, half, WINDOW:WINDOW + ls, :] = kd[h][half]
                    vwin_ref[s, h, half, WINDOW:WINDOW + ls, :] = vd[h][half]
            if carry:
                kst_ref[0] = k_ref[rows - WINDOW:rows, :]
                vst_ref[0] = v_ref[rows - WINDOW:rows, :]
            else:
                hk = _split_heads(ck_ref[s], qk_scale)
                hv = _split_heads(cv_ref[s])
                pad = KEY_WIN - WINDOW - ls
                for h in range(N_KV_HEADS):
                    for half in range(2):
                        kwin_ref[s, h, half, 0:WINDOW, :] = hk[h][half]
                        vwin_ref[s, h, half, 0:WINDOW, :] = hv[h][half]
                        kwin_ref[s, h, half, WINDOW + ls:KEY_WIN, :] = jnp.zeros((pad, LANES), _BF16)
                        vwin_ref[s, h, half, WINDOW + ls:KEY_WIN, :] = jnp.zeros((pad, LANES), _BF16)
                kst_ref[s, 0:WINDOW - ls, :] = ck_ref[s, ls:WINDOW, :]
                kst_ref[s, WINDOW - ls:WINDOW, :] = k_ref[sl, :]
                vst_ref[s, 0:WINDOW - ls, :] = cv_ref[s, ls:WINDOW, :]
                vst_ref[s, WINDOW - ls:WINDOW, :] = v_ref[sl, :]

            hist = c0_ref[s] if not carry else hist_ref[0] * keep_prep
            first = lax.broadcasted_iota(jnp.int32, (SUBLANES, LANES), 0) == 0
            for col in range(0, d_rnn, LANES):
                lanes = slice(col, col + LANES)
                taps = [jnp.broadcast_to(v1280_ref[r:r + 1, lanes], (SUBLANES, LANES))
                        for r in range(CONV_W + 1)]
                blocks = [xr_ref[s * ls + jb * SUBLANES:s * ls + (jb + 1) * SUBLANES, lanes]
                          for jb in range(steps)]

                def earlier(jb, d):
                    if jb >= d:
                        return blocks[jb - d]
                    row = CONV_W - 1 + jb - d
                    prev_strand = pltpu.roll(blocks[steps + jb - d], 1, axis=0)
                    return jnp.where(first, hist[row:row + 1, lanes], prev_strand)

                for jb in range(steps):
                    acc = taps[ROW_B_CONV] + taps[CONV_W - 1] * blocks[jb]
                    for d in range(1, CONV_W):
                        acc = acc + taps[CONV_W - 1 - d] * earlier(jb, d)
                    xc_ref[s * ls + jb * SUBLANES:s * ls + (jb + 1) * SUBLANES, lanes] = acc
            tail = jnp.concatenate(
                [xr_ref[s * ls + (steps - d + 1) * SUBLANES - 1:s * ls + (steps - d + 1) * SUBLANES, :]
                 for d in range(CONV_W - 1, 0, -1)], axis=0)
            cst_ref[s] = tail
            if carry:
                hist_ref[s] = tail

    def output_projection(ao, ro):
        merged = jnp.concatenate(
            [(sga_ref[:, j * MXU_COLS:(j + 1) * MXU_COLS] * ao[j]
              + sgr_ref[:, j * MXU_COLS:(j + 1) * MXU_COLS] * ro[j]).astype(_BF16)
             for j in range(n_slab)], axis=1)
        return jnp.concatenate([_dot(merged, w_out_ref[j]) for j in range(n_slab)], axis=1)

    def gate_sigmoids():
        sga_ref[...] = _sigmoid(ga_ref[...])
        sgr_ref[...] = _sigmoid(gr_ref[...])

    def residual(mo):
        yn = _rms(mo)
        for s in range(ns):
            sl = slice(s * ls, (s + 1) * ls)
            gain = v1024_ref[1:2, :] * modb_ref[s, MOD_GATE_MIX:MOD_GATE_MIX + 1, :]
            y_ref[sl, :] = xb_ref[sl, :] + gain * yn[sl]

    if skew:
        publish_state()
        norm_input()
        ao = main_loop(with_proj=True)
        ro = recurrent_projection()
        prepare_window_and_conv()
        mo = output_projection(ao, ro)
        gate_sigmoids()
        residual(mo)
        emit_proj(n_blk, 0)
    else:
        norm_input()
        emit_proj(n_blk, 0)
        prepare_window_and_conv()
        gate_sigmoids()
        ao = main_loop(with_proj=False)
        residual(output_projection(ao, recurrent_projection()))
        publish_state()
    assert not proj


def _whole(memory_space=pltpu.VMEM):
    return pl.BlockSpec(memory_space=memory_space)


def _nbytes(shape, dtype):
    return math.prod(shape) * jnp.dtype(dtype).itemsize


def _vmem_limit(resident, windows, scratch, value_tiles, rows, d_model):
    total = sum(_nbytes(a.shape, a.dtype) for a in resident)
    total += 2 * sum(_nbytes(shape, dtype) for shape, dtype in windows)
    total += sum(_nbytes(s.shape, s.dtype) for s in scratch)
    return total + value_tiles * _nbytes((rows, d_model), _F32)


def _mixer_call(x, mod, hist, sinks, v1024, v1280, w_in, w_rg, w_ao, w_ro, w_out, *, ns, ls, skew):
    n_seq, seq_len, d_model = x.shape
    d_rnn = v1280.shape[1]
    carry = hist is None
    rows = ns * ls
    if carry:
        assert ns == 1 and seq_len % ls == 0 and ls % (2 * CHUNK) == 0
        tiles_per_seq = seq_len // ls
    else:
        assert seq_len == ls == CHUNK and n_seq % ns == 0
        tiles_per_seq = 1
    n_tiles = n_seq * seq_len // rows
    lq = min(ls, 2 * CHUNK)
    xf = x.reshape(n_seq * seq_len, d_model)

    if skew:
        tile_a = lambda i: jnp.minimum(i, n_tiles - 1)
        tile_b = lambda i: jnp.maximum(i - 1, 0)
    else:
        tile_a = tile_b = lambda i: i
    seq_a = lambda i: (tile_a(i) // tiles_per_seq, 0, 0)
    seq_b = lambda i: (tile_b(i) // tiles_per_seq, 0, 0)

    in_specs = [pl.BlockSpec((rows, d_model), lambda i: (tile_a(i), 0)),
                pl.BlockSpec((ns, 6, d_model), seq_a)]
    args = [xf, mod]
    if skew:
        in_specs = [in_specs[0], pl.BlockSpec((rows, d_model), lambda i: (tile_b(i), 0)),
                    in_specs[1], pl.BlockSpec((ns, 6, d_model), seq_b)]
        args = [xf, xf, mod, mod]
    if not carry:
        ck, cv, c0, h0 = hist
        in_specs += [pl.BlockSpec((ns, WINDOW, KV_W), seq_a),
                     pl.BlockSpec((ns, WINDOW, KV_W), seq_a),
                     pl.BlockSpec((ns, CONV_W - 1, d_rnn), seq_a),
                     pl.BlockSpec((ns, 1, d_rnn), seq_b)]
        args += [ck, cv, c0, h0]
    in_specs += [_whole(pltpu.SMEM)] + [_whole()] * 7
    args += [sinks, v1024, v1280, w_in, w_rg, w_ao, w_ro, w_out]

    out_shape = (jax.ShapeDtypeStruct((n_seq * seq_len, d_model), _F32),
                 jax.ShapeDtypeStruct((n_seq, WINDOW, KV_W), _F32),
                 jax.ShapeDtypeStruct((n_seq, WINDOW, KV_W), _F32),
                 jax.ShapeDtypeStruct((n_seq, CONV_W - 1, d_rnn), _F32),
                 jax.ShapeDtypeStruct((n_seq, 1, d_rnn), _F32))
    out_specs = (pl.BlockSpec((rows, d_model), lambda i: (tile_b(i), 0)),
                 pl.BlockSpec((ns, WINDOW, KV_W), seq_b),
                 pl.BlockSpec((ns, WINDOW, KV_W), seq_b),
                 pl.BlockSpec((ns, CONV_W - 1, d_rnn), seq_b),
                 pl.BlockSpec((ns, 1, d_rnn), seq_b))
    win_rows = WINDOW + max(ls, 2 * CHUNK)
    scratch = [pltpu.VMEM((rows, d_model), _BF16),
               pltpu.VMEM((rows, d_model), _BF16),
               pltpu.VMEM((rows, N_HEADS * HEAD_DIM), _BF16),
               pltpu.VMEM((rows, KV_W), _F32),
               pltpu.VMEM((rows, KV_W), _F32),
               pltpu.VMEM((rows, d_rnn), _F32),
               pltpu.VMEM((rows, d_rnn), _F32),
               pltpu.VMEM((rows, d_model), _F32),
               pltpu.VMEM((rows, d_model), _F32),
               pltpu.VMEM((rows, d_rnn), _F32),
               pltpu.VMEM((rows, d_model), _F32),
               pltpu.VMEM((rows, d_model), _F32),
               pltpu.VMEM((ns, N_KV_HEADS, 2, win_rows, LANES), _BF16),
               pltpu.VMEM((ns, N_KV_HEADS, 2, win_rows, LANES), _BF16),
               pltpu.VMEM((ns, WINDOW, KV_W), _F32),
               pltpu.VMEM((ns, WINDOW, KV_W), _F32),
               pltpu.VMEM((ns, CONV_W - 1, d_rnn), _F32),
               pltpu.VMEM((N_KV_HEADS, 2, 2 * lq, KEY_WIN), _F32),
               pltpu.VMEM((ns, CONV_W - 1, d_rnn), _F32),
               pltpu.VMEM((1, d_rnn), _F32),
               pltpu.VMEM((rows, N_HEADS * HEAD_DIM), _BF16),
               pltpu.VMEM((rows, d_rnn), _BF16),
               pltpu.VMEM((rows, rows), _BF16),
               pltpu.VMEM((rows, rows), _BF16)]
    kern = functools.partial(_mixer_kernel, skew, carry, ns, ls, tiles_per_seq, d_model, d_rnn)
    windows = [(spec.block_shape, _F32) for spec in (*in_specs, *out_specs)
               if spec.block_shape is not None]
    vmem_limit = _vmem_limit((v1024, v1280, w_in, w_rg, w_ao, w_ro, w_out), windows, scratch,
                             MIXER_VALUE_TILES, rows, d_model)
    return pl.pallas_call(
        kern,
        grid=(n_tiles + (1 if skew else 0),),
        in_specs=in_specs,
        out_specs=out_specs,
        out_shape=out_shape,
        scratch_shapes=scratch,
        compiler_params=pltpu.CompilerParams(dimension_semantics=("arbitrary",),
                                             vmem_limit_bytes=vmem_limit),
        name="mixer_stream" if carry else "mixer_step",
    )(*args)


def _ffn_kernel(n_prompt, sample_seqs, sample_len, xp_ref, xs_ref, modp_ref, mods_ref, v1024_ref,
                wg_ref, wu_ref, wd_ref, yp_ref, ys_ref, u_ref):
    rows = xp_ref.shape[0]
    is_prompt = pl.program_id(0) < n_prompt

    @pl.when(is_prompt)
    def _():
        _ffn_tile(1, rows, xp_ref, modp_ref, v1024_ref, wg_ref, wu_ref, wd_ref, yp_ref, u_ref)

    @pl.when(jnp.logical_not(is_prompt))
    def _():
        _ffn_tile(sample_seqs, sample_len, xs_ref, mods_ref, v1024_ref, wg_ref, wu_ref, wd_ref,
                  ys_ref, u_ref)


def _ffn_tile(ns, ls, x_ref, mod_ref, v1024_ref, wg_ref, wu_ref, wd_ref, y_ref, u_ref):
    x = x_ref[...]
    xn = _rms(x)
    for s in range(ns):
        sl = slice(s * ls, (s + 1) * ls)
        gain = v1024_ref[0:1, :] * (1.0 + mod_ref[s, MOD_SCALE_FFN:MOD_SCALE_FFN + 1, :])
        u_ref[sl, :] = (xn[sl] * gain + mod_ref[s, MOD_SHIFT_FFN:MOD_SHIFT_FFN + 1, :]).astype(_BF16)
    u = u_ref[...]
    g = _dot(u, wg_ref[...].astype(_BF16))
    up = _dot(u, wu_ref[...].astype(_BF16))
    hmid = (_silu(g) * up).astype(_BF16)
    yn = _rms(_dot(hmid, wd_ref[...].astype(_BF16)))
    for s in range(ns):
        sl = slice(s * ls, (s + 1) * ls)
        gain = v1024_ref[1:2, :] * mod_ref[s, MOD_GATE_FFN:MOD_GATE_FFN + 1, :]
        y_ref[sl, :] = x[sl] + gain * yn[sl]


def _ffn_call(xp, xs, mod_p, mod_s, v1024, wg, wu, wd, *, rows, prompt_len, sample_len):
    d_model = xp.shape[1]
    assert prompt_len % rows == 0 and rows % sample_len == 0
    assert xp.shape[0] % rows == 0 and xs.shape[0] % rows == 0
    n_prompt, n_sample = xp.shape[0] // rows, xs.shape[0] // rows
    sample_seqs = rows // sample_len
    tiles_per_seq = prompt_len // rows
    tile_p = lambda i: jnp.minimum(i, n_prompt - 1)
    tile_s = lambda i: jnp.maximum(i - n_prompt, 0)
    row_block = pl.BlockSpec((rows, d_model), lambda i: (tile_p(i), 0))
    row_block_s = pl.BlockSpec((rows, d_model), lambda i: (tile_s(i), 0))
    scratch = [pltpu.VMEM((rows, d_model), _BF16)]
    windows = [((rows, d_model), _F32)] * 4 + [((1, 6, d_model), _F32), ((sample_seqs, 6, d_model), _F32)]
    value_tiles = FFN_VALUE_TILES + FFN_WIDE_VALUES * pl.cdiv(wg.shape[1], d_model)
    return pl.pallas_call(
        functools.partial(_ffn_kernel, n_prompt, sample_seqs, sample_len),
        grid=(n_prompt + n_sample,),
        in_specs=[row_block, row_block_s,
                  pl.BlockSpec((1, 6, d_model), lambda i: (tile_p(i) // tiles_per_seq, 0, 0)),
                  pl.BlockSpec((sample_seqs, 6, d_model), lambda i: (tile_s(i), 0, 0)),
                  _whole(), _whole(), _whole(), _whole()],
        out_specs=(row_block, row_block_s),
        out_shape=(jax.ShapeDtypeStruct(xp.shape, _F32), jax.ShapeDtypeStruct(xs.shape, _F32)),
        scratch_shapes=scratch,
        compiler_params=pltpu.CompilerParams(
            dimension_semantics=("arbitrary",),
            vmem_limit_bytes=_vmem_limit((v1024, wg, wu, wd), windows, scratch, value_tiles,
                                         rows, d_model)),
        name="ffn",
    )(xp, xs, mod_p, mod_s, v1024, wg, wu, wd)


MIX_TILE = 256
MIX_SEQS = 4
SKEW_PROMPT = True
SKEW_SAMPLE = False
FFN_TILE = 512


def kernel(x_prompt, x_sample, c_prompt, c_sample, cache_k, cache_v, state_conv, state_h, w_ada, b_ada, g_pre_mix, g_post_mix, w_in, attn_sinks, w_conv, b_conv, w_rg_a, b_rg_a, w_rg_x, b_rg_x, rg_lambda, w_attn_o, w_rnn_o, w_out, g_pre_ffn, g_post_ffn, w_ffn_gate, w_ffn_up, w_ffn_down):
    depth = w_in.shape[0]
    assert depth == 1
    bp, sp, d_model = x_prompt.shape
    bs, ss, _ = x_sample.shape
    d_rnn = w_conv.shape[-1]
    l = 0

    n_c = bp + bs
    pad = (-n_c) % 16
    c_all = jnp.concatenate([c_prompt, c_sample, jnp.zeros((pad, d_model), _F32)], axis=0)
    mod = _ada_call(c_all, w_ada[l], b_ada[l])[:n_c].reshape(n_c, 6, d_model)
    mod_p, mod_s = mod[:bp], mod[bp:]

    bf = lambda w: w.astype(_BF16)
    v1024_mix = jnp.stack([g_pre_mix[l], g_post_mix[l]])
    v1024_ffn = jnp.stack([g_pre_ffn[l], g_post_ffn[l]])
    v1280 = jnp.concatenate([w_conv[l], b_conv[l][None], b_rg_a[l][None], b_rg_x[l][None],
                             rg_lambda[l][None]], axis=0)
    w_rg = bf(jnp.concatenate([w_rg_a[l], w_rg_x[l]], axis=-1))
    mix_w = (attn_sinks[l], v1024_mix, v1280, _slabs(w_in[l]), w_rg, _slabs(w_attn_o[l]),
             _slabs(w_rnn_o[l]), _slabs(w_out[l]))
    ffn_w = (v1024_ffn, w_ffn_gate[l], w_ffn_up[l], w_ffn_down[l])

    xp1, kp, vp, cp, hp = _mixer_call(x_prompt, mod_p, None, *mix_w, ns=1, ls=MIX_TILE,
                                      skew=SKEW_PROMPT)
    hist = (cache_k[l].reshape(bs, WINDOW, KV_W), cache_v[l].reshape(bs, WINDOW, KV_W),
            state_conv[l], state_h[l].reshape(bs, 1, d_rnn))
    xs1, ks, vs, cs, hs = _mixer_call(x_sample, mod_s, hist, *mix_w, ns=MIX_SEQS, ls=ss,
                                      skew=SKEW_SAMPLE)

    yp, ys = _ffn_call(xp1, xs1, mod_p, mod_s, *ffn_w, rows=FFN_TILE, prompt_len=sp, sample_len=ss)

    kv_shape = lambda n: (1, n, WINDOW, N_KV_HEADS, HEAD_DIM)
    return (yp.reshape(bp, sp, d_model), ys.reshape(bs, ss, d_model),
            kp.reshape(kv_shape(bp)), vp.reshape(kv_shape(bp)), cp[None], hp.reshape(1, bp, d_rnn),
            ks.reshape(kv_shape(bs)), vs.reshape(kv_shape(bs)), cs[None], hs.reshape(1, bs, d_rnn))
```

```python
import functools
import math

import jax
import jax.numpy as jnp
from jax import lax
from jax.experimental import pallas as pl
from jax.experimental.pallas import tpu as pltpu

CHUNK = 64
N_HEADS = 16
N_KV_HEADS = 4
HEAD_DIM = 64
GROUP = N_HEADS // N_KV_HEADS
WINDOW = 128
N_BACK = WINDOW // CHUNK
KV_W = N_KV_HEADS * HEAD_DIM
RNN_BLOCK = 128
CONV_W = 4
LRU_C = 8.0
EPS = 1e-6
NEG_INF = -1e30

LANES = 128
SUBLANES = 8
MXU_COLS = 256
SLABS_PER_CAST_STEP = 4
KEY_WIN = WINDOW + 2 * CHUNK
MIXER_VALUE_TILES = 14
FFN_VALUE_TILES = 2
FFN_WIDE_VALUES = 1
LOG2E = math.log2(math.e)
ROW_B_CONV, ROW_B_A, ROW_B_X, ROW_LAMBDA = CONV_W, CONV_W + 1, CONV_W + 2, CONV_W + 3
MOD_SHIFT_MIX, MOD_SCALE_MIX, MOD_GATE_MIX, MOD_SHIFT_FFN, MOD_SCALE_FFN, MOD_GATE_FFN = range(6)

_F32 = jnp.float32
_BF16 = jnp.bfloat16


def _dot(a, b):
    return lax.dot_general(a, b, (((1,), (0,)), ((), ())), preferred_element_type=_F32)


def _dot_nt(a, b):
    return lax.dot_general(a, b, (((1,), (1,)), ((), ())), preferred_element_type=_F32)


def _rms(x):
    return x * lax.rsqrt(jnp.mean(x * x, axis=-1, keepdims=True) + EPS)


def _softplus(x):
    return jnp.maximum(x, 0.0) + jnp.log1p(jnp.exp(-jnp.abs(x)))


def _sigmoid(x):
    return 0.5 * jnp.tanh(0.5 * x) + 0.5


def _silu(x):
    h = 0.5 * x
    return h * jnp.tanh(h) + h


def _sqrt_nonneg(x):
    return jnp.exp2((0.5 * LOG2E) * jnp.log(x))


def _neg_expm1_twice(x):
    t = jnp.tanh(x)
    return (-2.0 * t) / (1.0 - t)


def _slab_cast_kernel(n_slabs, w_ref, o_ref):
    for j in range(n_slabs):
        o_ref[j] = w_ref[:, j * MXU_COLS:(j + 1) * MXU_COLS].astype(_BF16)


def _slabs(w):
    k, n = w.shape
    total = n // MXU_COLS
    per_step = math.gcd(total, SLABS_PER_CAST_STEP)
    return pl.pallas_call(
        functools.partial(_slab_cast_kernel, per_step),
        grid=(total // per_step,),
        in_specs=[pl.BlockSpec((k, per_step * MXU_COLS), lambda j: (0, j))],
        out_specs=pl.BlockSpec((per_step, k, MXU_COLS), lambda j: (j, 0, 0)),
        out_shape=jax.ShapeDtypeStruct((total, k, MXU_COLS), _BF16),
        compiler_params=pltpu.CompilerParams(dimension_semantics=("arbitrary",)),
        name="slab_cast",
    )(w)


def _gelu_tanh(x):
    c = math.sqrt(2.0 / math.pi)
    hx = 0.5 * x
    return hx + hx * jnp.tanh(x * (c + (c * 0.044715) * (x * x)))


def _split_heads(kv, scale=None):
    n = kv.shape[0]
    lo = lax.broadcasted_iota(jnp.int32, (n, LANES), 1) < HEAD_DIM
    outs = []
    for blk in range(KV_W // LANES):
        b = kv[:, blk * LANES:(blk + 1) * LANES]
        if scale is not None:
            b = b * scale
        r = pltpu.roll(b, HEAD_DIM, axis=1)
        outs.append((jnp.where(lo, b, 0.0).astype(_BF16), jnp.where(lo, 0.0, r).astype(_BF16)))
        outs.append((jnp.where(lo, r, 0.0).astype(_BF16), jnp.where(lo, 0.0, b).astype(_BF16)))
    return outs


def _ada_kernel(c_ref, w_ref, b_ref, o_ref):
    c = c_ref[...]
    a = _silu(c).astype(_BF16)
    o_ref[...] = _dot(a, w_ref[...].astype(_BF16)) + b_ref[...]


def _ada_call(c, w_ada, b_ada):
    rows, d = c.shape
    n = w_ada.shape[1]
    bn = d
    return pl.pallas_call(
        _ada_kernel,
        grid=(n // bn,),
        in_specs=[
            pl.BlockSpec((rows, d), lambda j: (0, 0)),
            pl.BlockSpec((d, bn), lambda j: (0, j)),
            pl.BlockSpec((1, bn), lambda j: (0, j)),
        ],
        out_specs=pl.BlockSpec((rows, bn), lambda j: (0, j)),
        out_shape=jax.ShapeDtypeStruct((rows, n), _F32),
        compiler_params=pltpu.CompilerParams(dimension_semantics=("arbitrary",)),
        name="ada",
    )(c, w_ada, b_ada.reshape(1, n))


def _mixer_kernel(skew, carry, ns, ls, tiles_per_seq, d_model, d_rnn, *refs):
    if skew:
        xa_ref, xb_ref, moda_ref, modb_ref = refs[:4]
        refs = refs[4:]
    else:
        xa_ref, moda_ref = xb_ref, modb_ref = refs[:2]
        refs = refs[2:]
    n_in = 0 if carry else 4
    if not carry:
        ck_ref, cv_ref, c0_ref, h0_ref = refs[:4]
    (sinks_ref, v1024_ref, v1280_ref, w_in_ref, w_rg_ref, w_ao_ref, w_ro_ref, w_out_ref,
     y_ref, ks_ref, vs_ref, cs_ref, hs_ref,
     u_ref, us_ref, q_ref, k_ref, v_ref, xr_ref, yr_ref, ga_ref, gr_ref, xc_ref, sga_ref, sgr_ref,
     kwin_ref, vwin_ref, kst_ref, vst_ref, cst_ref, bias_ref, hist_ref, hcar_ref,
     attn_ref, rnn_ref, to_strand_ref, from_strand_ref) = refs[n_in:]

    i = pl.program_id(0)
    rows = ns * ls
    lq = min(ls, 2 * CHUNK)
    n_win = ls // lq
    q_w = N_HEADS * HEAD_DIM
    n_blk = d_rnn // RNN_BLOCK
    steps = ls // SUBLANES
    if carry:
        mixed = i + tiles_per_seq - 1 if skew else i
        keep_mix = jnp.where(lax.rem(mixed, tiles_per_seq) == 0, 0.0, 1.0)
        keep_prep = jnp.where(lax.rem(i, tiles_per_seq) == 0, 0.0, 1.0)

    @pl.when(i == 0)
    def _():
        for ref in (kwin_ref, vwin_ref, kst_ref, vst_ref, cst_ref, hist_ref, hcar_ref):
            ref[...] = jnp.zeros(ref.shape, ref.dtype)
        r = lax.broadcasted_iota(jnp.int32, (rows, rows), 0)
        c = lax.broadcasted_iota(jnp.int32, (rows, rows), 1)
        seg, rr = (r // ls) * ls, r % ls
        to_strand_ref[...] = jnp.where(
            c == seg + (rr % SUBLANES) * steps + rr // SUBLANES, 1.0, 0.0).astype(_BF16)
        from_strand_ref[...] = jnp.where(
            c == seg + (rr % steps) * SUBLANES + rr // steps, 1.0, 0.0).astype(_BF16)
        t = lax.broadcasted_iota(jnp.int32, (lq, KEY_WIN), 0)
        j = lax.broadcasted_iota(jnp.int32, (lq, KEY_WIN), 1)
        dist = jnp.abs(t + WINDOW - j).astype(_F32)
        qc = t // CHUNK
        kc = j // CHUNK
        base = jnp.where(kc >= qc, jnp.where(kc <= qc + N_BACK, 0.0, NEG_INF), NEG_INF)
        for h in range(N_KV_HEADS):
            for g in range(GROUP):
                slope = 2.0 ** (-8.0 * (h * GROUP + g + 1) / N_HEADS)
                bias_ref[h, g % 2, (g // 2) * lq:(g // 2 + 1) * lq, :] = LOG2E * (base - slope * dist)

    def publish_state():
        ks_ref[...] = kst_ref[...]
        vs_ref[...] = vst_ref[...]
        cs_ref[...] = cst_ref[...]

    def norm_input():
        xn = _rms(xa_ref[...])
        for s in range(ns):
            sl = slice(s * ls, (s + 1) * ls)
            gain = v1024_ref[0:1, :] * (1.0 + moda_ref[s, MOD_SCALE_MIX:MOD_SCALE_MIX + 1, :])
            u_ref[sl, :] = (xn[sl] * gain + moda_ref[s, MOD_SHIFT_MIX:MOD_SHIFT_MIX + 1, :]).astype(_BF16)
        us_ref[...] = _dot(to_strand_ref[...], u_ref[...]).astype(_BF16)

    proj = []
    off = 0
    for src, ref, width, free_after in (
            (u_ref, q_ref, q_w, lambda c: 2 * (c // MXU_COLS) + 1),
            (u_ref, k_ref, KV_W, lambda c: -1), (u_ref, v_ref, KV_W, lambda c: -1),
            (us_ref, xr_ref, d_rnn, lambda c: -1),
            (us_ref, yr_ref, d_rnn, lambda c: (c + MXU_COLS - 1) // RNN_BLOCK),
            (u_ref, ga_ref, d_model, lambda c: -1), (u_ref, gr_ref, d_model, lambda c: -1)):
        for c in range(0, width, MXU_COLS):
            proj.append((free_after(c), src, ref, c, off + c))
        off += width
    proj.sort(key=lambda p: p[0])

    def emit_proj(slot, slots_left):
        ready = [p for p in proj if p[0] <= slot]
        quota = -(-len(proj) // slots_left) if slots_left else len(proj)
        for p in ready[:quota]:
            proj.remove(p)
            _, src, ref, c, wc = p
            ref[:, c:c + MXU_COLS] = _dot(src[...], w_in_ref[wc // MXU_COLS]).astype(ref.dtype)

    lo_sel = lax.broadcasted_iota(jnp.int32, (lq, LANES), 1) < HEAD_DIM
    if carry:
        kcol = lax.broadcasted_iota(jnp.int32, (1, KEY_WIN), 1)
        hist_bias = jnp.where(kcol < WINDOW, NEG_INF, 0.0).astype(_F32) * (1.0 - keep_mix)

    def attend_scores(s, w, h):
        r0 = s * ls + w * lq
        kw = slice(w * lq, w * lq + KEY_WIN)
        c0 = h * GROUP * HEAD_DIM
        qab = jnp.concatenate([q_ref[r0:r0 + lq, c0:c0 + LANES],
                               q_ref[r0:r0 + lq, c0 + LANES:c0 + 2 * LANES]], axis=0)
        ps, inv = [[], []], [None] * GROUP
        for half in range(2):
            sc = _dot_nt(qab, kwin_ref[s, h, half, kw, :]) + bias_ref[h, half]
            if carry and w == 0:
                sc = sc + hist_bias
            for pair in range(2):
                g = 2 * pair + half
                sg = sc[pair * lq:(pair + 1) * lq]
                sink = LOG2E * sinks_ref[h * GROUP + g]
                mg = jnp.maximum(jnp.max(sg, axis=-1, keepdims=True), sink)
                pg = jnp.exp2(sg - mg)
                inv[g] = 1.0 / (jnp.sum(pg, axis=-1, keepdims=True) + jnp.exp2(sink - mg))
                ps[half].append(pg.astype(_BF16))
        return s, h, r0, kw, c0, ps, inv

    def attend_values(s, h, r0, kw, c0, ps, inv):
        o = (_dot(jnp.concatenate(ps[0], axis=0), vwin_ref[s, h, 0, kw, :])
             + _dot(jnp.concatenate(ps[1], axis=0), vwin_ref[s, h, 1, kw, :]))
        for pair in range(2):
            norm = jnp.where(lo_sel, inv[2 * pair], inv[2 * pair + 1])
            attn_ref[r0:r0 + lq, c0 + pair * LANES:c0 + (pair + 1) * LANES] = (
                o[pair * lq:(pair + 1) * lq] * norm).astype(_BF16)

    units = [(s, w, h) for h in range(N_KV_HEADS) for s in range(ns) for w in range(n_win)]
    units_per_slot = len(units) // (2 * N_KV_HEADS)

    sub = lax.broadcasted_iota(jnp.int32, (SUBLANES, RNN_BLOCK), 0)

    def recur_gates(n):
        half_xc = 0.5 * xc_ref[:, n * RNN_BLOCK:(n + 1) * RNN_BLOCK]
        return half_xc, _dot(half_xc.astype(_BF16), w_rg_ref[n])

    def recur(n, half_xc, half_gates):
        cs_ = slice(n * RNN_BLOCK, (n + 1) * RNN_BLOCK)
        half_c = (-0.5 * LRU_C) * _softplus(-v1280_ref[ROW_LAMBDA:ROW_LAMBDA + 1, cs_])
        half_ba = 0.5 * v1280_ref[ROW_B_A:ROW_B_A + 1, cs_]
        half_bx = 0.5 * v1280_ref[ROW_B_X:ROW_B_X + 1, cs_]
        log_a = half_c * jnp.tanh(half_gates[:, :RNN_BLOCK] + half_ba) + half_c
        gated_x = jnp.tanh(half_gates[:, RNN_BLOCK:] + half_bx) * half_xc + half_xc
        a = jnp.exp(log_a)
        b = _sqrt_nonneg(_neg_expm1_twice(log_a)) * gated_x
        hs = []
        for s in range(ns):
            sl = slice(s * ls, (s + 1) * ls)
            a3 = a[sl].reshape(steps, SUBLANES, RNN_BLOCK)
            b3 = b[sl].reshape(steps, SUBLANES, RNN_BLOCK)
            hz, ap = [b3[0]], [a3[0]]
            for jb in range(1, steps):
                hz.append(a3[jb] * hz[-1] + b3[jb])
                ap.append(a3[jb] * ap[-1])
            ae, he = ap[-1], hz[-1]
            d = 1
            while d < SUBLANES:
                keep = sub >= d
                he = he + ae * jnp.where(keep, pltpu.roll(he, d, axis=0), 0.0)
                ae = ae * jnp.where(keep, pltpu.roll(ae, d, axis=0), 1.0)
                d *= 2
            hprev = hcar_ref[0:1, cs_] * keep_mix if carry else h0_ref[s, 0:1, cs_]
            after = he + ae * hprev
            before = jnp.where(sub == 0, hprev, pltpu.roll(after, 1, axis=0))
            hs.extend(hz[jb] + ap[jb] * before for jb in range(steps))
            hlast = after[SUBLANES - 1:SUBLANES, :]
            hs_ref[s, 0:1, cs_] = hlast
            if carry:
                hcar_ref[0:1, cs_] = hlast
        hfull = jnp.concatenate(hs, axis=0)
        rnn_ref[:, cs_] = (hfull * _gelu_tanh(yr_ref[:, cs_])).astype(_BF16)

    def next_scores():
        return [attend_scores(*units.pop(0)) for _ in range(min(units_per_slot, len(units)))]

    def main_loop(with_proj):
        scored = next_scores()
        gates = recur_gates(0)
        ao = None
        for n in range(n_blk):
            scored_next = next_scores()
            if with_proj:
                emit_proj(n - 1, n_blk - n)
            gates_next = recur_gates(n + 1) if n + 1 < n_blk else None
            recur(n, *gates)
            for unit in scored:
                attend_values(*unit)
            if ao is None and not (scored or scored_next or units):
                ao = attention_projection()
            scored, gates = scored_next, gates_next
        for unit in scored:
            attend_values(*unit)
        if with_proj:
            emit_proj(n_blk - 1, 1)
        assert not units
        return ao if ao is not None else attention_projection()

    n_slab = d_model // MXU_COLS

    def attention_projection():
        return [_dot(attn_ref[...], w_ao_ref[j]) for j in range(n_slab)]

    def recurrent_projection():
        rnn = _dot(from_strand_ref[...], rnn_ref[...]).astype(_BF16)
        return [_dot(rnn, w_ro_ref[j]) for j in range(n_slab)]

    def prepare_window_and_conv():
        qk_scale = LOG2E * HEAD_DIM ** -0.5
        if carry:
            kwin_ref[0, :, :, 0:WINDOW, :] = kwin_ref[0, :, :, ls:ls + WINDOW, :]
            vwin_ref[0, :, :, 0:WINDOW, :] = vwin_ref[0, :, :, ls:ls + WINDOW, :]
        for s in range(ns):
            sl = slice(s * ls, (s + 1) * ls)
            kd = _split_heads(k_ref[sl, :], qk_scale)
            vd = _split_heads(v_ref[sl, :])
            for h in range(N_KV_HEADS):
                for half in range(2):
                    kwin_ref[s, h, half, WINDOW:WINDOW + ls, :] = kd[h][half]
                    vwin_ref[s, h, half, WINDOW:WINDOW + ls, :] = vd[h][half]
            if carry:
                kst_ref[0] = k_ref[rows - WINDOW:rows, :]
                vst_ref[0] = v_ref[rows - WINDOW:rows, :]
            else:
                hk = _split_heads(ck_ref[s], qk_scale)
                hv = _split_heads(cv_ref[s])
                pad = KEY_WIN - WINDOW - ls
                for h in range(N_KV_HEADS):
                    for half in range(2):
                        kwin_ref[s, h, half, 0:WINDOW, :] = hk[h][half]
                        vwin_ref[s, h, half, 0:WINDOW, :] = hv[h][half]
                        kwin_ref[s, h, half, WINDOW + ls:KEY_WIN, :] = jnp.zeros((pad, LANES), _BF16)
                        vwin_ref[s, h, half, WINDOW + ls:KEY_WIN, :] = jnp.zeros((pad, LANES), _BF16)
                kst_ref[s, 0:WINDOW - ls, :] = ck_ref[s, ls:WINDOW, :]
                kst_ref[s, WINDOW - ls:WINDOW, :] = k_ref[sl, :]
                vst_ref[s, 0:WINDOW - ls, :] = cv_ref[s, ls:WINDOW, :]
                vst_ref[s, WINDOW - ls:WINDOW, :] = v_ref[sl, :]

            hist = c0_ref[s] if not carry else hist_ref[0] * keep_prep
            first = lax.broadcasted_iota(jnp.int32, (SUBLANES, LANES), 0) == 0
            for col in range(0, d_rnn, LANES):
                lanes = slice(col, col + LANES)
                taps = [jnp.broadcast_to(v1280_ref[r:r + 1, lanes], (SUBLANES, LANES))
                        for r in range(CONV_W + 1)]
                blocks = [xr_ref[s * ls + jb * SUBLANES:s * ls + (jb + 1) * SUBLANES, lanes]
                          for jb in range(steps)]

                def earlier(jb, d):
                    if jb >= d:
                        return blocks[jb - d]
                    row = CONV_W - 1 + jb - d
                    prev_strand = pltpu.roll(blocks[steps + jb - d], 1, axis=0)
                    return jnp.where(first, hist[row:row + 1, lanes], prev_strand)

                for jb in range(steps):
                    acc = taps[ROW_B_CONV] + taps[CONV_W - 1] * blocks[jb]
                    for d in range(1, CONV_W):
                        acc = acc + taps[CONV_W - 1 - d] * earlier(jb, d)
                    xc_ref[s * ls + jb * SUBLANES:s * ls + (jb + 1) * SUBLANES, lanes] = acc
            tail = jnp.concatenate(
                [xr_ref[s * ls + (steps - d + 1) * SUBLANES - 1:s * ls + (steps - d + 1) * SUBLANES, :]
                 for d in range(CONV_W - 1, 0, -1)], axis=0)
            cst_ref[s] = tail
            if carry:
                hist_ref[s] = tail

    def output_projection(ao, ro):
        merged = jnp.concatenate(
            [(sga_ref[:, j * MXU_COLS:(j + 1) * MXU_COLS] * ao[j]
              + sgr_ref[:, j * MXU_COLS:(j + 1) * MXU_COLS] * ro[j]).astype(_BF16)
             for j in range(n_slab)], axis=1)
        return jnp.concatenate([_dot(merged, w_out_ref[j]) for j in range(n_slab)], axis=1)

    def gate_sigmoids():
        sga_ref[...] = _sigmoid(ga_ref[...])
        sgr_ref[...] = _sigmoid(gr_ref[...])

    def residual(mo):
        yn = _rms(mo)
        for s in range(ns):
            sl = slice(s * ls, (s + 1) * ls)
            gain = v1024_ref[1:2, :] * modb_ref[s, MOD_GATE_MIX:MOD_GATE_MIX + 1, :]
            y_ref[sl, :] = xb_ref[sl, :] + gain * yn[sl]

    proj_all, units_all = list(proj), list(units)

    def step_variant(when):
        def wrap(body):
            @pl.when(when)
            def _():
                proj[:], units[:] = proj_all, units_all
                body()
        return wrap

    def prepare_tile():
        norm_input()
        emit_proj(n_blk, 0)
        prepare_window_and_conv()
        gate_sigmoids()

    def mix_tile():
        ao = main_loop(with_proj=False)
        residual(output_projection(ao, recurrent_projection()))

    if skew:
        last = pl.num_programs(0) - 1

        step_variant(i == 0)(prepare_tile)

        @step_variant(jnp.logical_and(i > 0, i < last))
        def _():
            publish_state()
            norm_input()
            ao = main_loop(with_proj=True)
            ro = recurrent_projection()
            prepare_window_and_conv()
            mo = output_projection(ao, ro)
            gate_sigmoids()
            residual(mo)
            emit_proj(n_blk, 0)
            assert not proj

        @step_variant(i == last)
        def _():
            publish_state()
            mix_tile()
    else:
        prepare_tile()
        mix_tile()
        publish_state()
        assert not proj and not units


def _whole(memory_space=pltpu.VMEM):
    return pl.BlockSpec(memory_space=memory_space)


def _nbytes(shape, dtype):
    return math.prod(shape) * jnp.dtype(dtype).itemsize


def _vmem_limit(resident, windows, scratch, value_tiles, rows, d_model):
    total = sum(_nbytes(a.shape, a.dtype) for a in resident)
    total += 2 * sum(_nbytes(shape, dtype) for shape, dtype in windows)
    total += sum(_nbytes(s.shape, s.dtype) for s in scratch)
    return total + value_tiles * _nbytes((rows, d_model), _F32)


def _mixer_call(x, mod, hist, sinks, v1024, v1280, w_in, w_rg, w_ao, w_ro, w_out, *, ns, ls, skew):
    n_seq, seq_len, d_model = x.shape
    d_rnn = v1280.shape[1]
    carry = hist is None
    rows = ns * ls
    if carry:
        assert ns == 1 and seq_len % ls == 0 and ls % (2 * CHUNK) == 0
        tiles_per_seq = seq_len // ls
    else:
        assert seq_len == ls == CHUNK and n_seq % ns == 0
        tiles_per_seq = 1
    n_tiles = n_seq * seq_len // rows
    lq = min(ls, 2 * CHUNK)
    xf = x.reshape(n_seq * seq_len, d_model)

    if skew:
        tile_a = lambda i: jnp.minimum(i, n_tiles - 1)
        tile_b = lambda i: jnp.maximum(i - 1, 0)
    else:
        tile_a = tile_b = lambda i: i
    seq_a = lambda i: (tile_a(i) // tiles_per_seq, 0, 0)
    seq_b = lambda i: (tile_b(i) // tiles_per_seq, 0, 0)

    in_specs = [pl.BlockSpec((rows, d_model), lambda i: (tile_a(i), 0)),
                pl.BlockSpec((ns, 6, d_model), seq_a)]
    args = [xf, mod]
    if skew:
        in_specs = [in_specs[0], pl.BlockSpec((rows, d_model), lambda i: (tile_b(i), 0)),
                    in_specs[1], pl.BlockSpec((ns, 6, d_model), seq_b)]
        args = [xf, xf, mod, mod]
    if not carry:
        ck, cv, c0, h0 = hist
        in_specs += [pl.BlockSpec((ns, WINDOW, KV_W), seq_a),
                     pl.BlockSpec((ns, WINDOW, KV_W), seq_a),
                     pl.BlockSpec((ns, CONV_W - 1, d_rnn), seq_a),
                     pl.BlockSpec((ns, 1, d_rnn), seq_b)]
        args += [ck, cv, c0, h0]
    in_specs += [_whole(pltpu.SMEM)] + [_whole()] * 7
    args += [sinks, v1024, v1280, w_in, w_rg, w_ao, w_ro, w_out]

    out_shape = (jax.ShapeDtypeStruct((n_seq * seq_len, d_model), _F32),
                 jax.ShapeDtypeStruct((n_seq, WINDOW, KV_W), _F32),
                 jax.ShapeDtypeStruct((n_seq, WINDOW, KV_W), _F32),
                 jax.ShapeDtypeStruct((n_seq, CONV_W - 1, d_rnn), _F32),
                 jax.ShapeDtypeStruct((n_seq, 1, d_rnn), _F32))
    out_specs = (pl.BlockSpec((rows, d_model), lambda i: (tile_b(i), 0)),
                 pl.BlockSpec((ns, WINDOW, KV_W), seq_b),
                 pl.BlockSpec((ns, WINDOW, KV_W), seq_b),
                 pl.BlockSpec((ns, CONV_W - 1, d_rnn), seq_b),
                 pl.BlockSpec((ns, 1, d_rnn), seq_b))
    win_rows = WINDOW + max(ls, 2 * CHUNK)
    scratch = [pltpu.VMEM((rows, d_model), _BF16),
               pltpu.VMEM((rows, d_model), _BF16),
               pltpu.VMEM((rows, N_HEADS * HEAD_DIM), _BF16),
               pltpu.VMEM((rows, KV_W), _F32),
               pltpu.VMEM((rows, KV_W), _F32),
               pltpu.VMEM((rows, d_rnn), _F32),
               pltpu.VMEM((rows, d_rnn), _F32),
               pltpu.VMEM((rows, d_model), _F32),
               pltpu.VMEM((rows, d_model), _F32),
               pltpu.VMEM((rows, d_rnn), _F32),
               pltpu.VMEM((rows, d_model), _F32),
               pltpu.VMEM((rows, d_model), _F32),
               pltpu.VMEM((ns, N_KV_HEADS, 2, win_rows, LANES), _BF16),
               pltpu.VMEM((ns, N_KV_HEADS, 2, win_rows, LANES), _BF16),
               pltpu.VMEM((ns, WINDOW, KV_W), _F32),
               pltpu.VMEM((ns, WINDOW, KV_W), _F32),
               pltpu.VMEM((ns, CONV_W - 1, d_rnn), _F32),
               pltpu.VMEM((N_KV_HEADS, 2, 2 * lq, KEY_WIN), _F32),
               pltpu.VMEM((ns, CONV_W - 1, d_rnn), _F32),
               pltpu.VMEM((1, d_rnn), _F32),
               pltpu.VMEM((rows, N_HEADS * HEAD_DIM), _BF16),
               pltpu.VMEM((rows, d_rnn), _BF16),
               pltpu.VMEM((rows, rows), _BF16),
               pltpu.VMEM((rows, rows), _BF16)]
    kern = functools.partial(_mixer_kernel, skew, carry, ns, ls, tiles_per_seq, d_model, d_rnn)
    windows = [(spec.block_shape, _F32) for spec in (*in_specs, *out_specs)
               if spec.block_shape is not None]
    vmem_limit = _vmem_limit((v1024, v1280, w_in, w_rg, w_ao, w_ro, w_out), windows, scratch,
                             MIXER_VALUE_TILES, rows, d_model)
    return pl.pallas_call(
        kern,
        grid=(n_tiles + (1 if skew else 0),),
        in_specs=in_specs,
        out_specs=out_specs,
        out_shape=out_shape,
        scratch_shapes=scratch,
        compiler_params=pltpu.CompilerParams(dimension_semantics=("arbitrary",),
                                             vmem_limit_bytes=vmem_limit),
        name="mixer_stream" if carry else "mixer_step",
    )(*args)


def _ffn_kernel(n_prompt, sample_seqs, sample_len, xp_ref, xs_ref, modp_ref, mods_ref, v1024_ref,
                wg_ref, wu_ref, wd_ref, yp_ref, ys_ref, u_ref):
    rows = xp_ref.shape[0]
    is_prompt = pl.program_id(0) < n_prompt

    @pl.when(is_prompt)
    def _():
        _ffn_tile(1, rows, xp_ref, modp_ref, v1024_ref, wg_ref, wu_ref, wd_ref, yp_ref, u_ref)

    @pl.when(jnp.logical_not(is_prompt))
    def _():
        _ffn_tile(sample_seqs, sample_len, xs_ref, mods_ref, v1024_ref, wg_ref, wu_ref, wd_ref,
                  ys_ref, u_ref)


def _ffn_tile(ns, ls, x_ref, mod_ref, v1024_ref, wg_ref, wu_ref, wd_ref, y_ref, u_ref):
    x = x_ref[...]
    xn = _rms(x)
    for s in range(ns):
        sl = slice(s * ls, (s + 1) * ls)
        gain = v1024_ref[0:1, :] * (1.0 + mod_ref[s, MOD_SCALE_FFN:MOD_SCALE_FFN + 1, :])
        u_ref[sl, :] = (xn[sl] * gain + mod_ref[s, MOD_SHIFT_FFN:MOD_SHIFT_FFN + 1, :]).astype(_BF16)
    u = u_ref[...]
    g = _dot(u, wg_ref[...].astype(_BF16))
    up = _dot(u, wu_ref[...].astype(_BF16))
    hmid = (_silu(g) * up).astype(_BF16)
    yn = _rms(_dot(hmid, wd_ref[...].astype(_BF16)))
    for s in range(ns):
        sl = slice(s * ls, (s + 1) * ls)
        gain = v1024_ref[1:2, :] * mod_ref[s, MOD_GATE_FFN:MOD_GATE_FFN + 1, :]
        y_ref[sl, :] = x[sl] + gain * yn[sl]


def _ffn_call(xp, xs, mod_p, mod_s, v1024, wg, wu, wd, *, rows, prompt_len, sample_len):
    d_model = xp.shape[1]
    assert prompt_len % rows == 0 and rows % sample_len == 0
    assert xp.shape[0] % rows == 0 and xs.shape[0] % rows == 0
    n_prompt, n_sample = xp.shape[0] // rows, xs.shape[0] // rows
    sample_seqs = rows // sample_len
    tiles_per_seq = prompt_len // rows
    tile_p = lambda i: jnp.minimum(i, n_prompt - 1)
    tile_s = lambda i: jnp.maximum(i - n_prompt, 0)
    row_block = pl.BlockSpec((rows, d_model), lambda i: (tile_p(i), 0))
    row_block_s = pl.BlockSpec((rows, d_model), lambda i: (tile_s(i), 0))
    scratch = [pltpu.VMEM((rows, d_model), _BF16)]
    windows = [((rows, d_model), _F32)] * 4 + [((1, 6, d_model), _F32), ((sample_seqs, 6, d_model), _F32)]
    value_tiles = FFN_VALUE_TILES + FFN_WIDE_VALUES * pl.cdiv(wg.shape[1], d_model)
    return pl.pallas_call(
        functools.partial(_ffn_kernel, n_prompt, sample_seqs, sample_len),
        grid=(n_prompt + n_sample,),
        in_specs=[row_block, row_block_s,
                  pl.BlockSpec((1, 6, d_model), lambda i: (tile_p(i) // tiles_per_seq, 0, 0)),
                  pl.BlockSpec((sample_seqs, 6, d_model), lambda i: (tile_s(i), 0, 0)),
                  _whole(), _whole(), _whole(), _whole()],
        out_specs=(row_block, row_block_s),
        out_shape=(jax.ShapeDtypeStruct(xp.shape, _F32), jax.ShapeDtypeStruct(xs.shape, _F32)),
        scratch_shapes=scratch,
        compiler_params=pltpu.CompilerParams(
            dimension_semantics=("arbitrary",),
            vmem_limit_bytes=_vmem_limit((v1024, wg, wu, wd), windows, scratch, value_tiles,
                                         rows, d_model)),
        name="ffn",
    )(xp, xs, mod_p, mod_s, v1024, wg, wu, wd)


MIX_TILE = 256
MIX_SEQS = 4
SKEW_PROMPT = True
SKEW_SAMPLE = False
FFN_TILE = 512


def kernel(x_prompt, x_sample, c_prompt, c_sample, cache_k, cache_v, state_conv, state_h, w_ada, b_ada, g_pre_mix, g_post_mix, w_in, attn_sinks, w_conv, b_conv, w_rg_a, b_rg_a, w_rg_x, b_rg_x, rg_lambda, w_attn_o, w_rnn_o, w_out, g_pre_ffn, g_post_ffn, w_ffn_gate, w_ffn_up, w_ffn_down):
    depth = w_in.shape[0]
    assert depth == 1
    bp, sp, d_model = x_prompt.shape
    bs, ss, _ = x_sample.shape
    d_rnn = w_conv.shape[-1]
    l = 0

    n_c = bp + bs
    pad = (-n_c) % 16
    c_all = jnp.concatenate([c_prompt, c_sample, jnp.zeros((pad, d_model), _F32)], axis=0)
    mod = _ada_call(c_all, w_ada[l], b_ada[l])[:n_c].reshape(n_c, 6, d_model)
    mod_p, mod_s = mod[:bp], mod[bp:]

    bf = lambda w: w.astype(_BF16)
    v1024_mix = jnp.stack([g_pre_mix[l], g_post_mix[l]])
    v1024_ffn = jnp.stack([g_pre_ffn[l], g_post_ffn[l]])
    v1280 = jnp.concatenate([w_conv[l], b_conv[l][None], b_rg_a[l][None], b_rg_x[l][None],
                             rg_lambda[l][None]], axis=0)
    w_rg = bf(jnp.concatenate([w_rg_a[l], w_rg_x[l]], axis=-1))
    mix_w = (attn_sinks[l], v1024_mix, v1280, _slabs(w_in[l]), w_rg, _slabs(w_attn_o[l]),
             _slabs(w_rnn_o[l]), _slabs(w_out[l]))
    ffn_w = (v1024_ffn, w_ffn_gate[l], w_ffn_up[l], w_ffn_down[l])

    xp1, kp, vp, cp, hp = _mixer_call(x_prompt, mod_p, None, *mix_w, ns=1, ls=MIX_TILE,
                                      skew=SKEW_PROMPT)
    hist = (cache_k[l].reshape(bs, WINDOW, KV_W), cache_v[l].reshape(bs, WINDOW, KV_W),
            state_conv[l], state_h[l].reshape(bs, 1, d_rnn))
    xs1, ks, vs, cs, hs = _mixer_call(x_sample, mod_s, hist, *mix_w, ns=MIX_SEQS, ls=ss,
                                      skew=SKEW_SAMPLE)

    yp, ys = _ffn_call(xp1, xs1, mod_p, mod_s, *ffn_w, rows=FFN_TILE, prompt_len=sp, sample_len=ss)

    kv_shape = lambda n: (1, n, WINDOW, N_KV_HEADS, HEAD_DIM)
    return (yp.reshape(bp, sp, d_model), ys.reshape(bs, ss, d_model),
            kp.reshape(kv_shape(bp)), vp.reshape(kv_shape(bp)), cp[None], hp.reshape(1, bp, d_rnn),
            ks.reshape(kv_shape(bs)), vs.reshape(kv_shape(bs)), cs[None], hs.reshape(1, bs, d_rnn))
```

```python
import functools
import math

import jax
import jax.numpy as jnp
from jax import lax
from jax.experimental import pallas as pl
from jax.experimental.pallas import tpu as pltpu

CHUNK = 64
N_HEADS = 16
N_KV_HEADS = 4
HEAD_DIM = 64
GROUP = N_HEADS // N_KV_HEADS
WINDOW = 128
N_BACK = WINDOW // CHUNK
KV_W = N_KV_HEADS * HEAD_DIM
RNN_BLOCK = 128
CONV_W = 4
LRU_C = 8.0
EPS = 1e-6
NEG_INF = -1e30

LANES = 128
SUBLANES = 8
MXU_COLS = 256
SLABS_PER_CAST_STEP = 4
KEY_WIN = WINDOW + 2 * CHUNK
MIXER_VALUE_TILES = 14
FFN_VALUE_TILES = 2
FFN_WIDE_VALUES = 1
LOG2E = math.log2(math.e)
ROW_B_CONV, ROW_B_A, ROW_B_X, ROW_LAMBDA = CONV_W, CONV_W + 1, CONV_W + 2, CONV_W + 3
MOD_SHIFT_MIX, MOD_SCALE_MIX, MOD_GATE_MIX, MOD_SHIFT_FFN, MOD_SCALE_FFN, MOD_GATE_FFN = range(6)

_F32 = jnp.float32
_BF16 = jnp.bfloat16


def _dot(a, b):
    return lax.dot_general(a, b, (((1,), (0,)), ((), ())), preferred_element_type=_F32)


def _dot_nt(a, b):
    return lax.dot_general(a, b, (((1,), (1,)), ((), ())), preferred_element_type=_F32)


def _rms(x):
    return x * lax.rsqrt(jnp.mean(x * x, axis=-1, keepdims=True) + EPS)


def _softplus(x):
    return jnp.maximum(x, 0.0) + jnp.log1p(jnp.exp(-jnp.abs(x)))


def _sigmoid(x):
    return 0.5 * jnp.tanh(0.5 * x) + 0.5


def _silu(x):
    h = 0.5 * x
    return h * jnp.tanh(h) + h


def _sqrt_nonneg(x):
    return jnp.exp2((0.5 * LOG2E) * jnp.log(x))


def _neg_expm1_twice(x):
    t = jnp.tanh(x)
    return (-2.0 * t) / (1.0 - t)


def _slab_cast_kernel(n_slabs, w_ref, o_ref):
    for j in range(n_slabs):
        o_ref[j] = w_ref[:, j * MXU_COLS:(j + 1) * MXU_COLS].astype(_BF16)


def _slabs(w):
    k, n = w.shape
    total = n // MXU_COLS
    per_step = math.gcd(total, SLABS_PER_CAST_STEP)
    return pl.pallas_call(
        functools.partial(_slab_cast_kernel, per_step),
        grid=(total // per_step,),
        in_specs=[pl.BlockSpec((k, per_step * MXU_COLS), lambda j: (0, j))],
        out_specs=pl.BlockSpec((per_step, k, MXU_COLS), lambda j: (j, 0, 0)),
        out_shape=jax.ShapeDtypeStruct((total, k, MXU_COLS), _BF16),
        compiler_params=pltpu.CompilerParams(dimension_semantics=("arbitrary",)),
        name="slab_cast",
    )(w)


def _gelu_tanh(x):
    c = math.sqrt(2.0 / math.pi)
    hx = 0.5 * x
    return hx + hx * jnp.tanh(x * (c + (c * 0.044715) * (x * x)))


def _split_heads(kv, scale=None):
    n = kv.shape[0]
    lo = lax.broadcasted_iota(jnp.int32, (n, LANES), 1) < HEAD_DIM
    outs = []
    for blk in range(KV_W // LANES):
        b = kv[:, blk * LANES:(blk + 1) * LANES]
        if scale is not None:
            b = b * scale
        r = pltpu.roll(b, HEAD_DIM, axis=1)
        outs.append((jnp.where(lo, b, 0.0).astype(_BF16), jnp.where(lo, 0.0, r).astype(_BF16)))
        outs.append((jnp.where(lo, r, 0.0).astype(_BF16), jnp.where(lo, 0.0, b).astype(_BF16)))
    return outs


def _ada_kernel(c_ref, w_ref, b_ref, o_ref):
    c = c_ref[...]
    a = _silu(c).astype(_BF16)
    o_ref[...] = _dot(a, w_ref[...].astype(_BF16)) + b_ref[...]


def _ada_call(c, w_ada, b_ada):
    rows, d = c.shape
    n = w_ada.shape[1]
    bn = d
    return pl.pallas_call(
        _ada_kernel,
        grid=(n // bn,),
        in_specs=[
            pl.BlockSpec((rows, d), lambda j: (0, 0)),
            pl.BlockSpec((d, bn), lambda j: (0, j)),
            pl.BlockSpec((1, bn), lambda j: (0, j)),
        ],
        out_specs=pl.BlockSpec((rows, bn), lambda j: (0, j)),
        out_shape=jax.ShapeDtypeStruct((rows, n), _F32),
        compiler_params=pltpu.CompilerParams(dimension_semantics=("arbitrary",)),
        name="ada",
    )(c, w_ada, b_ada.reshape(1, n))


def _mixer_kernel(skew, carry, ns, ls, tiles_per_seq, d_model, d_rnn, *refs):
    if skew:
        xa_ref, xb_ref, moda_ref, modb_ref = refs[:4]
        refs = refs[4:]
    else:
        xa_ref, moda_ref = xb_ref, modb_ref = refs[:2]
        refs = refs[2:]
    n_in = 0 if carry else 4
    if not carry:
        ck_ref, cv_ref, c0_ref, h0_ref = refs[:4]
    (sinks_ref, v1024_ref, v1280_ref, w_in_ref, w_rg_ref, w_ao_ref, w_ro_ref, w_out_ref,
     y_ref, ks_ref, vs_ref, cs_ref, hs_ref,
     u_ref, us_ref, q_ref, k_ref, v_ref, xr_ref, yr_ref, ga_ref, gr_ref, xc_ref, sga_ref, sgr_ref,
     kwin_ref, vwin_ref, kst_ref, vst_ref, cst_ref, bias_ref, hist_ref, hcar_ref,
     attn_ref, rnn_ref, to_strand_ref, from_strand_ref) = refs[n_in:]

    i = pl.program_id(0)
    rows = ns * ls
    lq = min(ls, 2 * CHUNK)
    n_win = ls // lq
    q_w = N_HEADS * HEAD_DIM
    n_blk = d_rnn // RNN_BLOCK
    steps = ls // SUBLANES
    if carry:
        mixed = i + tiles_per_seq - 1 if skew else i
        keep_mix = jnp.where(lax.rem(mixed, tiles_per_seq) == 0, 0.0, 1.0)
        keep_prep = jnp.where(lax.rem(i, tiles_per_seq) == 0, 0.0, 1.0)

    @pl.when(i == 0)
    def _():
        for ref in (q_ref, yr_ref, xc_ref, sga_ref, sgr_ref, kwin_ref, vwin_ref, kst_ref,
                    vst_ref, cst_ref, hist_ref, hcar_ref):
            ref[...] = jnp.zeros(ref.shape, ref.dtype)
        r = lax.broadcasted_iota(jnp.int32, (rows, rows), 0)
        c = lax.broadcasted_iota(jnp.int32, (rows, rows), 1)
        seg, rr = (r // ls) * ls, r % ls
        to_strand_ref[...] = jnp.where(
            c == seg + (rr % SUBLANES) * steps + rr // SUBLANES, 1.0, 0.0).astype(_BF16)
        from_strand_ref[...] = jnp.where(
            c == seg + (rr % steps) * SUBLANES + rr // steps, 1.0, 0.0).astype(_BF16)
        t = lax.broadcasted_iota(jnp.int32, (lq, KEY_WIN), 0)
        j = lax.broadcasted_iota(jnp.int32, (lq, KEY_WIN), 1)
        dist = jnp.abs(t + WINDOW - j).astype(_F32)
        qc = t // CHUNK
        kc = j // CHUNK
        base = jnp.where(kc >= qc, jnp.where(kc <= qc + N_BACK, 0.0, NEG_INF), NEG_INF)
        for h in range(N_KV_HEADS):
            for g in range(GROUP):
                slope = 2.0 ** (-8.0 * (h * GROUP + g + 1) / N_HEADS)
                rows_g = slice((g // 2) * lq, (g // 2 + 1) * lq)
                bias_ref[0, h, g % 2, rows_g, :] = LOG2E * (base - slope * dist)
                if carry:
                    bias_ref[1, h, g % 2, rows_g, :] = jnp.where(
                        j < WINDOW, NEG_INF, LOG2E * (base - slope * dist))

    def publish_state():
        ks_ref[...] = kst_ref[...]
        vs_ref[...] = vst_ref[...]
        cs_ref[...] = cst_ref[...]

    def norm_input():
        xn = _rms(xa_ref[...])
        for s in range(ns):
            sl = slice(s * ls, (s + 1) * ls)
            gain = v1024_ref[0:1, :] * (1.0 + moda_ref[s, MOD_SCALE_MIX:MOD_SCALE_MIX + 1, :])
            u_ref[sl, :] = (xn[sl] * gain + moda_ref[s, MOD_SHIFT_MIX:MOD_SHIFT_MIX + 1, :]).astype(_BF16)
        us_ref[...] = _dot(to_strand_ref[...], u_ref[...]).astype(_BF16)

    proj = []
    off = 0
    for src, ref, width, free_after in (
            (u_ref, q_ref, q_w, lambda c: 2 * (c // MXU_COLS) + 1),
            (u_ref, k_ref, KV_W, lambda c: -1), (u_ref, v_ref, KV_W, lambda c: -1),
            (us_ref, xr_ref, d_rnn, lambda c: -1),
            (us_ref, yr_ref, d_rnn, lambda c: (c + MXU_COLS - 1) // RNN_BLOCK),
            (u_ref, ga_ref, d_model, lambda c: -1), (u_ref, gr_ref, d_model, lambda c: -1)):
        for c in range(0, width, MXU_COLS):
            proj.append((free_after(c), src, ref, c, off + c))
        off += width
    proj.sort(key=lambda p: p[0])

    def emit_proj(slot, slots_left):
        ready = [p for p in proj if p[0] <= slot]
        quota = -(-len(proj) // slots_left) if slots_left else len(proj)
        for p in ready[:quota]:
            proj.remove(p)
            _, src, ref, c, wc = p
            ref[:, c:c + MXU_COLS] = _dot(src[...], w_in_ref[wc // MXU_COLS]).astype(ref.dtype)

    lo_sel = lax.broadcasted_iota(jnp.int32, (lq, LANES), 1) < HEAD_DIM
    if carry:
        start_table = jnp.where(lax.rem(mixed, tiles_per_seq) == 0, 1, 0)

    def attend_scores(s, w, h):
        r0 = s * ls + w * lq
        kw = slice(w * lq, w * lq + KEY_WIN)
        c0 = h * GROUP * HEAD_DIM
        qab = jnp.concatenate([q_ref[r0:r0 + lq, c0:c0 + LANES],
                               q_ref[r0:r0 + lq, c0 + LANES:c0 + 2 * LANES]], axis=0)
        ps, inv = [[], []], [None] * GROUP
        for half in range(2):
            table = start_table if carry and w == 0 else 0
            sc = _dot_nt(qab, kwin_ref[s, h, half, kw, :]) + bias_ref[table, h, half]
            for pair in range(2):
                g = 2 * pair + half
                sg = sc[pair * lq:(pair + 1) * lq]
                sink = LOG2E * sinks_ref[h * GROUP + g]
                mg = jnp.maximum(jnp.max(sg, axis=-1, keepdims=True), sink)
                pg = jnp.exp2(sg - mg)
                inv[g] = 1.0 / (jnp.sum(pg, axis=-1, keepdims=True) + jnp.exp2(sink - mg))
                ps[half].append(pg.astype(_BF16))
        return s, h, r0, kw, c0, ps, inv

    def attend_values(s, h, r0, kw, c0, ps, inv):
        o = (_dot(jnp.concatenate(ps[0], axis=0), vwin_ref[s, h, 0, kw, :])
             + _dot(jnp.concatenate(ps[1], axis=0), vwin_ref[s, h, 1, kw, :]))
        for pair in range(2):
            norm = jnp.where(lo_sel, inv[2 * pair], inv[2 * pair + 1])
            attn_ref[r0:r0 + lq, c0 + pair * LANES:c0 + (pair + 1) * LANES] = (
                o[pair * lq:(pair + 1) * lq] * norm).astype(_BF16)

    units = [(s, w, h) for h in range(N_KV_HEADS) for s in range(ns) for w in range(n_win)]
    units_per_slot = len(units) // (2 * N_KV_HEADS)

    sub = lax.broadcasted_iota(jnp.int32, (SUBLANES, RNN_BLOCK), 0)

    def recur_gates(n):
        half_xc = 0.5 * xc_ref[:, n * RNN_BLOCK:(n + 1) * RNN_BLOCK]
        return half_xc, _dot(half_xc.astype(_BF16), w_rg_ref[n])

    def recur(n, half_xc, half_gates):
        cs_ = slice(n * RNN_BLOCK, (n + 1) * RNN_BLOCK)
        half_c = (-0.5 * LRU_C) * _softplus(-v1280_ref[ROW_LAMBDA:ROW_LAMBDA + 1, cs_])
        half_ba = 0.5 * v1280_ref[ROW_B_A:ROW_B_A + 1, cs_]
        half_bx = 0.5 * v1280_ref[ROW_B_X:ROW_B_X + 1, cs_]
        log_a = half_c * jnp.tanh(half_gates[:, :RNN_BLOCK] + half_ba) + half_c
        gated_x = jnp.tanh(half_gates[:, RNN_BLOCK:] + half_bx) * half_xc + half_xc
        a = jnp.exp(log_a)
        b = _sqrt_nonneg(_neg_expm1_twice(log_a)) * gated_x
        hs = []
        for s in range(ns):
            sl = slice(s * ls, (s + 1) * ls)
            a3 = a[sl].reshape(steps, SUBLANES, RNN_BLOCK)
            b3 = b[sl].reshape(steps, SUBLANES, RNN_BLOCK)
            hz, ap = [b3[0]], [a3[0]]
            for jb in range(1, steps):
                hz.append(a3[jb] * hz[-1] + b3[jb])
                ap.append(a3[jb] * ap[-1])
            ae, he = ap[-1], hz[-1]
            d = 1
            while d < SUBLANES:
                keep = sub >= d
                he = he + ae * jnp.where(keep, pltpu.roll(he, d, axis=0), 0.0)
                ae = ae * jnp.where(keep, pltpu.roll(ae, d, axis=0), 1.0)
                d *= 2
            hprev = hcar_ref[0:1, cs_] * keep_mix if carry else h0_ref[s, 0:1, cs_]
            after = he + ae * hprev
            before = jnp.where(sub == 0, hprev, pltpu.roll(after, 1, axis=0))
            hs.extend(hz[jb] + ap[jb] * before for jb in range(steps))
            hlast = after[SUBLANES - 1:SUBLANES, :]
            hs_ref[s, 0:1, cs_] = hlast
            if carry:
                hcar_ref[0:1, cs_] = hlast
        hfull = jnp.concatenate(hs, axis=0)
        rnn_ref[:, cs_] = (hfull * _gelu_tanh(yr_ref[:, cs_])).astype(_BF16)

    def next_scores():
        return [attend_scores(*units.pop(0)) for _ in range(min(units_per_slot, len(units)))]

    def main_loop(with_proj):
        scored = next_scores()
        gates = recur_gates(0)
        ao = None
        for n in range(n_blk):
            scored_next = next_scores()
            if with_proj:
                emit_proj(n - 1, n_blk - n)
            gates_next = recur_gates(n + 1) if n + 1 < n_blk else None
            recur(n, *gates)
            for unit in scored:
                attend_values(*unit)
            if ao is None and not (scored or scored_next or units):
                ao = attention_projection()
            scored, gates = scored_next, gates_next
        for unit in scored:
            attend_values(*unit)
        if with_proj:
            emit_proj(n_blk - 1, 1)
        assert not units
        return ao if ao is not None else attention_projection()

    n_slab = d_model // MXU_COLS

    def attention_projection():
        return [_dot(attn_ref[...], w_ao_ref[j]) for j in range(n_slab)]

    def recurrent_projection():
        rnn = _dot(from_strand_ref[...], rnn_ref[...]).astype(_BF16)
        return [_dot(rnn, w_ro_ref[j]) for j in range(n_slab)]

    def prepare_window_and_conv():
        qk_scale = LOG2E * HEAD_DIM ** -0.5
        if carry:
            kwin_ref[0, :, :, 0:WINDOW, :] = kwin_ref[0, :, :, ls:ls + WINDOW, :]
            vwin_ref[0, :, :, 0:WINDOW, :] = vwin_ref[0, :, :, ls:ls + WINDOW, :]
        for s in range(ns):
            sl = slice(s * ls, (s + 1) * ls)
            kd = _split_heads(k_ref[sl, :], qk_scale)
            vd = _split_heads(v_ref[sl, :])
            for h in range(N_KV_HEADS):
                for half in range(2):
                    kwin_ref[s, h, half, WINDOW:WINDOW + ls, :] = kd[h][half]
                    vwin_ref[s, h, half, WINDOW:WINDOW + ls, :] = vd[h][half]
            if carry:
                kst_ref[0] = k_ref[rows - WINDOW:rows, :]
                vst_ref[0] = v_ref[rows - WINDOW:rows, :]
            else:
                hk = _split_heads(ck_ref[s], qk_scale)
                hv = _split_heads(cv_ref[s])
                pad = KEY_WIN - WINDOW - ls
                for h in range(N_KV_HEADS):
                    for half in range(2):
                        kwin_ref[s, h, half, 0:WINDOW, :] = hk[h][half]
                        vwin_ref[s, h, half, 0:WINDOW, :] = hv[h][half]
                        kwin_ref[s, h, half, WINDOW + ls:KEY_WIN, :] = jnp.zeros((pad, LANES), _BF16)
                        vwin_ref[s, h, half, WINDOW + ls:KEY_WIN, :] = jnp.zeros((pad, LANES), _BF16)
                kst_ref[s, 0:WINDOW - ls, :] = ck_ref[s, ls:WINDOW, :]
                kst_ref[s, WINDOW - ls:WINDOW, :] = k_ref[sl, :]
                vst_ref[s, 0:WINDOW - ls, :] = cv_ref[s, ls:WINDOW, :]
                vst_ref[s, WINDOW - ls:WINDOW, :] = v_ref[sl, :]

            hist = c0_ref[s] if not carry else hist_ref[0] * keep_prep
            first = lax.broadcasted_iota(jnp.int32, (SUBLANES, LANES), 0) == 0
            for col in range(0, d_rnn, LANES):
                lanes = slice(col, col + LANES)
                taps = [jnp.broadcast_to(v1280_ref[r:r + 1, lanes], (SUBLANES, LANES))
                        for r in range(CONV_W + 1)]
                blocks = [xr_ref[s * ls + jb * SUBLANES:s * ls + (jb + 1) * SUBLANES, lanes]
                          for jb in range(steps)]

                def earlier(jb, d):
                    if jb >= d:
                        return blocks[jb - d]
                    row = CONV_W - 1 + jb - d
                    prev_strand = pltpu.roll(blocks[steps + jb - d], 1, axis=0)
                    return jnp.where(first, hist[row:row + 1, lanes], prev_strand)

                for jb in range(steps):
                    acc = taps[ROW_B_CONV] + taps[CONV_W - 1] * blocks[jb]
                    for d in range(1, CONV_W):
                        acc = acc + taps[CONV_W - 1 - d] * earlier(jb, d)
                    xc_ref[s * ls + jb * SUBLANES:s * ls + (jb + 1) * SUBLANES, lanes] = acc
            tail = jnp.concatenate(
                [xr_ref[s * ls + (steps - d + 1) * SUBLANES - 1:s * ls + (steps - d + 1) * SUBLANES, :]
                 for d in range(CONV_W - 1, 0, -1)], axis=0)
            cst_ref[s] = tail
            if carry:
                hist_ref[s] = tail

    def output_projection(ao, ro):
        merged = jnp.concatenate(
            [(sga_ref[:, j * MXU_COLS:(j + 1) * MXU_COLS] * ao[j]
              + sgr_ref[:, j * MXU_COLS:(j + 1) * MXU_COLS] * ro[j]).astype(_BF16)
             for j in range(n_slab)], axis=1)
        return jnp.concatenate([_dot(merged, w_out_ref[j]) for j in range(n_slab)], axis=1)

    def gate_sigmoids():
        sga_ref[...] = _sigmoid(ga_ref[...])
        sgr_ref[...] = _sigmoid(gr_ref[...])

    def residual(mo):
        yn = _rms(mo)
        for s in range(ns):
            sl = slice(s * ls, (s + 1) * ls)
            gain = v1024_ref[1:2, :] * modb_ref[s, MOD_GATE_MIX:MOD_GATE_MIX + 1, :]
            y_ref[sl, :] = xb_ref[sl, :] + gain * yn[sl]

    if skew:
        publish_state()
        norm_input()
        ao = main_loop(with_proj=True)
        ro = recurrent_projection()
        prepare_window_and_conv()
        mo = output_projection(ao, ro)
        gate_sigmoids()
        residual(mo)
        emit_proj(n_blk, 0)
    else:
        norm_input()
        emit_proj(n_blk, 0)
        prepare_window_and_conv()
        gate_sigmoids()
        ao = main_loop(with_proj=False)
        residual(output_projection(ao, recurrent_projection()))
        publish_state()
    assert not proj


def _whole(memory_space=pltpu.VMEM):
    return pl.BlockSpec(memory_space=memory_space)


def _nbytes(shape, dtype):
    return math.prod(shape) * jnp.dtype(dtype).itemsize


def _vmem_limit(resident, windows, scratch, value_tiles, rows, d_model):
    total = sum(_nbytes(a.shape, a.dtype) for a in resident)
    total += 2 * sum(_nbytes(shape, dtype) for shape, dtype in windows)
    total += sum(_nbytes(s.shape, s.dtype) for s in scratch)
    return total + value_tiles * _nbytes((rows, d_model), _F32)


def _mixer_call(x, mod, hist, sinks, v1024, v1280, w_in, w_rg, w_ao, w_ro, w_out, *, ns, ls, skew):
    n_seq, seq_len, d_model = x.shape
    d_rnn = v1280.shape[1]
    carry = hist is None
    rows = ns * ls
    if carry:
        assert ns == 1 and seq_len % ls == 0 and ls % (2 * CHUNK) == 0
        tiles_per_seq = seq_len // ls
    else:
        assert seq_len == ls == CHUNK and n_seq % ns == 0
        tiles_per_seq = 1
    n_tiles = n_seq * seq_len // rows
    lq = min(ls, 2 * CHUNK)
    xf = x.reshape(n_seq * seq_len, d_model)

    if skew:
        tile_a = lambda i: jnp.minimum(i, n_tiles - 1)
        tile_b = lambda i: jnp.maximum(i - 1, 0)
    else:
        tile_a = tile_b = lambda i: i
    seq_a = lambda i: (tile_a(i) // tiles_per_seq, 0, 0)
    seq_b = lambda i: (tile_b(i) // tiles_per_seq, 0, 0)

    in_specs = [pl.BlockSpec((rows, d_model), lambda i: (tile_a(i), 0)),
                pl.BlockSpec((ns, 6, d_model), seq_a)]
    args = [xf, mod]
    if skew:
        in_specs = [in_specs[0], pl.BlockSpec((rows, d_model), lambda i: (tile_b(i), 0)),
                    in_specs[1], pl.BlockSpec((ns, 6, d_model), seq_b)]
        args = [xf, xf, mod, mod]
    if not carry:
        ck, cv, c0, h0 = hist
        in_specs += [pl.BlockSpec((ns, WINDOW, KV_W), seq_a),
                     pl.BlockSpec((ns, WINDOW, KV_W), seq_a),
                     pl.BlockSpec((ns, CONV_W - 1, d_rnn), seq_a),
                     pl.BlockSpec((ns, 1, d_rnn), seq_b)]
        args += [ck, cv, c0, h0]
    in_specs += [_whole(pltpu.SMEM)] + [_whole()] * 7
    args += [sinks, v1024, v1280, w_in, w_rg, w_ao, w_ro, w_out]

    out_shape = (jax.ShapeDtypeStruct((n_seq * seq_len, d_model), _F32),
                 jax.ShapeDtypeStruct((n_seq, WINDOW, KV_W), _F32),
                 jax.ShapeDtypeStruct((n_seq, WINDOW, KV_W), _F32),
                 jax.ShapeDtypeStruct((n_seq, CONV_W - 1, d_rnn), _F32),
                 jax.ShapeDtypeStruct((n_seq, 1, d_rnn), _F32))
    out_specs = (pl.BlockSpec((rows, d_model), lambda i: (tile_b(i), 0)),
                 pl.BlockSpec((ns, WINDOW, KV_W), seq_b),
                 pl.BlockSpec((ns, WINDOW, KV_W), seq_b),
                 pl.BlockSpec((ns, CONV_W - 1, d_rnn), seq_b),
                 pl.BlockSpec((ns, 1, d_rnn), seq_b))
    win_rows = WINDOW + max(ls, 2 * CHUNK)
    scratch = [pltpu.VMEM((rows, d_model), _BF16),
               pltpu.VMEM((rows, d_model), _BF16),
               pltpu.VMEM((rows, N_HEADS * HEAD_DIM), _BF16),
               pltpu.VMEM((rows, KV_W), _F32),
               pltpu.VMEM((rows, KV_W), _F32),
               pltpu.VMEM((rows, d_rnn), _F32),
               pltpu.VMEM((rows, d_rnn), _F32),
               pltpu.VMEM((rows, d_model), _F32),
               pltpu.VMEM((rows, d_model), _F32),
               pltpu.VMEM((rows, d_rnn), _F32),
               pltpu.VMEM((rows, d_model), _F32),
               pltpu.VMEM((rows, d_model), _F32),
               pltpu.VMEM((ns, N_KV_HEADS, 2, win_rows, LANES), _BF16),
               pltpu.VMEM((ns, N_KV_HEADS, 2, win_rows, LANES), _BF16),
               pltpu.VMEM((ns, WINDOW, KV_W), _F32),
               pltpu.VMEM((ns, WINDOW, KV_W), _F32),
               pltpu.VMEM((ns, CONV_W - 1, d_rnn), _F32),
               pltpu.VMEM((2 if carry else 1, N_KV_HEADS, 2, 2 * lq, KEY_WIN), _F32),
               pltpu.VMEM((ns, CONV_W - 1, d_rnn), _F32),
               pltpu.VMEM((1, d_rnn), _F32),
               pltpu.VMEM((rows, N_HEADS * HEAD_DIM), _BF16),
               pltpu.VMEM((rows, d_rnn), _BF16),
               pltpu.VMEM((rows, rows), _BF16),
               pltpu.VMEM((rows, rows), _BF16)]
    kern = functools.partial(_mixer_kernel, skew, carry, ns, ls, tiles_per_seq, d_model, d_rnn)
    windows = [(spec.block_shape, _F32) for spec in (*in_specs, *out_specs)
               if spec.block_shape is not None]
    vmem_limit = _vmem_limit((v1024, v1280, w_in, w_rg, w_ao, w_ro, w_out), windows, scratch,
                             MIXER_VALUE_TILES, rows, d_model)
    return pl.pallas_call(
        kern,
        grid=(n_tiles + (1 if skew else 0),),
        in_specs=in_specs,
        out_specs=out_specs,
        out_shape=out_shape,
        scratch_shapes=scratch,
        compiler_params=pltpu.CompilerParams(dimension_semantics=("arbitrary",),
                                             vmem_limit_bytes=vmem_limit),
        name="mixer_stream" if carry else "mixer_step",
    )(*args)


def _ffn_kernel(n_prompt, sample_seqs, sample_len, xp_ref, xs_ref, modp_ref, mods_ref, v1024_ref,
                wg_ref, wu_ref, wd_ref, yp_ref, ys_ref, u_ref):
    rows = xp_ref.shape[0]
    is_prompt = pl.program_id(0) < n_prompt

    @pl.when(is_prompt)
    def _():
        _ffn_tile(1, rows, xp_ref, modp_ref, v1024_ref, wg_ref, wu_ref, wd_ref, yp_ref, u_ref)

    @pl.when(jnp.logical_not(is_prompt))
    def _():
        _ffn_tile(sample_seqs, sample_len, xs_ref, mods_ref, v1024_ref, wg_ref, wu_ref, wd_ref,
                  ys_ref, u_ref)


def _ffn_tile(ns, ls, x_ref, mod_ref, v1024_ref, wg_ref, wu_ref, wd_ref, y_ref, u_ref):
    x = x_ref[...]
    xn = _rms(x)
    for s in range(ns):
        sl = slice(s * ls, (s + 1) * ls)
        gain = v1024_ref[0:1, :] * (1.0 + mod_ref[s, MOD_SCALE_FFN:MOD_SCALE_FFN + 1, :])
        u_ref[sl, :] = (xn[sl] * gain + mod_ref[s, MOD_SHIFT_FFN:MOD_SHIFT_FFN + 1, :]).astype(_BF16)
    u = u_ref[...]
    g = _dot(u, wg_ref[...].astype(_BF16))
    up = _dot(u, wu_ref[...].astype(_BF16))
    hmid = (_silu(g) * up).astype(_BF16)
    yn = _rms(_dot(hmid, wd_ref[...].astype(_BF16)))
    for s in range(ns):
        sl = slice(s * ls, (s + 1) * ls)
        gain = v1024_ref[1:2, :] * mod_ref[s, MOD_GATE_FFN:MOD_GATE_FFN + 1, :]
        y_ref[sl, :] = x[sl] + gain * yn[sl]


def _ffn_call(xp, xs, mod_p, mod_s, v1024, wg, wu, wd, *, rows, prompt_len, sample_len):
    d_model = xp.shape[1]
    assert prompt_len % rows == 0 and rows % sample_len == 0
    assert xp.shape[0] % rows == 0 and xs.shape[0] % rows == 0
    n_prompt, n_sample = xp.shape[0] // rows, xs.shape[0] // rows
    sample_seqs = rows // sample_len
    tiles_per_seq = prompt_len // rows
    tile_p = lambda i: jnp.minimum(i, n_prompt - 1)
    tile_s = lambda i: jnp.maximum(i - n_prompt, 0)
    row_block = pl.BlockSpec((rows, d_model), lambda i: (tile_p(i), 0))
    row_block_s = pl.BlockSpec((rows, d_model), lambda i: (tile_s(i), 0))
    scratch = [pltpu.VMEM((rows, d_model), _BF16)]
    windows = [((rows, d_model), _F32)] * 4 + [((1, 6, d_model), _F32), ((sample_seqs, 6, d_model), _F32)]
    value_tiles = FFN_VALUE_TILES + FFN_WIDE_VALUES * pl.cdiv(wg.shape[1], d_model)
    return pl.pallas_call(
        functools.partial(_ffn_kernel, n_prompt, sample_seqs, sample_len),
        grid=(n_prompt + n_sample,),
        in_specs=[row_block, row_block_s,
                  pl.BlockSpec((1, 6, d_model), lambda i: (tile_p(i) // tiles_per_seq, 0, 0)),
                  pl.BlockSpec((sample_seqs, 6, d_model), lambda i: (tile_s(i), 0, 0)),
                  _whole(), _whole(), _whole(), _whole()],
        out_specs=(row_block, row_block_s),
        out_shape=(jax.ShapeDtypeStruct(xp.shape, _F32), jax.ShapeDtypeStruct(xs.shape, _F32)),
        scratch_shapes=scratch,
        compiler_params=pltpu.CompilerParams(
            dimension_semantics=("arbitrary",),
            vmem_limit_bytes=_vmem_limit((v1024, wg, wu, wd), windows, scratch, value_tiles,
                                         rows, d_model)),
        name="ffn",
    )(xp, xs, mod_p, mod_s, v1024, wg, wu, wd)


MIX_TILE = 256
MIX_SEQS = 4
SKEW_PROMPT = True
SKEW_SAMPLE = False
FFN_TILE = 512


def kernel(x_prompt, x_sample, c_prompt, c_sample, cache_k, cache_v, state_conv, state_h, w_ada, b_ada, g_pre_mix, g_post_mix, w_in, attn_sinks, w_conv, b_conv, w_rg_a, b_rg_a, w_rg_x, b_rg_x, rg_lambda, w_attn_o, w_rnn_o, w_out, g_pre_ffn, g_post_ffn, w_ffn_gate, w_ffn_up, w_ffn_down):
    depth = w_in.shape[0]
    assert depth == 1
    bp, sp, d_model = x_prompt.shape
    bs, ss, _ = x_sample.shape
    d_rnn = w_conv.shape[-1]
    l = 0

    n_c = bp + bs
    pad = (-n_c) % 16
    c_all = jnp.concatenate([c_prompt, c_sample, jnp.zeros((pad, d_model), _F32)], axis=0)
    mod = _ada_call(c_all, w_ada[l], b_ada[l])[:n_c].reshape(n_c, 6, d_model)
    mod_p, mod_s = mod[:bp], mod[bp:]

    bf = lambda w: w.astype(_BF16)
    v1024_mix = jnp.stack([g_pre_mix[l], g_post_mix[l]])
    v1024_ffn = jnp.stack([g_pre_ffn[l], g_post_ffn[l]])
    v1280 = jnp.concatenate([w_conv[l], b_conv[l][None], b_rg_a[l][None], b_rg_x[l][None],
                             rg_lambda[l][None]], axis=0)
    w_rg = bf(jnp.concatenate([w_rg_a[l], w_rg_x[l]], axis=-1))
    mix_w = (attn_sinks[l], v1024_mix, v1280, _slabs(w_in[l]), w_rg, _slabs(w_attn_o[l]),
             _slabs(w_rnn_o[l]), _slabs(w_out[l]))
    ffn_w = (v1024_ffn, w_ffn_gate[l], w_ffn_up[l], w_ffn_down[l])

    xp1, kp, vp, cp, hp = _mixer_call(x_prompt, mod_p, None, *mix_w, ns=1, ls=MIX_TILE,
                                      skew=SKEW_PROMPT)
    hist = (cache_k[l].reshape(bs, WINDOW, KV_W), cache_v[l].reshape(bs, WINDOW, KV_W),
            state_conv[l], state_h[l].reshape(bs, 1, d_rnn))
    xs1, ks, vs, cs, hs = _mixer_call(x_sample, mod_s, hist, *mix_w, ns=MIX_SEQS, ls=ss,
                                      skew=SKEW_SAMPLE)

    yp, ys = _ffn_call(xp1, xs1, mod_p, mod_s, *ffn_w, rows=FFN_TILE, prompt_len=sp, sample_len=ss)

    kv_shape = lambda n: (1, n, WINDOW, N_KV_HEADS, HEAD_DIM)
    return (yp.reshape(bp, sp, d_model), ys.reshape(bs, ss, d_model),
            kp.reshape(kv_shape(bp)), vp.reshape(kv_shape(bp)), cp[None], hp.reshape(1, bp, d_rnn),
            ks.reshape(kv_shape(bs)), vs.reshape(kv_shape(bs)), cs[None], hs.reshape(1, bs, d_rnn))
```
